```python
import math
import jax
import jax.numpy as jnp
from jax import lax
import numpy as np

D_MODEL = 2048
BATCH = 8
SEQ = 2048
DEPTH = 4

HEAD_DIM = 128
DN_HEADS = 6
DIL_HEADS = 4
NSA_HEADS = 6
DN_WIDTH = DN_HEADS * HEAD_DIM
DIL_WIDTH = DIL_HEADS * HEAD_DIM
NSA_WIDTH = NSA_HEADS * HEAD_DIM
MIX_WIDTH = DN_WIDTH + DIL_WIDTH + NSA_WIDTH
DN_CONV = 4
DN_CHUNK = 64
DT_MIN = 1e-3
DT_MAX = 1e-1
DIL_PATTERNS = ((128, 1), (512, 4), (2048, 16))
NSA_KV_GROUPS = 2
NSA_Q_PER_KV = NSA_HEADS // NSA_KV_GROUPS
CMP_LEN = 32
CMP_STRIDE = 16
CMP_HIDDEN = 256
SLC_BLOCK = 64
SLC_TOPK = 16
SLC_LOCAL = 2
SLC_Q_BLOCK = 32
WIN = 512
NSA_KV_COLS = 3 * 2 * NSA_KV_GROUPS * HEAD_DIM
NSA_GATE_COLS = 3 * NSA_HEADS
IN_COLS = 3 * DN_WIDTH + DN_WIDTH + 2 * DN_HEADS + 3 * DIL_WIDTH + NSA_WIDTH + NSA_KV_COLS + NSA_GATE_COLS
REL_BUCKETS = 32
REL_MAX_DIST = 2048
N_BIAS_HEADS = DIL_HEADS + NSA_HEADS
D_FF = 5632
FFN_CONV = 3
Q_BLOCK = 128
EPS = 1e-6
NEG = -1e30

kernel_name = 'hybrid_deltanet_dilated_nsa_convffn'


def rmsnorm(x, g):
    xf = x.astype(jnp.float32)
    y = xf * lax.rsqrt(jnp.mean(xf * xf, axis=-1, keepdims=True) + EPS)
    return (y * g.astype(jnp.float32)).astype(x.dtype)


def l2norm(x):
    return x * lax.rsqrt(jnp.sum(x * x, axis=-1, keepdims=True) + EPS)


def causal_dwconv(x, w):
    k_w, c = w.shape
    return lax.conv_general_dilated(x, w[:, None, :].astype(x.dtype), window_strides=(1,),
                                    padding=[(k_w - 1, 0)], dimension_numbers=('NWC', 'WIO', 'NWC'),
                                    feature_group_count=c)


def rel_bucket(dist):
    max_exact = REL_BUCKETS // 2
    n = jnp.maximum(dist, 0)
    nf = jnp.maximum(n, 1).astype(jnp.float32)
    large = max_exact + (jnp.log(nf / max_exact) / math.log(REL_MAX_DIST / max_exact)
                         * (REL_BUCKETS - max_exact)).astype(jnp.int32)
    large = jnp.minimum(large, REL_BUCKETS - 1)
    return jnp.where(n < max_exact, n, large)


def masked_softmax(logits, mask):
    logits = jnp.where(mask, logits.astype(jnp.float32), NEG)
    m = jnp.max(logits, axis=-1, keepdims=True)
    p = jnp.exp(logits - m) * mask
    return p / jnp.maximum(jnp.sum(p, axis=-1, keepdims=True), 1e-30)


def split_cols(proj):
    sizes = (3 * DN_WIDTH, DN_WIDTH, DN_HEADS, DN_HEADS, 3 * DIL_WIDTH, NSA_WIDTH, NSA_KV_COLS, NSA_GATE_COLS)
    points = []
    acc = 0
    for s in sizes[:-1]:
        acc += s
        points.append(acc)
    return jnp.split(proj, points, axis=-1)


def chunk_gated_delta_rule(q, k, v, beta, g):
    B, H, S, dk = q.shape
    dv = v.shape[-1]
    C = DN_CHUNK
    N = S // C

    def to_chunks(t):
        return t.reshape((B, H, N, C) + t.shape[3:])

    q, k, v, beta, g = (to_chunks(t) for t in (q, k, v, beta, g))
    g = jnp.cumsum(g, axis=-1)
    k_beta = k * beta[..., None]
    v_beta = v * beta[..., None]
    causal = jnp.tril(jnp.ones((C, C), dtype=bool))
    decay = jnp.exp(jnp.where(causal, g[..., :, None] - g[..., None, :], -jnp.inf))
    eye = jnp.eye(C, dtype=q.dtype)
    a_mat = jnp.einsum('bhnid,bhnjd->bhnij', k_beta, k) * decay * (1.0 - eye)
    t_mat = lax.linalg.triangular_solve(eye + a_mat, jnp.broadcast_to(eye, a_mat.shape),
                                        left_side=True, lower=True, unit_diagonal=True)
    u = jnp.einsum('bhnij,bhnjd->bhnid', t_mat, v_beta)
    w = jnp.einsum('bhnij,bhnjd->bhnid', t_mat, k_beta * jnp.exp(g)[..., None])
    attn = jnp.einsum('bhnid,bhnjd->bhnij', q, k) * decay

    def step(state, xs):
        q_c, k_c, u_c, w_c, g_c, attn_c = xs
        v_new = u_c - jnp.einsum('bhck,bhkv->bhcv', w_c, state)
        o_c = (jnp.einsum('bhck,bhkv->bhcv', q_c * jnp.exp(g_c)[..., None], state)
               + jnp.einsum('bhij,bhjv->bhiv', attn_c, v_new))
        g_last = g_c[..., -1:]
        state = (state * jnp.exp(g_last)[..., None]
                 + jnp.einsum('bhck,bhcv->bhkv', k_c * jnp.exp(g_last - g_c)[..., None], v_new))
        return state, o_c

    xs = tuple(jnp.moveaxis(t, 2, 0) for t in (q, k, u, w, g, attn))
    state0 = jnp.zeros((B, H, dk, dv), q.dtype)
    _, o = lax.scan(step, state0, xs)
    return jnp.moveaxis(o, 0, 2).reshape(B, H, S, dv)


def gated_deltanet(qkv, z, b, a, conv_w, a_log, dt_bias, norm_g):
    B, S, _ = qkv.shape
    H, hd = DN_HEADS, HEAD_DIM
    f32 = jnp.float32
    qkv = jax.nn.silu(causal_dwconv(qkv, conv_w)).astype(f32)
    qkv = qkv.reshape(B, S, 3, H, hd).transpose(2, 0, 3, 1, 4)
    q = l2norm(qkv[0]) * hd ** -0.5
    k = l2norm(qkv[1])
    v = qkv[2]
    beta = jax.nn.sigmoid(b.astype(f32)).transpose(0, 2, 1)
    g = (-jnp.exp(a_log.astype(f32)) * jax.nn.softplus(a.astype(f32) + dt_bias.astype(f32))).transpose(0, 2, 1)
    o = chunk_gated_delta_rule(q, k, v, beta, g).transpose(0, 2, 1, 3)
    o = rmsnorm(o, norm_g) * jax.nn.silu(z.astype(f32).reshape(B, S, H, hd))
    return o.reshape(B, S, H * hd)


def dilated_attention(qkv, bias_tab):
    B, S, _ = qkv.shape
    H, hd = DIL_HEADS, HEAD_DIM
    f32 = jnp.float32
    qkv = qkv.astype(f32).reshape(B, S, 3, H, hd).transpose(2, 0, 3, 1, 4)
    q, k, v = qkv[0] * hd ** -0.5, qkv[1], qkv[2]
    tab = bias_tab.astype(f32)

    def block(bi):
        t = bi * Q_BLOCK + jnp.arange(Q_BLOCK)
        qb = lax.dynamic_slice_in_dim(q, bi * Q_BLOCK, Q_BLOCK, axis=2)
        outs, maxs, dens = [], [], []
        for window, dil in DIL_PATTERNS:
            j = jnp.arange(window // dil + 1)
            idx = t[:, None] - dil * j[None, :]
            valid = idx >= 0
            idx = jnp.maximum(idx, 0)
            kg = jnp.take(k, idx, axis=2)
            vg = jnp.take(v, idx, axis=2)
            logits = (jnp.einsum('bhqd,bhqjd->bhqj', qb, kg)
                      + tab[:, rel_bucket(dil * j)][None, :, None, :])
            logits = jnp.where(valid, logits, NEG)
            m = jnp.max(logits, axis=-1, keepdims=True)
            p = jnp.exp(logits - m) * valid
            den = jnp.sum(p, axis=-1, keepdims=True)
            outs.append(jnp.einsum('bhqj,bhqjd->bhqd', p, vg) / den)
            maxs.append(m)
            dens.append(den)
        m_all = maxs[0]
        for m_ in maxs[1:]:
            m_all = jnp.maximum(m_all, m_)
        wts = [d_ * jnp.exp(m_ - m_all) for d_, m_ in zip(dens, maxs)]
        num = wts[0] * outs[0]
        den_all = wts[0]
        for w_, o_ in zip(wts[1:], outs[1:]):
            num = num + w_ * o_
            den_all = den_all + w_
        return num / den_all

    o = lax.map(block, jnp.arange(S // Q_BLOCK))
    return o.transpose(1, 0, 3, 2, 4).reshape(B, S, H * hd)


def nsa_attention(q, kv, gate_logits, bias_tab, cmp_pos, cmp_w1, cmp_w2):
    B, S, _ = q.shape
    G, R, hd = NSA_KV_GROUPS, NSA_Q_PER_KV, HEAD_DIM
    f32 = jnp.float32
    qg = q.astype(f32).reshape(B, S, G, R, hd).transpose(0, 2, 3, 1, 4) * hd ** -0.5
    kv = kv.astype(f32).reshape(B, S, 3, 2, G, hd).transpose(2, 3, 0, 4, 1, 5)
    t = jnp.arange(S)
    tab = bias_tab.astype(f32)
    head_bias = tab.reshape(G, R, REL_BUCKETS)

    n_cmp = (S - CMP_LEN) // CMP_STRIDE + 1
    c_start = CMP_STRIDE * jnp.arange(n_cmp)
    blk_idx = c_start[:, None] + jnp.arange(CMP_LEN)[None, :]

    def compress(xs, i):
        blocks = jnp.take(xs, blk_idx, axis=2) + cmp_pos[i].astype(f32)
        hmid = jax.nn.gelu(jnp.einsum('bgcf,fh->bgch', blocks.reshape(B, G, n_cmp, CMP_LEN * hd),
                                      cmp_w1[i].astype(f32)))
        return jnp.einsum('bgch,hd->bgcd', hmid, cmp_w2[i].astype(f32))

    k_cmp = compress(kv[0, 0], 0)
    v_cmp = compress(kv[0, 1], 1)
    c_end = c_start + CMP_LEN - 1
    c_mask = c_end[None, :] <= t[:, None]
    c_bias = head_bias[:, :, rel_bucket(t[:, None] - c_end[None, :])]
    c_logits = jnp.einsum('bgrsd,bgcd->bgrsc', qg, k_cmp) + c_bias
    p_cmp = masked_softmax(c_logits, c_mask)
    o_cmp = jnp.einsum('bgrsc,bgcd->bgrsd', p_cmp, v_cmp)

    n_slc = S // SLC_BLOCK
    s_start = SLC_BLOCK * jnp.arange(n_slc)
    overlap = jnp.clip(jnp.minimum(c_start[:, None] + CMP_LEN, s_start[None, :] + SLC_BLOCK)
                       - jnp.maximum(c_start[:, None], s_start[None, :]), 0).astype(f32) / CMP_STRIDE
    imp = jnp.einsum('bgrsc,cj->bgsj', p_cmp, overlap)
    cur = t // SLC_BLOCK
    jb = jnp.arange(n_slc)
    forced = (jb[None, :] == 0) | ((jb[None, :] <= cur[:, None]) & (jb[None, :] > cur[:, None] - SLC_LOCAL))
    allowed = jb[None, :] <= cur[:, None]
    imp = jnp.where(forced, jnp.inf, jnp.where(allowed, imp, -jnp.inf))
    n_top = min(SLC_TOPK, n_slc)
    _, sel = lax.top_k(imp, n_top)
    sel_ok = sel <= cur[None, None, :, None]

    k_slc = kv[1, 0].reshape(B, G, n_slc, SLC_BLOCK, hd)
    v_slc = kv[1, 1].reshape(B, G, n_slc, SLC_BLOCK, hd)
    b_ix = jnp.arange(B)[:, None, None, None]
    g_ix = jnp.arange(G)[None, :, None, None]
    flat_bias = tab.reshape(-1)
    head_off = (jnp.arange(G)[:, None] * R + jnp.arange(R)[None, :])[None, :, :, None, None] * REL_BUCKETS
    in_blk = jnp.arange(SLC_BLOCK)

    def slc_step(bi):
        t0 = bi * SLC_Q_BLOCK
        tq = t0 + jnp.arange(SLC_Q_BLOCK)
        qb = lax.dynamic_slice_in_dim(qg, t0, SLC_Q_BLOCK, axis=3)
        sb = lax.dynamic_slice_in_dim(sel, t0, SLC_Q_BLOCK, axis=2)
        ok = lax.dynamic_slice_in_dim(sel_ok, t0, SLC_Q_BLOCK, axis=2)
        kg = k_slc[b_ix, g_ix, sb].reshape(B, G, SLC_Q_BLOCK, n_top * SLC_BLOCK, hd)
        vg = v_slc[b_ix, g_ix, sb].reshape(B, G, SLC_Q_BLOCK, n_top * SLC_BLOCK, hd)
        pos5 = sb[..., None] * SLC_BLOCK + in_blk
        mask5 = ok[..., None] & (pos5 <= tq[None, None, :, None, None])
        pos = pos5.reshape(B, G, SLC_Q_BLOCK, n_top * SLC_BLOCK)
        mask = mask5.reshape(B, G, SLC_Q_BLOCK, n_top * SLC_BLOCK)
        bias = flat_bias[head_off + rel_bucket(tq[None, None, :, None] - pos)[:, :, None]]
        logits = jnp.einsum('bgrqd,bgqkd->bgrqk', qb, kg) + bias
        p = masked_softmax(logits, mask[:, :, None])
        return jnp.einsum('bgrqk,bgqkd->bgrqd', p, vg)

    o_slc = lax.map(slc_step, jnp.arange(S // SLC_Q_BLOCK))
    o_slc = o_slc.transpose(1, 2, 3, 0, 4, 5).reshape(B, G, R, S, hd)

    k_win = jnp.pad(kv[2, 0], ((0, 0), (0, 0), (WIN, 0), (0, 0)))
    v_win = jnp.pad(kv[2, 1], ((0, 0), (0, 0), (WIN, 0), (0, 0)))
    span = Q_BLOCK + WIN
    rel = jnp.arange(Q_BLOCK)[:, None] + WIN - jnp.arange(span)[None, :]
    band = (rel >= 0) & (rel < WIN)
    w_bias = head_bias[:, :, rel_bucket(rel)]

    def win_step(bi):
        t0 = bi * Q_BLOCK
        qb = lax.dynamic_slice_in_dim(qg, t0, Q_BLOCK, axis=3)
        kb = lax.dynamic_slice_in_dim(k_win, t0, span, axis=2)
        vb = lax.dynamic_slice_in_dim(v_win, t0, span, axis=2)
        mask = band & ((t0 - WIN + jnp.arange(span)) >= 0)[None, :]
        logits = jnp.einsum('bgrqd,bgkd->bgrqk', qb, kb) + w_bias
        p = masked_softmax(logits, mask)
        return jnp.einsum('bgrqk,bgkd->bgrqd', p, vb)

    o_win = lax.map(win_step, jnp.arange(S // Q_BLOCK))
    o_win = o_win.transpose(1, 2, 3, 0, 4, 5).reshape(B, G, R, S, hd)

    gates = jax.nn.sigmoid(gate_logits.astype(f32)).reshape(B, S, G, R, 3).transpose(4, 0, 2, 3, 1)[..., None]
    o = gates[0] * o_cmp + gates[1] * o_slc + gates[2] * o_win
    return o.transpose(0, 3, 1, 2, 4).reshape(B, S, NSA_WIDTH)


def setup_inputs(seed: int = 0) -> dict:
    key = jax.random.key(seed)
    ks = jax.random.split(key, 17)
    f32 = jnp.float32
    L = DEPTH

    def nrm(k, shape, scale):
        return jax.random.normal(k, shape, f32) * scale

    dt = jnp.exp(jax.random.uniform(ks[5], (L, DN_HEADS), f32, minval=math.log(DT_MIN), maxval=math.log(DT_MAX)))
    return {
        'x': nrm(ks[0], (BATCH, SEQ, D_MODEL), 1.0),
        'norm1_g': 1.0 + nrm(ks[1], (L, D_MODEL), 0.02),
        'w_in': nrm(ks[2], (L, D_MODEL, IN_COLS), D_MODEL ** -0.5),
        'dn_conv': nrm(ks[3], (L, DN_CONV, 3 * DN_WIDTH), DN_CONV ** -0.5),
        'dn_a_log': jnp.log(jax.random.uniform(ks[4], (L, DN_HEADS), f32, minval=1.0, maxval=16.0)),
        'dn_dt_bias': dt + jnp.log(-jnp.expm1(-dt)),
        'dn_norm_g': 1.0 + nrm(ks[6], (L, HEAD_DIM), 0.02),
        'cmp_pos': nrm(ks[7], (L, 2, CMP_LEN, HEAD_DIM), 0.02),
        'cmp_w1': nrm(ks[8], (L, 2, CMP_LEN * HEAD_DIM, CMP_HIDDEN), (CMP_LEN * HEAD_DIM) ** -0.5),
        'cmp_w2': nrm(ks[9], (L, 2, CMP_HIDDEN, HEAD_DIM), CMP_HIDDEN ** -0.5),
        'w_out': nrm(ks[10], (L, MIX_WIDTH, D_MODEL), MIX_WIDTH ** -0.5),
        'norm2_g': 1.0 + nrm(ks[11], (L, D_MODEL), 0.02),
        'ffn_up': nrm(ks[12], (L, D_MODEL, 2 * D_FF), D_MODEL ** -0.5),
        'ffn_conv': nrm(ks[13], (L, FFN_CONV, 2 * D_FF), FFN_CONV ** -0.5),
        'ffn_down': nrm(ks[14], (L, D_FF, D_MODEL), D_FF ** -0.5),
        'rel_bias': nrm(ks[15], (N_BIAS_HEADS, REL_BUCKETS), 0.2),
        'final_g': 1.0 + nrm(ks[16], (D_MODEL,), 0.02),
    }


def reference(x, norm1_g, w_in, dn_conv, dn_a_log, dn_dt_bias, dn_norm_g, cmp_pos, cmp_w1, cmp_w2,
              w_out, norm2_g, ffn_up, ffn_conv, ffn_down, rel_bias, final_g):
    dil_bias = rel_bias[:DIL_HEADS]
    nsa_bias = rel_bias[DIL_HEADS:]
    for l in range(DEPTH):
        h = rmsnorm(x, norm1_g[l])
        proj = jnp.einsum('bsd,dp->bsp', h, w_in[l])
        dn_qkv, dn_z, dn_b, dn_a, dil_qkv, nsa_q, nsa_kv, nsa_g = split_cols(proj)
        o_dn = gated_deltanet(dn_qkv, dn_z, dn_b, dn_a, dn_conv[l], dn_a_log[l], dn_dt_bias[l], dn_norm_g[l])
        o_dil = dilated_attention(dil_qkv, dil_bias)
        o_nsa = nsa_attention(nsa_q, nsa_kv, nsa_g, nsa_bias, cmp_pos[l], cmp_w1[l], cmp_w2[l])
        mix = jnp.concatenate([o_dn, o_dil, o_nsa], axis=-1).astype(x.dtype)
        x = x + jnp.einsum('bsm,md->bsd', mix, w_out[l])
        h = rmsnorm(x, norm2_g[l])
        u = causal_dwconv(jnp.einsum('bsd,df->bsf', h, ffn_up[l]), ffn_conv[l])
        u_gate, u_val = jnp.split(u, 2, axis=-1)
        x = x + jnp.einsum('bsf,fd->bsd', jax.nn.silu(u_gate) * u_val, ffn_down[l])
    return rmsnorm(x, final_g)
```

```python
import functools
import math

import numpy as np
import jax
import jax.numpy as jnp
from jax import lax
from jax.experimental import pallas as pl
from jax.experimental.pallas import tpu as pltpu

F32 = jnp.float32
BF16 = jnp.bfloat16
HIGHEST = lax.Precision.HIGHEST

LANES = 128
HEAD_DIM = 128
DN_HEADS = 6
DIL_HEADS = 4
NSA_GROUPS = 2
NSA_REP = 3
DN_CHUNK = 64
DN_SUB = 16
CMP_LEN = 32
CMP_STRIDE = 16
CMP_HIDDEN = 256
SLC_BLOCK = 64
SLC_SHIFT = 6
SLC_TOPK = 16
WIN = 512
Q_BLOCK = 128
DIL_PATTERNS = ((128, 1), (512, 4), (2048, 16))
REL_BUCKETS = 32
REL_MAX_DIST = 2048
EPS = 1e-6
NEG = -1e30
VMEM_LIMIT = 56 * 1024 * 1024

CB_DN_Q, CB_DN_K, CB_DN_V, CB_DN_Z = 0, 6, 12, 18
CB_DIL_Q, CB_DIL_K, CB_DIL_V = 24, 28, 32
CB_NSA_Q = 36
CB_NSA_KV = 42
CB_SMALL = 54
PROJ_COLS = 7168
LANE_B, LANE_A, LANE_GATE = 0, 6, 12


def _bucket_thresholds():
    n = np.arange(0, REL_MAX_DIST + 1)
    max_exact = REL_BUCKETS // 2
    out = []
    for dt in (np.float32, np.float64):
        nf = np.maximum(n, 1).astype(dt)
        large = max_exact + (np.log(nf / dt(max_exact)) / dt(math.log(REL_MAX_DIST / max_exact))
                             * dt(REL_BUCKETS - max_exact)).astype(np.int32)
        large = np.minimum(large, REL_BUCKETS - 1)
        out.append(np.where(n < max_exact, n, large))
    assert np.array_equal(out[0], out[1])
    bucket = out[1]
    assert np.all(np.diff(bucket) >= 0)
    thr = [0]
    for b in range(1, REL_BUCKETS):
        idx = np.nonzero(bucket >= b)[0]
        thr.append(int(idx[0]) if idx.size else REL_MAX_DIST + 1)
    return tuple(thr)


_THR = _bucket_thresholds()


def _bias_of_dist(dist, tab_ref, head):
    val = jnp.full(dist.shape, tab_ref[head, 0], F32)
    for b in range(1, REL_BUCKETS):
        val = jnp.where(dist >= _THR[b], tab_ref[head, b], val)
    return val


def _sigmoid(x):
    return 1.0 / (1.0 + jnp.exp(-x))


def _dot(a, b, **kw):
    return jnp.dot(a, b, preferred_element_type=F32, **kw)


def _dot_nt(a, b, **kw):
    return lax.dot_general(a, b, (((1,), (1,)), ((), ())), preferred_element_type=F32, **kw)


def _bmm(a, b):
    return jnp.einsum('nij,njk->nik', a, b, preferred_element_type=F32, precision=HIGHEST)


def _params(*sem):
    return pltpu.CompilerParams(dimension_semantics=sem, vmem_limit_bytes=VMEM_LIMIT)


def _norm_matmul_kernel(x_ref, g_ref, w_ref, o_ref, h_ref):
    @pl.when(pl.program_id(1) == 0)
    def _():
        x = x_ref[...]
        ms = jnp.mean(x * x, axis=-1, keepdims=True)
        h_ref[...] = (x * lax.rsqrt(ms + EPS) * g_ref[...]).astype(BF16)

    o_ref[...] = _dot(h_ref[...], w_ref[...]).astype(o_ref.dtype)


def _norm_matmul(x, g, w, tm, tn, out_dtype):
    m, d = x.shape
    n = w.shape[1]
    return pl.pallas_call(
        _norm_matmul_kernel,
        grid=(m // tm, n // tn),
        in_specs=[pl.BlockSpec((tm, d), lambda i, j: (i, 0)),
                  pl.BlockSpec((1, d), lambda i, j: (0, 0)),
                  pl.BlockSpec((d, tn), lambda i, j: (0, j))],
        out_specs=pl.BlockSpec((tm, tn), lambda i, j: (i, j)),
        out_shape=jax.ShapeDtypeStruct((m, n), out_dtype),
        scratch_shapes=[pltpu.VMEM((tm, d), BF16)],
        compiler_params=_params("parallel", "arbitrary"),
        name="in_proj",
    )(x, g.reshape(1, d), w)


def _out_proj_kernel(a_ref, b_ref, c_ref, wa_ref, wb_ref, wc_ref, x_ref, o_ref):
    acc = _dot(a_ref[...], wa_ref[...])
    acc = acc + _dot(b_ref[...], wb_ref[...])
    acc = acc + _dot(c_ref[...], wc_ref[...])
    o_ref[...] = x_ref[...] + acc


def _out_proj(o_dn, o_dil, o_nsa, w_out, x, tm):
    m, d = x.shape
    ka, kb, kc = o_dn.shape[1], o_dil.shape[1], o_nsa.shape[1]
    wa, wb, wc = w_out[:ka], w_out[ka:ka + kb], w_out[ka + kb:]
    row = lambda i: (i, 0)
    full = lambda i: (0, 0)
    return pl.pallas_call(
        _out_proj_kernel,
        grid=(m // tm,),
        in_specs=[pl.BlockSpec((tm, ka), row), pl.BlockSpec((tm, kb), row), pl.BlockSpec((tm, kc), row),
                  pl.BlockSpec((ka, d), full), pl.BlockSpec((kb, d), full), pl.BlockSpec((kc, d), full),
                  pl.BlockSpec((tm, d), row)],
        out_specs=pl.BlockSpec((tm, d), row),
        out_shape=jax.ShapeDtypeStruct((m, d), F32),
        compiler_params=_params("parallel"),
        name="out_proj",
    )(o_dn, o_dil, o_nsa, wa, wb, wc, x)


def _ffn_up_kernel(x_ref, xp_ref, g_ref, wg_ref, wv_ref, cg_ref, cv_ref, o_ref, h_ref, *, tiles_per_seq):
    tm = x_ref.shape[0]
    halo = xp_ref.shape[0]

    @pl.when(pl.program_id(1) == 0)
    def _():
        def norm(x):
            ms = jnp.mean(x * x, axis=-1, keepdims=True)
            return (x * lax.rsqrt(ms + EPS) * g_ref[...]).astype(BF16)
        first = (pl.program_id(0) % tiles_per_seq) == 0
        hp = norm(xp_ref[...])
        h_ref[0:halo, :] = jnp.where(first, jnp.zeros_like(hp), hp)
        h_ref[halo:halo + tm, :] = norm(x_ref[...])

    h = h_ref[...]

    def conv(w_ref, c_ref):
        u = _dot(h, w_ref[...])
        y = u * c_ref[2:3, :]
        y = y + pltpu.roll(u, 1, axis=0) * c_ref[1:2, :]
        y = y + pltpu.roll(u, 2, axis=0) * c_ref[0:1, :]
        return y[halo:, :]

    gate = conv(wg_ref, cg_ref)
    val = conv(wv_ref, cv_ref)
    o_ref[...] = (gate * _sigmoid(gate) * val).astype(o_ref.dtype)


def _ffn_up(x, g, w_up, conv_w, seq, tm, tn):
    m, d = x.shape
    f = w_up.shape[1] // 2
    halo = 8
    nf = f // tn
    kern = functools.partial(_ffn_up_kernel, tiles_per_seq=seq // tm)
    return pl.pallas_call(
        kern,
        grid=(m // tm, nf),
        in_specs=[pl.BlockSpec((tm, d), lambda i, j: (i, 0)),
                  pl.BlockSpec((halo, d), lambda i, j: (jnp.maximum(i * (tm // halo) - 1, 0), 0)),
                  pl.BlockSpec((1, d), lambda i, j: (0, 0)),
                  pl.BlockSpec((d, tn), lambda i, j: (0, j)),
                  pl.BlockSpec((d, tn), lambda i, j: (0, j + nf)),
                  pl.BlockSpec((3, tn), lambda i, j: (0, j)),
                  pl.BlockSpec((3, tn), lambda i, j: (0, j + nf))],
        out_specs=pl.BlockSpec((tm, tn), lambda i, j: (i, j)),
        out_shape=jax.ShapeDtypeStruct((m, f), BF16),
        scratch_shapes=[pltpu.VMEM((tm + halo, d), BF16)],
        compiler_params=_params("parallel", "arbitrary"),
        name="ffn_up",
    )(x, x, g.reshape(1, d), w_up, w_up, conv_w, conv_w)


def _ffn_down_kernel(a_ref, w_ref, x_ref, o_ref):
    o_ref[...] = x_ref[...] + _dot(a_ref[...], w_ref[...])


def _ffn_down(act, w_down, x, tm, tn):
    m, f = act.shape
    d = w_down.shape[1]
    return pl.pallas_call(
        _ffn_down_kernel,
        grid=(d // tn, m // tm),
        in_specs=[pl.BlockSpec((tm, f), lambda j, i: (i, 0)),
                  pl.BlockSpec((f, tn), lambda j, i: (0, j)),
                  pl.BlockSpec((tm, tn), lambda j, i: (i, j))],
        out_specs=pl.BlockSpec((tm, tn), lambda j, i: (i, j)),
        out_shape=jax.ShapeDtypeStruct((m, d), F32),
        compiler_params=_params("parallel", "parallel"),
        name="ffn_down",
    )(act, w_down, x)


def _rmsnorm_kernel(x_ref, g_ref, o_ref):
    x = x_ref[...]
    ms = jnp.mean(x * x, axis=-1, keepdims=True)
    o_ref[...] = x * lax.rsqrt(ms + EPS) * g_ref[...]


def _rmsnorm(x, g, tm):
    m, d = x.shape
    return pl.pallas_call(
        _rmsnorm_kernel,
        grid=(m // tm,),
        in_specs=[pl.BlockSpec((tm, d), lambda i: (i, 0)), pl.BlockSpec((1, d), lambda i: (0, 0))],
        out_specs=pl.BlockSpec((tm, d), lambda i: (i, 0)),
        out_shape=jax.ShapeDtypeStruct((m, d), F32),
        compiler_params=_params("parallel"),
        name="final_norm",
    )(x, g.reshape(1, d))


def _dn_kernel(alog_ref, dtb_ref, q_ref, k_ref, v_ref, z_ref, s_ref, cq_ref, ck_ref, cv_ref, ng_ref,
               o_ref, qg_s, w_s, u_s, kdt_s, attn_s, egl_s, o_s):
    h = pl.program_id(1)
    seq = q_ref.shape[0]
    c = DN_CHUNK
    nc = seq // c
    row = lax.broadcasted_iota(jnp.int32, (seq, 1), 0)

    def conv_silu(x_ref, c_ref):
        x = x_ref[...]
        taps = c_ref.shape[0]
        acc = x * c_ref[taps - 1:taps, :]
        for s in range(1, taps):
            xs = jnp.where(row >= s, pltpu.roll(x, s, axis=0), 0.0)
            acc = acc + xs * c_ref[taps - 1 - s:taps - s, :]
        return acc * _sigmoid(acc)

    def l2norm(x):
        return x * lax.rsqrt(jnp.sum(x * x, axis=-1, keepdims=True) + EPS)

    q = l2norm(conv_silu(q_ref, cq_ref)) * (HEAD_DIM ** -0.5)
    k = l2norm(conv_silu(k_ref, ck_ref))
    v = conv_silu(v_ref, cv_ref)

    lane = lax.broadcasted_iota(jnp.int32, (1, LANES), 1)
    sm = s_ref[...]
    b_col = jnp.sum(jnp.where(lane == LANE_B + h, sm, 0.0), axis=-1, keepdims=True)
    a_col = jnp.sum(jnp.where(lane == LANE_A + h, sm, 0.0), axis=-1, keepdims=True)
    beta = _sigmoid(b_col)
    ap = a_col + dtb_ref[h]
    softplus = jnp.maximum(ap, 0.0) + jnp.log1p(jnp.exp(-jnp.abs(ap)))
    neg_rate = -jnp.exp(jnp.full((1, 1), alog_ref[h], F32))
    g = jnp.broadcast_to(neg_rate * softplus, (seq, LANES))

    pos = row & (c - 1)
    gc = g
    s = 1
    while s < c:
        gc = gc + jnp.where(pos >= s, pltpu.roll(gc, s, axis=0), 0.0)
        s *= 2

    egc = jnp.exp(gc)
    gc3 = gc.reshape(nc, c, LANES)
    glast3 = jnp.broadcast_to(gc3[:, c - 1:c, :], (nc, c, LANES))
    kb = k * beta
    vb = v * beta
    k3 = k.reshape(nc, c, HEAD_DIM)

    qg_s[...] = q * egc
    kd3 = k3 * jnp.exp(glast3 - gc3)
    kdt_s[...] = jnp.swapaxes(kd3, 1, 2)
    egl_s[...] = jnp.exp(glast3[:, 0:8, :])

    ii = lax.broadcasted_iota(jnp.int32, (c, c), 0)
    jj = lax.broadcasted_iota(jnp.int32, (c, c), 1)
    gcol = gc3[:, :, 0:c]
    grow = jnp.swapaxes(gc3, 1, 2)[:, 0:c, :]
    decay = jnp.where(ii >= jj, jnp.exp(jnp.minimum(gcol - grow, 0.0)), 0.0)

    kk = jnp.einsum('nid,njd->nij', kb.reshape(nc, c, HEAD_DIM), k3,
                    preferred_element_type=F32, precision=HIGHEST)
    qk = jnp.einsum('nid,njd->nij', q.reshape(nc, c, HEAD_DIM), k3,
                    preferred_element_type=F32, precision=HIGHEST)
    attn_s[...] = qk * decay
    a_mat = jnp.where(ii > jj, kk * decay, 0.0)

    same = (ii & -DN_SUB) == (jj & -DN_SUB)
    eye = jnp.where(ii == jj, 1.0, 0.0).astype(F32)
    x1 = jnp.where(same, -a_mat, 0.0)
    a_off = jnp.where(same, 0.0, a_mat)
    t_d = eye + x1
    xp = x1
    p = 2
    while p < DN_SUB:
        xp = _bmm(xp, xp)
        t_d = t_d + _bmm(t_d, xp)
        p *= 2
    y1 = -_bmm(t_d, a_off)
    t_o = eye + y1
    yp = y1
    p = 2
    while p < c // DN_SUB:
        yp = _bmm(yp, yp)
        t_o = t_o + _bmm(t_o, yp)
        p *= 2
    t_mat = _bmm(t_o, t_d)

    u_s[...] = _bmm(t_mat, vb.reshape(nc, c, HEAD_DIM)).reshape(seq, HEAD_DIM)
    w_s[...] = _bmm(t_mat, (kb * egc).reshape(nc, c, HEAD_DIM)).reshape(seq, HEAD_DIM)

    def chunk_step(n, state):
        r0 = pl.multiple_of(n * c, c)
        rows = pl.ds(r0, c)
        v_new = u_s[rows, :] - _dot(w_s[rows, :], state, precision=HIGHEST)
        o_s[rows, :] = (_dot(qg_s[rows, :], state, precision=HIGHEST)
                        + _dot(attn_s[n], v_new, precision=HIGHEST))
        return state * egl_s[n][0:1, :] + _dot(kdt_s[n], v_new, precision=HIGHEST)

    lax.fori_loop(0, nc, chunk_step, jnp.zeros((HEAD_DIM, HEAD_DIM), F32))

    o = o_s[...]
    y = o * lax.rsqrt(jnp.mean(o * o, axis=-1, keepdims=True) + EPS) * ng_ref[...]
    z = z_ref[...]
    o_ref[...] = (y * (z * _sigmoid(z))).astype(o_ref.dtype)


def _deltanet(proj, dn_conv, a_log, dt_bias, norm_g):
    bsz, seq, _ = proj.shape
    nc = seq // DN_CHUNK
    col = lambda off: pl.BlockSpec((None, seq, LANES), lambda b, h: (b, 0, off + h))
    cw = lambda off: pl.BlockSpec((dn_conv.shape[0], LANES), lambda b, h: (0, off + h))
    smem = pl.BlockSpec(memory_space=pltpu.SMEM)
    return pl.pallas_call(
        _dn_kernel,
        grid=(bsz, DN_HEADS),
        in_specs=[smem, smem, col(CB_DN_Q), col(CB_DN_K), col(CB_DN_V), col(CB_DN_Z),
                  pl.BlockSpec((None, seq, LANES), lambda b, h: (b, 0, CB_SMALL)),
                  cw(CB_DN_Q), cw(CB_DN_K), cw(CB_DN_V),
                  pl.BlockSpec((1, HEAD_DIM), lambda b, h: (0, 0))],
        out_specs=pl.BlockSpec((None, seq, HEAD_DIM), lambda b, h: (b, 0, h)),
        out_shape=jax.ShapeDtypeStruct((bsz, seq, DN_HEADS * HEAD_DIM), BF16),
        scratch_shapes=[pltpu.VMEM((seq, HEAD_DIM), F32),
                        pltpu.VMEM((seq, HEAD_DIM), F32),
                        pltpu.VMEM((seq, HEAD_DIM), F32),
                        pltpu.VMEM((nc, HEAD_DIM, DN_CHUNK), F32),
                        pltpu.VMEM((nc, DN_CHUNK, DN_CHUNK), F32),
                        pltpu.VMEM((nc, 8, LANES), F32),
                        pltpu.VMEM((seq, HEAD_DIM), F32)],
        compiler_params=_params("parallel", "parallel"),
        name="deltanet",
    )(a_log, dt_bias, proj, proj, proj, proj, proj, dn_conv, dn_conv, dn_conv, norm_g.reshape(1, HEAD_DIM))


def _dil_kernel(tab_ref, q_ref, k_ref, v_ref, o_ref, num_s, m_s, l_s):
    head = pl.program_id(1)
    seq = q_ref.shape[0]
    qb = Q_BLOCK
    scale = HEAD_DIM ** -0.5

    r = lax.broadcasted_iota(jnp.int32, (qb, 2 * qb), 0)
    cidx = lax.broadcasted_iota(jnp.int32, (qb, 2 * qb), 1)
    sub = qb + r - cidx

    def band_tile(window, dil):
        valid = (sub >= 0) & (sub <= window // dil)
        return jnp.where(valid, _bias_of_dist(jnp.maximum(sub, 0) * dil, tab_ref, head), NEG)

    def block(qv, kv, vv, tile):
        s = _dot_nt((qv * scale).astype(BF16), kv.astype(BF16)) + tile
        m = jnp.max(s, axis=-1, keepdims=True)
        p = jnp.exp(s - m)
        l = jnp.sum(p, axis=-1, keepdims=True)
        return _dot(p.astype(BF16), vv.astype(BF16)), m, l

    for pat, (window, dil) in enumerate(DIL_PATTERNS):
        assert window // dil == qb
        tile = band_tile(window, dil)
        sub_len = seq // dil
        for res in range(dil):
            for bi in range(sub_len // qb):
                def rows(first_blk, nblk):
                    start = res + dil * qb * first_blk
                    return pl.ds(start, nblk * qb, stride=dil) if dil > 1 else pl.ds(start, nblk * qb)
                qr = rows(bi, 1)
                if bi == 0:
                    kr, t = rows(0, 1), tile[:, qb:]
                else:
                    kr, t = rows(bi - 1, 2), tile
                num, m, l = block(q_ref[qr, :], k_ref[kr, :], v_ref[kr, :], t)
                if pat == 0:
                    num_s[qr, :] = num
                    m_s[qr, :] = jnp.broadcast_to(m, (qb, LANES))
                    l_s[qr, :] = jnp.broadcast_to(l, (qb, LANES))
                else:
                    m_old = m_s[qr, :]
                    m_new = jnp.maximum(m_old, m)
                    a_old = jnp.exp(m_old - m_new)
                    a_new = jnp.exp(m - m_new)
                    num_s[qr, :] = num_s[qr, :] * a_old + num * a_new
                    l_s[qr, :] = l_s[qr, :] * a_old + l * a_new
                    m_s[qr, :] = m_new

    o_ref[...] = (num_s[...] / l_s[...]).astype(o_ref.dtype)


def _dilated(proj, dil_tab):
    bsz, seq, _ = proj.shape
    col = lambda off: pl.BlockSpec((None, seq, LANES), lambda b, h: (b, 0, off + h))
    return pl.pallas_call(
        _dil_kernel,
        grid=(bsz, DIL_HEADS),
        in_specs=[pl.BlockSpec(memory_space=pltpu.SMEM), col(CB_DIL_Q), col(CB_DIL_K), col(CB_DIL_V)],
        out_specs=pl.BlockSpec((None, seq, HEAD_DIM), lambda b, h: (b, 0, h)),
        out_shape=jax.ShapeDtypeStruct((bsz, seq, DIL_HEADS * HEAD_DIM), BF16),
        scratch_shapes=[pltpu.VMEM((seq, HEAD_DIM), F32), pltpu.VMEM((seq, LANES), F32),
                        pltpu.VMEM((seq, LANES), F32)],
        compiler_params=_params("parallel", "parallel"),
        name="dilated",
    )(dil_tab, proj, proj, proj)


def _cmp_kernel(x_ref, pos_ref, w1_ref, w2_ref, o_ref):
    nblk = o_ref.shape[0]
    half = CMP_LEN // 2
    ha = jnp.zeros((nblk, CMP_HIDDEN), F32)
    hb = jnp.zeros((nblk, CMP_HIDDEN), F32)
    for l in range(half):
        xl = x_ref[pl.ds(l, nblk, stride=CMP_STRIDE), :]
        wa = w1_ref[l * HEAD_DIM:(l + 1) * HEAD_DIM, :]
        wb = w1_ref[(half + l) * HEAD_DIM:(half + l + 1) * HEAD_DIM, :]
        ha = ha + _dot((xl + pos_ref[l:l + 1, :]).astype(BF16), wa)
        hb = hb + _dot((xl + pos_ref[half + l:half + l + 1, :]).astype(BF16), wb)
    hmid = ha + pltpu.roll(hb, nblk - 1, axis=0)
    hmid = 0.5 * hmid * (1.0 + jnp.tanh(math.sqrt(2.0 / math.pi) * (hmid + 0.044715 * hmid * hmid * hmid)))
    out = _dot(hmid.astype(BF16), w2_ref[...])
    rowi = lax.broadcasted_iota(jnp.int32, out.shape, 0)
    o_ref[...] = jnp.where(rowi < nblk - 1, out, 0.0)


def _compress(proj, cmp_pos, w1, w2):
    bsz, seq, _ = proj.shape
    nblk = seq // CMP_STRIDE
    return pl.pallas_call(
        _cmp_kernel,
        grid=(2, bsz, NSA_GROUPS),
        in_specs=[pl.BlockSpec((None, seq, LANES), lambda i, b, g: (b, 0, CB_NSA_KV + i * NSA_GROUPS + g)),
                  pl.BlockSpec((None, CMP_LEN, HEAD_DIM), lambda i, b, g: (i, 0, 0)),
                  pl.BlockSpec((None, CMP_LEN * HEAD_DIM, CMP_HIDDEN), lambda i, b, g: (i, 0, 0)),
                  pl.BlockSpec((None, CMP_HIDDEN, HEAD_DIM), lambda i, b, g: (i, 0, 0))],
        out_specs=pl.BlockSpec((None, None, None, nblk, HEAD_DIM), lambda i, b, g: (b, g, i, 0, 0)),
        out_shape=jax.ShapeDtypeStruct((bsz, NSA_GROUPS, 2, nblk, HEAD_DIM), F32),
        compiler_params=_params("parallel", "parallel", "parallel"),
        name="nsa_compress",
    )(proj, cmp_pos, w1, w2)


def _nsa_kernel(tab_ref, q0_ref, q1_ref, q2_ref, ks_ref, vs_ref, kw_ref, vw_ref, cmp_ref, s_ref,
                o_ref, strip_s, cbias_s, wtile_s, q_s, kv_s):
    grp = pl.program_id(0)
    i = pl.program_id(2)
    seq = q0_ref.shape[0]
    qb = Q_BLOCK
    nqb = seq // qb
    n_slc = seq // SLC_BLOCK
    ncmp = cmp_ref.shape[1]
    nwin = WIN // qb
    scale = HEAD_DIM ** -0.5
    q_refs = (q0_ref, q1_ref, q2_ref)

    @pl.when(i == 0)
    def _():
        for rr in range(NSA_REP):
            q_s[rr] = (q_refs[rr][...] * scale).astype(BF16)
        for n, ref in enumerate((ks_ref, vs_ref, kw_ref, vw_ref)):
            kv_s[n] = ref[...].astype(BF16)

    @pl.when(jnp.logical_and(pl.program_id(1) == 0, i == 0))
    def _():
        for rr in range(NSA_REP):
            head = DIL_HEADS + grp * NSA_REP + rr
            rs = lax.broadcasted_iota(jnp.int32, (qb, seq), 0)
            cs = lax.broadcasted_iota(jnp.int32, (qb, seq), 1)
            dist = rs - cs + (seq - qb)
            strip_s[rr] = jnp.where(dist >= 0, _bias_of_dist(jnp.maximum(dist, 0), tab_ref, head), NEG)
            tq = lax.broadcasted_iota(jnp.int32, (seq, ncmp), 0)
            cc = lax.broadcasted_iota(jnp.int32, (seq, ncmp), 1)
            dist = tq - (cc * CMP_STRIDE + CMP_LEN - 1)
            cbias_s[rr] = jnp.where(dist >= 0, _bias_of_dist(jnp.maximum(dist, 0), tab_ref, head), NEG)
            rw = lax.broadcasted_iota(jnp.int32, (qb, WIN + qb), 0)
            cw = lax.broadcasted_iota(jnp.int32, (qb, WIN + qb), 1)
            dist = rw + WIN - cw
            wtile_s[rr] = jnp.where((dist >= 0) & (dist < WIN),
                                    _bias_of_dist(jnp.clip(dist, 0, WIN), tab_ref, head), NEG)

    k_cmp = cmp_ref[0].astype(BF16)
    v_cmp = cmp_ref[1].astype(BF16)

    oj = lax.broadcasted_iota(jnp.int32, (n_slc, ncmp), 0) * SLC_BLOCK
    oc = lax.broadcasted_iota(jnp.int32, (n_slc, ncmp), 1) * CMP_STRIDE
    overlap_t = (jnp.maximum(jnp.minimum(oc + CMP_LEN, oj + SLC_BLOCK) - jnp.maximum(oc, oj), 0)
                 .astype(F32) / CMP_STRIDE)
    ej = lax.broadcasted_iota(jnp.int32, (LANES, qb), 0)
    ep = lax.broadcasted_iota(jnp.int32, (LANES, qb), 1)

    lane = lax.broadcasted_iota(jnp.int32, (1, LANES), 1)
    jrow = lax.broadcasted_iota(jnp.int32, (n_slc, qb), 0)
    tcol = lax.broadcasted_iota(jnp.int32, (n_slc, qb), 1)

    def flash(q, k_idx, v_idx, j_lo, bias_fn, keep_fn):
        def body(j, carry):
            m, l, acc = carry
            kr = pl.ds(pl.multiple_of(j * qb, qb), qb)
            s = _dot_nt(q, kv_s[k_idx, kr, :]) + bias_fn(j)
            if keep_fn is not None:
                s = jnp.where(keep_fn(j), s, NEG)
            m_new = jnp.maximum(m, jnp.max(s, axis=-1, keepdims=True))
            alpha = jnp.exp(m - m_new)
            p = jnp.exp(s - m_new)
            l = l * alpha + jnp.sum(p, axis=-1, keepdims=True)
            acc = acc * alpha + _dot(p.astype(BF16), kv_s[v_idx, kr, :])
            return m_new, l, acc
        init = (jnp.full((qb, 1), 0.5 * NEG, F32), jnp.zeros((qb, 1), F32), jnp.zeros((qb, HEAD_DIM), F32))
        _, l, acc = lax.fori_loop(j_lo, i + 1, body, init)
        return acc / l

    rows = pl.ds(pl.multiple_of(i * qb, qb), qb)
    qs = [q_s[rr, rows, :] for rr in range(NSA_REP)]
    sm = s_ref[rows, :]

    def gate(rr, branch):
        ln = LANE_GATE + (grp * NSA_REP + rr) * 3 + branch
        return _sigmoid(jnp.sum(jnp.where(lane == ln, sm, 0.0), axis=-1, keepdims=True))

    acc = []
    p_sum = jnp.zeros((qb, ncmp), F32)
    for rr in range(NSA_REP):
        cb = cbias_s[rr, rows, :]
        ok = cb > 0.5 * NEG
        s = _dot_nt(qs[rr], k_cmp) + cb
        m = jnp.max(s, axis=-1, keepdims=True)
        p = jnp.where(ok, jnp.exp(s - m), 0.0)
        p = p / jnp.maximum(jnp.sum(p, axis=-1, keepdims=True), 1e-30)
        p_sum = p_sum + p
        acc.append(gate(rr, 0) * _dot(p.astype(BF16), v_cmp))

    imp = _dot_nt(overlap_t, p_sum, precision=HIGHEST)
    cur = jnp.right_shift(i * qb + tcol, SLC_SHIFT)
    forced = (jrow == 0) | ((jrow <= cur) & (jrow > cur - 2))
    imp = jnp.where(forced, jnp.inf, jnp.where(jrow <= cur, imp, -jnp.inf))
    cnt = jnp.zeros((n_slc, qb), F32)
    for j2 in range(n_slc):
        other = imp[j2:j2 + 1, :]
        ge = jnp.where(other >= imp, 1.0, 0.0)
        gt = jnp.where(other > imp, 1.0, 0.0)
        cnt = cnt + jnp.where(jrow > j2, ge, gt)
    sel_t = jnp.where((cnt < SLC_TOPK) & (jrow <= cur), 1.0, 0.0)
    sel_t = jnp.concatenate([sel_t, jnp.zeros((LANES - n_slc, qb), F32)], axis=0)
    sel = sel_t.T.astype(BF16)

    def keep_fn(j):
        expand = jnp.where(ej == 2 * j + jnp.right_shift(ep, SLC_SHIFT), 1.0, 0.0).astype(BF16)
        return _dot(sel, expand) > 0.5

    for rr in range(NSA_REP):
        slc_bias = lambda j, rr=rr: strip_s[rr, :, pl.ds(pl.multiple_of((nqb - 1 - i + j) * qb, qb), qb)]
        acc[rr] = acc[rr] + gate(rr, 1) * flash(qs[rr], 0, 1, 0, slc_bias, keep_fn)
        win_bias = lambda j, rr=rr: wtile_s[rr, :, pl.ds(pl.multiple_of((j - i + nwin) * qb, qb), qb)]
        acc[rr] = acc[rr] + gate(rr, 2) * flash(qs[rr], 2, 3, jnp.maximum(i - nwin, 0), win_bias, None)
        o_ref[:, rr * HEAD_DIM:(rr + 1) * HEAD_DIM] = acc[rr].astype(o_ref.dtype)


def _nsa(proj, cmp_kv, rel_bias):
    bsz, seq, _ = proj.shape
    ncmp = cmp_kv.shape[3]
    col = lambda fn: pl.BlockSpec((None, seq, LANES), lambda g, b, i: (b, 0, fn(g)))
    kv = lambda branch, which: col(lambda g: CB_NSA_KV + (branch * 2 + which) * NSA_GROUPS + g)
    return pl.pallas_call(
        _nsa_kernel,
        grid=(NSA_GROUPS, bsz, seq // Q_BLOCK),
        in_specs=[pl.BlockSpec(memory_space=pltpu.SMEM),
                  col(lambda g: CB_NSA_Q + g * NSA_REP), col(lambda g: CB_NSA_Q + g * NSA_REP + 1),
                  col(lambda g: CB_NSA_Q + g * NSA_REP + 2),
                  kv(1, 0), kv(1, 1), kv(2, 0), kv(2, 1),
                  pl.BlockSpec((None, None, 2, ncmp, HEAD_DIM), lambda g, b, i: (b, g, 0, 0, 0)),
                  pl.BlockSpec((None, seq, LANES), lambda g, b, i: (b, 0, CB_SMALL))],
        out_specs=pl.BlockSpec((None, Q_BLOCK, NSA_REP * HEAD_DIM), lambda g, b, i: (b, i, g)),
        out_shape=jax.ShapeDtypeStruct((bsz, seq, NSA_GROUPS * NSA_REP * HEAD_DIM), BF16),
        scratch_shapes=[pltpu.VMEM((NSA_REP, Q_BLOCK, seq), F32),
                        pltpu.VMEM((NSA_REP, seq, ncmp), F32),
                        pltpu.VMEM((NSA_REP, Q_BLOCK, WIN + Q_BLOCK), F32),
                        pltpu.VMEM((NSA_REP, seq, HEAD_DIM), BF16),
                        pltpu.VMEM((4, seq, HEAD_DIM), BF16)],
        compiler_params=_params("arbitrary", "arbitrary", "arbitrary"),
        name="nsa_attention",
    )(rel_bias, proj, proj, proj, proj, proj, proj, proj, cmp_kv, proj)


def _reorder_w_in(w):
    wide_a, small_a, wide_b, small_b = 3072, 3084, 6924, 6942
    pad = PROJ_COLS - small_b
    return jnp.concatenate([w[:, :wide_a], w[:, small_a:wide_b], w[:, wide_a:small_a], w[:, wide_b:small_b],
                            jnp.zeros((w.shape[0], pad), w.dtype)], axis=1)


def kernel(x, norm1_g, w_in, dn_conv, dn_a_log, dn_dt_bias, dn_norm_g, cmp_pos, cmp_w1, cmp_w2, w_out,
           norm2_g, ffn_up, ffn_conv, ffn_down, rel_bias, final_g):
    bsz, seq, d = x.shape
    depth = w_in.shape[0]
    xf = x.reshape(bsz * seq, d)
    for l in range(depth):
        proj = _norm_matmul(xf, norm1_g[l], _reorder_w_in(w_in[l]).astype(BF16), 512, 1024, F32)
        proj = proj.reshape(bsz, seq, PROJ_COLS)
        o_dn = _deltanet(proj, dn_conv[l], dn_a_log[l], dn_dt_bias[l], dn_norm_g[l])
        o_dil = _dilated(proj, rel_bias)
        cmp_kv = _compress(proj, cmp_pos[l], cmp_w1[l].astype(BF16), cmp_w2[l].astype(BF16))
        o_nsa = _nsa(proj, cmp_kv, rel_bias)
        xf = _out_proj(o_dn.reshape(bsz * seq, -1), o_dil.reshape(bsz * seq, -1), o_nsa.reshape(bsz * seq, -1),
                       w_out[l].astype(BF16), xf, 512)
        act = _ffn_up(xf, norm2_g[l], ffn_up[l].astype(BF16), ffn_conv[l], seq, 512, 512)
        xf = _ffn_down(act, ffn_down[l].astype(BF16), xf, 512, 512)
    return _rmsnorm(xf, final_g, 512).reshape(bsz, seq, d)
```

```python
import functools
import math

import numpy as np
import jax
import jax.numpy as jnp
from jax import lax
from jax.experimental import pallas as pl
from jax.experimental.pallas import tpu as pltpu

F32 = jnp.float32
BF16 = jnp.bfloat16
HIGHEST = lax.Precision.HIGHEST

LANES = 128
HEAD_DIM = 128
DN_HEADS = 6
DIL_HEADS = 4
NSA_GROUPS = 2
NSA_REP = 3
DN_CHUNK = 64
DN_SUB = 16
CMP_LEN = 32
CMP_STRIDE = 16
CMP_HIDDEN = 256
SLC_BLOCK = 64
SLC_SHIFT = 6
SLC_TOPK = 16
SLC_KEY_TILE = 512
WIN = 512
Q_BLOCK = 128
DIL_PATTERNS = ((128, 1), (512, 4), (2048, 16))
REL_BUCKETS = 32
REL_MAX_DIST = 2048
EPS = 1e-6
NEG = -1e30
VMEM_LIMIT = 56 * 1024 * 1024

CB_DN_Q, CB_DN_K, CB_DN_V, CB_DN_Z = 0, 6, 12, 18
CB_DIL_Q, CB_DIL_K, CB_DIL_V = 24, 28, 32
CB_NSA_Q = 36
CB_NSA_KV = 42
CB_SMALL = 54
PROJ_COLS = 7168
LANE_B, LANE_A, LANE_GATE = 0, 6, 12


def _bucket_thresholds():
    n = np.arange(0, REL_MAX_DIST + 1)
    max_exact = REL_BUCKETS // 2
    out = []
    for dt in (np.float32, np.float64):
        nf = np.maximum(n, 1).astype(dt)
        large = max_exact + (np.log(nf / dt(max_exact)) / dt(math.log(REL_MAX_DIST / max_exact))
                             * dt(REL_BUCKETS - max_exact)).astype(np.int32)
        large = np.minimum(large, REL_BUCKETS - 1)
        out.append(np.where(n < max_exact, n, large))
    assert np.array_equal(out[0], out[1])
    bucket = out[1]
    assert np.all(np.diff(bucket) >= 0)
    thr = [0]
    for b in range(1, REL_BUCKETS):
        idx = np.nonzero(bucket >= b)[0]
        thr.append(int(idx[0]) if idx.size else REL_MAX_DIST + 1)
    return tuple(thr)


_THR = _bucket_thresholds()


def _bias_of_dist(dist, tab_ref, head):
    val = jnp.full(dist.shape, tab_ref[head, 0], F32)
    for b in range(1, REL_BUCKETS):
        val = jnp.where(dist >= _THR[b], tab_ref[head, b], val)
    return val


def _sigmoid(x):
    return 1.0 / (1.0 + jnp.exp(-x))


def _bf(x):
    return x.astype(BF16)


def _dot(a, b, **kw):
    return jnp.dot(a, b, preferred_element_type=F32, **kw)


def _dot_nt(a, b, **kw):
    return lax.dot_general(a, b, (((1,), (1,)), ((), ())), preferred_element_type=F32, **kw)


def _bmm(a, b):
    return jnp.einsum('nij,njk->nik', _bf(a), _bf(b), preferred_element_type=F32)


def _bmm_nt(a, b):
    return jnp.einsum('nid,njd->nij', _bf(a), _bf(b), preferred_element_type=F32)


def _params(*sem):
    return pltpu.CompilerParams(dimension_semantics=sem, vmem_limit_bytes=VMEM_LIMIT)


def _norm_matmul_kernel(x_ref, g_ref, w_ref, o_ref, h_ref):
    @pl.when(pl.program_id(1) == 0)
    def _():
        x = x_ref[...]
        ms = jnp.mean(x * x, axis=-1, keepdims=True)
        h_ref[...] = (x * lax.rsqrt(ms + EPS) * g_ref[...]).astype(BF16)

    o_ref[...] = _dot_nt(h_ref[...], w_ref[...]).astype(o_ref.dtype)


def _norm_matmul(x, g, w_t, tm, tn, out_dtype):
    m, d = x.shape
    n = w_t.shape[0]
    return pl.pallas_call(
        _norm_matmul_kernel,
        grid=(m // tm, n // tn),
        in_specs=[pl.BlockSpec((tm, d), lambda i, j: (i, 0)),
                  pl.BlockSpec((1, d), lambda i, j: (0, 0)),
                  pl.BlockSpec((tn, d), lambda i, j: (j, 0))],
        out_specs=pl.BlockSpec((tm, tn), lambda i, j: (i, j)),
        out_shape=jax.ShapeDtypeStruct((m, n), out_dtype),
        scratch_shapes=[pltpu.VMEM((tm, d), BF16)],
        compiler_params=_params("parallel", "arbitrary"),
        name="in_proj",
    )(x, g.reshape(1, d), w_t)


def _out_proj_kernel(a_ref, b_ref, c_ref, wa_ref, wb_ref, wc_ref, x_ref, o_ref):
    acc = _dot(a_ref[...], wa_ref[...])
    acc = acc + _dot(b_ref[...], wb_ref[...])
    acc = acc + _dot(c_ref[...], wc_ref[...])
    o_ref[...] = x_ref[...] + acc


def _out_proj(o_dn, o_dil, o_nsa, w_out, x, tm):
    m, d = x.shape
    ka, kb, kc = o_dn.shape[1], o_dil.shape[1], o_nsa.shape[1]
    wa, wb, wc = w_out[:ka], w_out[ka:ka + kb], w_out[ka + kb:]
    row = lambda i: (i, 0)
    full = lambda i: (0, 0)
    return pl.pallas_call(
        _out_proj_kernel,
        grid=(m // tm,),
        in_specs=[pl.BlockSpec((tm, ka), row), pl.BlockSpec((tm, kb), row), pl.BlockSpec((tm, kc), row),
                  pl.BlockSpec((ka, d), full), pl.BlockSpec((kb, d), full), pl.BlockSpec((kc, d), full),
                  pl.BlockSpec((tm, d), row)],
        out_specs=pl.BlockSpec((tm, d), row),
        out_shape=jax.ShapeDtypeStruct((m, d), F32),
        compiler_params=_params("parallel"),
        name="out_proj",
    )(o_dn, o_dil, o_nsa, wa, wb, wc, x)


def _ffn_up_kernel(x_ref, xp_ref, g_ref, wg_ref, wv_ref, cg_ref, cv_ref, o_ref, h_ref, *, tiles_per_seq):
    tm = x_ref.shape[0]
    halo = xp_ref.shape[0]

    @pl.when(pl.program_id(1) == 0)
    def _():
        def norm(x):
            ms = jnp.mean(x * x, axis=-1, keepdims=True)
            return (x * lax.rsqrt(ms + EPS) * g_ref[...]).astype(BF16)
        first = (pl.program_id(0) % tiles_per_seq) == 0
        hp = norm(xp_ref[...])
        h_ref[0:halo, :] = jnp.where(first, jnp.zeros_like(hp), hp)
        h_ref[halo:halo + tm, :] = norm(x_ref[...])

    h = h_ref[...]

    def conv(w_ref, c_ref):
        u = _dot(h, w_ref[...])
        y = u * c_ref[2:3, :]
        y = y + pltpu.roll(u, 1, axis=0) * c_ref[1:2, :]
        y = y + pltpu.roll(u, 2, axis=0) * c_ref[0:1, :]
        return y[halo:, :]

    gate = conv(wg_ref, cg_ref)
    val = conv(wv_ref, cv_ref)
    o_ref[...] = (gate * _sigmoid(gate) * val).astype(o_ref.dtype)


def _ffn_up(x, g, w_up, conv_w, seq, tm, tn):
    m, d = x.shape
    f = w_up.shape[1] // 2
    halo = 8
    nf = f // tn
    kern = functools.partial(_ffn_up_kernel, tiles_per_seq=seq // tm)
    return pl.pallas_call(
        kern,
        grid=(m // tm, nf),
        in_specs=[pl.BlockSpec((tm, d), lambda i, j: (i, 0)),
                  pl.BlockSpec((halo, d), lambda i, j: (jnp.maximum(i * (tm // halo) - 1, 0), 0)),
                  pl.BlockSpec((1, d), lambda i, j: (0, 0)),
                  pl.BlockSpec((d, tn), lambda i, j: (0, j)),
                  pl.BlockSpec((d, tn), lambda i, j: (0, j + nf)),
                  pl.BlockSpec((3, tn), lambda i, j: (0, j)),
                  pl.BlockSpec((3, tn), lambda i, j: (0, j + nf))],
        out_specs=pl.BlockSpec((tm, tn), lambda i, j: (i, j)),
        out_shape=jax.ShapeDtypeStruct((m, f), BF16),
        scratch_shapes=[pltpu.VMEM((tm + halo, d), BF16)],
        compiler_params=_params("parallel", "arbitrary"),
        name="ffn_up",
    )(x, x, g.reshape(1, d), w_up, w_up, conv_w, conv_w)


def _ffn_down_kernel(a_ref, w_ref, x_ref, o_ref):
    o_ref[...] = x_ref[...] + _dot(a_ref[...], w_ref[...])


def _ffn_down(act, w_down, x, tm, tn):
    m, f = act.shape
    d = w_down.shape[1]
    return pl.pallas_call(
        _ffn_down_kernel,
        grid=(d // tn, m // tm),
        in_specs=[pl.BlockSpec((tm, f), lambda j, i: (i, 0)),
                  pl.BlockSpec((f, tn), lambda j, i: (0, j)),
                  pl.BlockSpec((tm, tn), lambda j, i: (i, j))],
        out_specs=pl.BlockSpec((tm, tn), lambda j, i: (i, j)),
        out_shape=jax.ShapeDtypeStruct((m, d), F32),
        compiler_params=_params("parallel", "parallel"),
        name="ffn_down",
    )(act, w_down, x)


def _rmsnorm_kernel(x_ref, g_ref, o_ref):
    x = x_ref[...]
    ms = jnp.mean(x * x, axis=-1, keepdims=True)
    o_ref[...] = x * lax.rsqrt(ms + EPS) * g_ref[...]


def _rmsnorm(x, g, tm):
    m, d = x.shape
    return pl.pallas_call(
        _rmsnorm_kernel,
        grid=(m // tm,),
        in_specs=[pl.BlockSpec((tm, d), lambda i: (i, 0)), pl.BlockSpec((1, d), lambda i: (0, 0))],
        out_specs=pl.BlockSpec((tm, d), lambda i: (i, 0)),
        out_shape=jax.ShapeDtypeStruct((m, d), F32),
        compiler_params=_params("parallel"),
        name="final_norm",
    )(x, g.reshape(1, d))


def _dn_kernel(alog_ref, dtb_ref, q_ref, k_ref, v_ref, z_ref, s_ref, cq_ref, ck_ref, cv_ref, ng_ref,
               o_ref, lhs_s, n_s, op_s, egl_s, o_s):
    h = pl.program_id(1)
    seq = q_ref.shape[0]
    c = DN_CHUNK
    nc = seq // c
    row = lax.broadcasted_iota(jnp.int32, (seq, 1), 0)

    def conv_silu(x_ref, c_ref):
        x = x_ref[...]
        taps = c_ref.shape[0]
        acc = x * c_ref[taps - 1:taps, :]
        for s in range(1, taps):
            xs = jnp.where(row >= s, pltpu.roll(x, s, axis=0), 0.0)
            acc = acc + xs * c_ref[taps - 1 - s:taps - s, :]
        return acc * _sigmoid(acc)

    def l2norm(x):
        return x * lax.rsqrt(jnp.sum(x * x, axis=-1, keepdims=True) + EPS)

    q = l2norm(conv_silu(q_ref, cq_ref)) * (HEAD_DIM ** -0.5)
    k = l2norm(conv_silu(k_ref, ck_ref))
    v = conv_silu(v_ref, cv_ref)

    lane = lax.broadcasted_iota(jnp.int32, (1, LANES), 1)
    sm = s_ref[...]
    b_col = jnp.sum(jnp.where(lane == LANE_B + h, sm, 0.0), axis=-1, keepdims=True)
    a_col = jnp.sum(jnp.where(lane == LANE_A + h, sm, 0.0), axis=-1, keepdims=True)
    beta = _sigmoid(b_col)
    ap = a_col + dtb_ref[h]
    softplus = jnp.maximum(ap, 0.0) + jnp.log1p(jnp.exp(-jnp.abs(ap)))
    neg_rate = -jnp.exp(jnp.full((1, 1), alog_ref[h], F32))
    g = jnp.broadcast_to(neg_rate * softplus, (seq, LANES))

    pos = row & (c - 1)
    gc = g
    s = 1
    while s < c:
        gc = gc + jnp.where(pos >= s, pltpu.roll(gc, s, axis=0), 0.0)
        s *= 2

    egc = jnp.exp(gc)
    gc3 = gc.reshape(nc, c, LANES)
    glast3 = jnp.broadcast_to(gc3[:, c - 1:c, :], (nc, c, LANES))
    kb = k * beta
    k3 = k.reshape(nc, c, HEAD_DIM)
    kdt = jnp.swapaxes(k3 * jnp.exp(glast3 - gc3), 1, 2)
    egl_s[...] = jnp.exp(glast3[:, 0:8, :])

    ii = lax.broadcasted_iota(jnp.int32, (c, c), 0)
    jj = lax.broadcasted_iota(jnp.int32, (c, c), 1)
    gcol = gc3[:, :, 0:c]
    grow = jnp.swapaxes(gc3, 1, 2)[:, 0:c, :]
    decay = jnp.where(ii >= jj, jnp.exp(jnp.minimum(gcol - grow, 0.0)), 0.0)

    attn = _bmm_nt(q.reshape(nc, c, HEAD_DIM), k3) * decay
    a_mat = jnp.where(ii > jj, _bmm_nt(kb.reshape(nc, c, HEAD_DIM), k3) * decay, 0.0)

    same = (ii & -DN_SUB) == (jj & -DN_SUB)
    eye = jnp.where(ii == jj, 1.0, 0.0).astype(F32)
    x1 = jnp.where(same, -a_mat, 0.0)
    a_off = jnp.where(same, 0.0, a_mat)
    t_d = eye + x1
    xp = x1
    p = 2
    while p < DN_SUB:
        xp = _bmm(xp, xp)
        t_d = t_d + _bmm(t_d, xp)
        p *= 2
    y1 = -_bmm(t_d, a_off)
    t_o = eye + y1
    yp = y1
    p = 2
    while p < c // DN_SUB:
        yp = _bmm(yp, yp)
        t_o = t_o + _bmm(t_o, yp)
        p *= 2
    t_mat = _bmm(t_o, t_d)

    u = _bmm(t_mat, (v * beta).reshape(nc, c, HEAD_DIM))
    w = _bmm(t_mat, (kb * egc).reshape(nc, c, HEAD_DIM))

    lhs_s[:, 0:HEAD_DIM, :] = _bf(-_bmm(kdt, w))
    lhs_s[:, HEAD_DIM:HEAD_DIM + c, :] = _bf((q * egc).reshape(nc, c, HEAD_DIM) - _bmm(attn, w))
    n_s[...] = _bmm(kdt, u)
    op_s[...] = _bmm(attn, u)

    def chunk_step(n, state):
        r = _dot(lhs_s[n], _bf(state))
        o_s[pl.ds(pl.multiple_of(n * c, c), c), :] = r[HEAD_DIM:, :] + op_s[n]
        return state * egl_s[n][0:1, :] + r[0:HEAD_DIM, :] + n_s[n]

    lax.fori_loop(0, nc, chunk_step, jnp.zeros((HEAD_DIM, HEAD_DIM), F32))

    o = o_s[...]
    y = o * lax.rsqrt(jnp.mean(o * o, axis=-1, keepdims=True) + EPS) * ng_ref[...]
    z = z_ref[...]
    o_ref[...] = (y * (z * _sigmoid(z))).astype(o_ref.dtype)


def _deltanet(proj, dn_conv, a_log, dt_bias, norm_g):
    bsz, seq, _ = proj.shape
    nc = seq // DN_CHUNK
    col = lambda off: pl.BlockSpec((None, seq, LANES), lambda b, h: (b, 0, off + h))
    cw = lambda off: pl.BlockSpec((dn_conv.shape[0], LANES), lambda b, h: (0, off + h))
    smem = pl.BlockSpec(memory_space=pltpu.SMEM)
    return pl.pallas_call(
        _dn_kernel,
        grid=(bsz, DN_HEADS),
        in_specs=[smem, smem, col(CB_DN_Q), col(CB_DN_K), col(CB_DN_V), col(CB_DN_Z),
                  pl.BlockSpec((None, seq, LANES), lambda b, h: (b, 0, CB_SMALL)),
                  cw(CB_DN_Q), cw(CB_DN_K), cw(CB_DN_V),
                  pl.BlockSpec((1, HEAD_DIM), lambda b, h: (0, 0))],
        out_specs=pl.BlockSpec((None, seq, HEAD_DIM), lambda b, h: (b, 0, h)),
        out_shape=jax.ShapeDtypeStruct((bsz, seq, DN_HEADS * HEAD_DIM), BF16),
        scratch_shapes=[pltpu.VMEM((nc, HEAD_DIM + DN_CHUNK, HEAD_DIM), BF16),
                        pltpu.VMEM((nc, HEAD_DIM, HEAD_DIM), F32),
                        pltpu.VMEM((nc, DN_CHUNK, HEAD_DIM), F32),
                        pltpu.VMEM((nc, 8, LANES), F32),
                        pltpu.VMEM((seq, HEAD_DIM), F32)],
        compiler_params=_params("parallel", "parallel"),
        name="deltanet",
    )(a_log, dt_bias, proj, proj, proj, proj, proj, dn_conv, dn_conv, dn_conv, norm_g.reshape(1, HEAD_DIM))


def _dil_kernel(tab_ref, q_ref, k_ref, v_ref, o_ref, num_s, m_s, l_s):
    head = pl.program_id(1)
    seq = q_ref.shape[0]
    qb = Q_BLOCK
    scale = HEAD_DIM ** -0.5

    r = lax.broadcasted_iota(jnp.int32, (qb, 2 * qb), 0)
    cidx = lax.broadcasted_iota(jnp.int32, (qb, 2 * qb), 1)
    sub = qb + r - cidx

    def band_tile(window, dil):
        valid = (sub >= 0) & (sub <= window // dil)
        return jnp.where(valid, _bias_of_dist(jnp.maximum(sub, 0) * dil, tab_ref, head), NEG)

    def block(qv, kv, vv, tile):
        s = _dot_nt((qv * scale).astype(BF16), kv.astype(BF16)) + tile
        m = jnp.max(s, axis=-1, keepdims=True)
        p = jnp.exp(s - m)
        l = jnp.sum(p, axis=-1, keepdims=True)
        return _dot(p.astype(BF16), vv.astype(BF16)), m, l

    for pat, (window, dil) in enumerate(DIL_PATTERNS):
        assert window // dil == qb
        tile = band_tile(window, dil)
        sub_len = seq // dil
        for res in range(dil):
            for bi in range(sub_len // qb):
                def rows(first_blk, nblk):
                    start = res + dil * qb * first_blk
                    return pl.ds(start, nblk * qb, stride=dil) if dil > 1 else pl.ds(start, nblk * qb)
                qr = rows(bi, 1)
                if bi == 0:
                    kr, t = rows(0, 1), tile[:, qb:]
                else:
                    kr, t = rows(bi - 1, 2), tile
                num, m, l = block(q_ref[qr, :], k_ref[kr, :], v_ref[kr, :], t)
                if pat == 0:
                    num_s[qr, :] = num
                    m_s[qr, :] = jnp.broadcast_to(m, (qb, LANES))
                    l_s[qr, :] = jnp.broadcast_to(l, (qb, LANES))
                else:
                    m_old = m_s[qr, :]
                    m_new = jnp.maximum(m_old, m)
                    a_old = jnp.exp(m_old - m_new)
                    a_new = jnp.exp(m - m_new)
                    num_s[qr, :] = num_s[qr, :] * a_old + num * a_new
                    l_s[qr, :] = l_s[qr, :] * a_old + l * a_new
                    m_s[qr, :] = m_new

    o_ref[...] = (num_s[...] / l_s[...]).astype(o_ref.dtype)


def _dilated(proj, dil_tab):
    bsz, seq, _ = proj.shape
    col = lambda off: pl.BlockSpec((None, seq, LANES), lambda b, h: (b, 0, off + h))
    return pl.pallas_call(
        _dil_kernel,
        grid=(bsz, DIL_HEADS),
        in_specs=[pl.BlockSpec(memory_space=pltpu.SMEM), col(CB_DIL_Q), col(CB_DIL_K), col(CB_DIL_V)],
        out_specs=pl.BlockSpec((None, seq, HEAD_DIM), lambda b, h: (b, 0, h)),
        out_shape=jax.ShapeDtypeStruct((bsz, seq, DIL_HEADS * HEAD_DIM), BF16),
        scratch_shapes=[pltpu.VMEM((seq, HEAD_DIM), F32), pltpu.VMEM((seq, LANES), F32),
                        pltpu.VMEM((seq, LANES), F32)],
        compiler_params=_params("parallel", "parallel"),
        name="dilated",
    )(dil_tab, proj, proj, proj)


def _cmp_kernel(x_ref, pos_ref, w1_ref, w2_ref, o_ref):
    nblk = o_ref.shape[0]
    half = CMP_LEN // 2
    ha = jnp.zeros((nblk, CMP_HIDDEN), F32)
    hb = jnp.zeros((nblk, CMP_HIDDEN), F32)
    for l in range(half):
        xl = x_ref[pl.ds(l, nblk, stride=CMP_STRIDE), :]
        wa = w1_ref[l * HEAD_DIM:(l + 1) * HEAD_DIM, :]
        wb = w1_ref[(half + l) * HEAD_DIM:(half + l + 1) * HEAD_DIM, :]
        ha = ha + _dot((xl + pos_ref[l:l + 1, :]).astype(BF16), wa)
        hb = hb + _dot((xl + pos_ref[half + l:half + l + 1, :]).astype(BF16), wb)
    hmid = ha + pltpu.roll(hb, nblk - 1, axis=0)
    hmid = 0.5 * hmid * (1.0 + jnp.tanh(math.sqrt(2.0 / math.pi) * (hmid + 0.044715 * hmid * hmid * hmid)))
    out = _dot(hmid.astype(BF16), w2_ref[...])
    rowi = lax.broadcasted_iota(jnp.int32, out.shape, 0)
    o_ref[...] = jnp.where(rowi < nblk - 1, out, 0.0)


def _compress(proj, cmp_pos, w1, w2):
    bsz, seq, _ = proj.shape
    nblk = seq // CMP_STRIDE
    return pl.pallas_call(
        _cmp_kernel,
        grid=(2, bsz, NSA_GROUPS),
        in_specs=[pl.BlockSpec((None, seq, LANES), lambda i, b, g: (b, 0, CB_NSA_KV + i * NSA_GROUPS + g)),
                  pl.BlockSpec((None, CMP_LEN, HEAD_DIM), lambda i, b, g: (i, 0, 0)),
                  pl.BlockSpec((None, CMP_LEN * HEAD_DIM, CMP_HIDDEN), lambda i, b, g: (i, 0, 0)),
                  pl.BlockSpec((None, CMP_HIDDEN, HEAD_DIM), lambda i, b, g: (i, 0, 0))],
        out_specs=pl.BlockSpec((None, None, None, nblk, HEAD_DIM), lambda i, b, g: (b, g, i, 0, 0)),
        out_shape=jax.ShapeDtypeStruct((bsz, NSA_GROUPS, 2, nblk, HEAD_DIM), F32),
        compiler_params=_params("parallel", "parallel", "parallel"),
        name="nsa_compress",
    )(proj, cmp_pos, w1, w2)


def _nsa_kernel(tab_ref, q0_ref, q1_ref, q2_ref, ks_ref, vs_ref, kw_ref, vw_ref, cmp_ref, s_ref,
                o_ref, strip_s, cbias_s, wtile_s, q_s, kv_s, oslc_s):
    grp = pl.program_id(0)
    i = pl.program_id(2)
    seq = q0_ref.shape[0]
    qb = Q_BLOCK
    nqb = seq // qb
    n_slc = seq // SLC_BLOCK
    ncmp = cmp_ref.shape[1]
    nwin = WIN // qb
    win_keys = WIN + qb
    scale = HEAD_DIM ** -0.5
    q_refs = (q0_ref, q1_ref, q2_ref)
    head_rows = lambda x, rr: x[rr * qb:(rr + 1) * qb, :]

    @pl.when(i == 0)
    def _():
        for rr in range(NSA_REP):
            q_s[rr] = (q_refs[rr][...] * scale).astype(BF16)
        for n, ref in enumerate((ks_ref, vs_ref, kw_ref, vw_ref)):
            kv_s[n] = ref[...].astype(BF16)

    @pl.when(jnp.logical_and(pl.program_id(1) == 0, i == 0))
    def _():
        for rr in range(NSA_REP):
            head = DIL_HEADS + grp * NSA_REP + rr
            shape = strip_s.shape[1:]
            dist = (lax.broadcasted_iota(jnp.int32, shape, 0) - lax.broadcasted_iota(jnp.int32, shape, 1)
                    + (seq - qb))
            strip_s[rr] = jnp.where(dist >= 0, _bias_of_dist(jnp.maximum(dist, 0), tab_ref, head), NEG)
            shape = cbias_s.shape[1:]
            dist = (lax.broadcasted_iota(jnp.int32, shape, 0)
                    - (lax.broadcasted_iota(jnp.int32, shape, 1) * CMP_STRIDE + CMP_LEN - 1))
            cbias_s[rr] = jnp.where(dist >= 0, _bias_of_dist(jnp.maximum(dist, 0), tab_ref, head), NEG)
            shape = wtile_s.shape[1:]
            dist = lax.broadcasted_iota(jnp.int32, shape, 0) + WIN - lax.broadcasted_iota(jnp.int32, shape, 1)
            wtile_s[rr] = jnp.where((dist >= 0) & (dist < WIN),
                                    _bias_of_dist(jnp.clip(dist, 0, WIN), tab_ref, head), NEG)

    rows = pl.ds(pl.multiple_of(i * qb, qb), qb)
    q3 = jnp.concatenate([q_s[rr, rows, :] for rr in range(NSA_REP)], axis=0)
    sm = s_ref[rows, :]
    lane = lax.broadcasted_iota(jnp.int32, (1, LANES), 1)

    def gate(rr, branch):
        ln = LANE_GATE + (grp * NSA_REP + rr) * 3 + branch
        return _sigmoid(jnp.sum(jnp.where(lane == ln, sm, 0.0), axis=-1, keepdims=True))

    def softmax_rows(s):
        m = jnp.max(s, axis=-1, keepdims=True)
        p = jnp.exp(s - m)
        return p, jnp.sum(p, axis=-1, keepdims=True)

    s_all = _dot_nt(q3, _bf(cmp_ref[0]))
    p_sum = jnp.zeros((qb, ncmp), F32)
    probs = []
    for rr in range(NSA_REP):
        cb = cbias_s[rr, rows, :]
        p, l = softmax_rows(head_rows(s_all, rr) + cb)
        p = jnp.where(cb > 0.5 * NEG, p, 0.0)
        p = p / jnp.maximum(jnp.sum(p, axis=-1, keepdims=True), 1e-30)
        p_sum = p_sum + p
        probs.append(_bf(p))
    o_cmp = _dot(jnp.concatenate(probs, axis=0), _bf(cmp_ref[1]))
    acc = [gate(rr, 0) * head_rows(o_cmp, rr) for rr in range(NSA_REP)]

    oj = lax.broadcasted_iota(jnp.int32, (n_slc, ncmp), 0) * SLC_BLOCK
    oc = lax.broadcasted_iota(jnp.int32, (n_slc, ncmp), 1) * CMP_STRIDE
    overlap_t = (jnp.maximum(jnp.minimum(oc + CMP_LEN, oj + SLC_BLOCK) - jnp.maximum(oc, oj), 0)
                 .astype(F32) / CMP_STRIDE)
    jrow = lax.broadcasted_iota(jnp.int32, (n_slc, qb), 0)
    tcol = lax.broadcasted_iota(jnp.int32, (n_slc, qb), 1)
    imp = _dot_nt(overlap_t, p_sum, precision=HIGHEST)
    cur = jnp.right_shift(i * qb + tcol, SLC_SHIFT)
    forced = (jrow == 0) | ((jrow <= cur) & (jrow > cur - 2))
    imp = jnp.where(forced, jnp.inf, jnp.where(jrow <= cur, imp, -jnp.inf))
    cnt = jnp.zeros((n_slc, qb), F32)
    for j2 in range(n_slc):
        other = imp[j2:j2 + 1, :]
        ge = jnp.where(other >= imp, 1.0, 0.0)
        gt = jnp.where(other > imp, 1.0, 0.0)
        cnt = cnt + jnp.where(jrow > j2, ge, gt)
    sel_t = jnp.where((cnt < SLC_TOPK) & (jrow <= cur), 1.0, 0.0)
    sel_t = jnp.concatenate([sel_t, jnp.zeros((LANES - n_slc, qb), F32)], axis=0)
    sel = _bf(sel_t.T)

    strip_start = pl.multiple_of((nqb - 1 - i) * qb, qb)
    blocks_per_tile = SLC_KEY_TILE // qb
    for kq in range(seq // SLC_KEY_TILE):
        @pl.when(i // blocks_per_tile == kq)
        def _(kq=kq):
            nk = (kq + 1) * SLC_KEY_TILE
            s_all = _dot_nt(q3, kv_s[0, 0:nk, :])
            ej = lax.broadcasted_iota(jnp.int32, (LANES, nk), 0)
            ep = lax.broadcasted_iota(jnp.int32, (LANES, nk), 1)
            expand = jnp.where(ej == jnp.right_shift(ep, SLC_SHIFT), 1.0, 0.0).astype(BF16)
            keep = _dot(sel, expand) > 0.5
            probs, dens = [], []
            for rr in range(NSA_REP):
                s = head_rows(s_all, rr) + strip_s[rr, :, pl.ds(strip_start, nk)]
                p, l = softmax_rows(jnp.where(keep, s, NEG))
                probs.append(_bf(p))
                dens.append(l)
            pv = _dot(jnp.concatenate(probs, axis=0), kv_s[1, 0:nk, :])
            for rr in range(NSA_REP):
                oslc_s[rr] = head_rows(pv, rr) / dens[rr]

    j0 = jnp.maximum(i - nwin, 0)
    krows = pl.ds(pl.multiple_of(j0 * qb, qb), win_keys)
    tile_start = pl.multiple_of((j0 - i + nwin) * qb, qb)
    s_all = _dot_nt(q3, kv_s[2, krows, :])
    probs, dens = [], []
    for rr in range(NSA_REP):
        p, l = softmax_rows(head_rows(s_all, rr) + wtile_s[rr, :, pl.ds(tile_start, win_keys)])
        probs.append(_bf(p))
        dens.append(l)
    pv = _dot(jnp.concatenate(probs, axis=0), kv_s[3, krows, :])
    for rr in range(NSA_REP):
        out = acc[rr] + gate(rr, 1) * oslc_s[rr] + gate(rr, 2) * (head_rows(pv, rr) / dens[rr])
        o_ref[:, rr * HEAD_DIM:(rr + 1) * HEAD_DIM] = out.astype(o_ref.dtype)


def _nsa(proj, cmp_kv, rel_bias):
    bsz, seq, _ = proj.shape
    ncmp = cmp_kv.shape[3]
    nqb = seq // Q_BLOCK
    strip_w = (nqb - 1) * Q_BLOCK + SLC_KEY_TILE
    wtile_w = WIN + WIN + Q_BLOCK
    col = lambda fn: pl.BlockSpec((None, seq, LANES), lambda g, b, i: (b, 0, fn(g)))
    kv = lambda branch, which: col(lambda g: CB_NSA_KV + (branch * 2 + which) * NSA_GROUPS + g)
    return pl.pallas_call(
        _nsa_kernel,
        grid=(NSA_GROUPS, bsz, nqb),
        in_specs=[pl.BlockSpec(memory_space=pltpu.SMEM),
                  col(lambda g: CB_NSA_Q + g * NSA_REP), col(lambda g: CB_NSA_Q + g * NSA_REP + 1),
                  col(lambda g: CB_NSA_Q + g * NSA_REP + 2),
                  kv(1, 0), kv(1, 1), kv(2, 0), kv(2, 1),
                  pl.BlockSpec((None, None, 2, ncmp, HEAD_DIM), lambda g, b, i: (b, g, 0, 0, 0)),
                  pl.BlockSpec((None, seq, LANES), lambda g, b, i: (b, 0, CB_SMALL))],
        out_specs=pl.BlockSpec((None, Q_BLOCK, NSA_REP * HEAD_DIM), lambda g, b, i: (b, i, g)),
        out_shape=jax.ShapeDtypeStruct((bsz, seq, NSA_GROUPS * NSA_REP * HEAD_DIM), BF16),
        scratch_shapes=[pltpu.VMEM((NSA_REP, Q_BLOCK, strip_w), F32),
                        pltpu.VMEM((NSA_REP, seq, ncmp), F32),
                        pltpu.VMEM((NSA_REP, Q_BLOCK, wtile_w), F32),
                        pltpu.VMEM((NSA_REP, seq, HEAD_DIM), BF16),
                        pltpu.VMEM((4, seq, HEAD_DIM), BF16),
                        pltpu.VMEM((NSA_REP, Q_BLOCK, HEAD_DIM), F32)],
        compiler_params=_params("arbitrary", "arbitrary", "arbitrary"),
        name="nsa_attention",
    )(rel_bias, proj, proj, proj, proj, proj, proj, proj, cmp_kv, proj)


IN_WIDE_A = 3072
IN_SMALL_A = 12
IN_WIDE_B = 3840
IN_SMALL_B = 18


def _w_in_prep_kernel(a_ref, b_ref, c_ref, o_ref):
    j = pl.program_id(0)
    depth = o_ref.shape[0]
    blk = o_ref.shape[1]
    n_a = IN_WIDE_A // blk
    n_b = IN_WIDE_B // blk
    s = IN_SMALL_A

    def put(lo, hi, src_ref, src_lo):
        for l in range(depth):
            o_ref[l, lo:hi, :] = src_ref[src_lo:src_lo + hi - lo, l, :].astype(o_ref.dtype)

    @pl.when(j < n_a)
    def _():
        put(0, blk, a_ref, 0)

    @pl.when(jnp.logical_and(j >= n_a, j < n_a + n_b))
    def _():
        put(0, blk - s, a_ref, s)
        put(blk - s, blk, b_ref, 0)

    @pl.when(j == n_a + n_b)
    def _():
        put(0, s, c_ref, 0)
        put(s, s + IN_SMALL_B, a_ref, s)
        o_ref[:, s + IN_SMALL_B:, :] = jnp.zeros((depth, blk - s - IN_SMALL_B, o_ref.shape[2]), o_ref.dtype)

    @pl.when(j > n_a + n_b)
    def _():
        o_ref[...] = jnp.zeros(o_ref.shape, o_ref.dtype)


def _prep_w_in(w):
    depth, d, cols = w.shape
    blk = LANES
    assert cols == IN_WIDE_A + IN_SMALL_A + IN_WIDE_B + IN_SMALL_B
    wt = jnp.transpose(w, (2, 0, 1))
    last = (cols - 1) // blk
    src = lambda off: pl.BlockSpec((blk, depth, d), lambda j: (jnp.minimum(j + off, last), 0, 0))
    return pl.pallas_call(
        _w_in_prep_kernel,
        grid=(PROJ_COLS // blk,),
        in_specs=[src(0), src(1), pl.BlockSpec((blk, depth, d), lambda j: (IN_WIDE_A // blk, 0, 0))],
        out_specs=pl.BlockSpec((depth, blk, d), lambda j: (0, j, 0)),
        out_shape=jax.ShapeDtypeStruct((depth, PROJ_COLS, d), BF16),
        compiler_params=_params("parallel"),
        name="w_in_prep",
    )(wt, wt, wt)


def kernel(x, norm1_g, w_in, dn_conv, dn_a_log, dn_dt_bias, dn_norm_g, cmp_pos, cmp_w1, cmp_w2, w_out,
           norm2_g, ffn_up, ffn_conv, ffn_down, rel_bias, final_g):
    bsz, seq, d = x.shape
    depth = w_in.shape[0]
    w_in_b = _prep_w_in(w_in)
    cmp_w1_b, cmp_w2_b = cmp_w1.astype(BF16), cmp_w2.astype(BF16)
    w_out_b, ffn_up_b, ffn_down_b = w_out.astype(BF16), ffn_up.astype(BF16), ffn_down.astype(BF16)
    xf = x.reshape(bsz * seq, d)
    for l in range(depth):
        proj = _norm_matmul(xf, norm1_g[l], w_in_b[l], 512, 1024, F32).reshape(bsz, seq, PROJ_COLS)
        o_dn = _deltanet(proj, dn_conv[l], dn_a_log[l], dn_dt_bias[l], dn_norm_g[l])
        o_dil = _dilated(proj, rel_bias)
        cmp_kv = _compress(proj, cmp_pos[l], cmp_w1_b[l], cmp_w2_b[l])
        o_nsa = _nsa(proj, cmp_kv, rel_bias)
        xf = _out_proj(o_dn.reshape(bsz * seq, -1), o_dil.reshape(bsz * seq, -1), o_nsa.reshape(bsz * seq, -1),
                       w_out_b[l], xf, 512)
        act = _ffn_up(xf, norm2_g[l], ffn_up_b[l], ffn_conv[l], seq, 512, 512)
        xf = _ffn_down(act, ffn_down_b[l], xf, 512, 512)
    return _rmsnorm(xf, final_g, 512).reshape(bsz, seq, d)
```

```python
import functools
import math

import numpy as np
import jax
import jax.numpy as jnp
from jax import lax
from jax.experimental import pallas as pl
from jax.experimental.pallas import tpu as pltpu

F32 = jnp.float32
BF16 = jnp.bfloat16
HIGHEST = lax.Precision.HIGHEST

LANES = 128
HEAD_DIM = 128
DN_HEADS = 6
DIL_HEADS = 4
NSA_GROUPS = 2
NSA_REP = 3
DN_CHUNK = 64
DN_SUB = 16
CMP_LEN = 32
CMP_STRIDE = 16
CMP_HIDDEN = 256
SLC_BLOCK = 64
SLC_SHIFT = 6
SLC_TOPK = 16
SLC_KEY_TILE = 512
NSA_BLOCKS_PER_STEP = 2
WIN = 512
Q_BLOCK = 128
DIL_PATTERNS = ((128, 1), (512, 4), (2048, 16))
REL_BUCKETS = 32
REL_MAX_DIST = 2048
EPS = 1e-6
NEG = -1e30
VMEM_LIMIT = 56 * 1024 * 1024

CB_DN_Q, CB_DN_K, CB_DN_V, CB_DN_Z = 0, 6, 12, 18
CB_DIL_Q, CB_DIL_K, CB_DIL_V = 24, 28, 32
CB_NSA_Q = 36
CB_NSA_KV = 42
CB_SMALL = 54
PROJ_COLS = 7168
LANE_B, LANE_A, LANE_GATE = 0, 6, 12


def _bucket_thresholds():
    n = np.arange(0, REL_MAX_DIST + 1)
    max_exact = REL_BUCKETS // 2
    out = []
    for dt in (np.float32, np.float64):
        nf = np.maximum(n, 1).astype(dt)
        large = max_exact + (np.log(nf / dt(max_exact)) / dt(math.log(REL_MAX_DIST / max_exact))
                             * dt(REL_BUCKETS - max_exact)).astype(np.int32)
        large = np.minimum(large, REL_BUCKETS - 1)
        out.append(np.where(n < max_exact, n, large))
    assert np.array_equal(out[0], out[1])
    bucket = out[1]
    assert np.all(np.diff(bucket) >= 0)
    thr = [0]
    for b in range(1, REL_BUCKETS):
        idx = np.nonzero(bucket >= b)[0]
        thr.append(int(idx[0]) if idx.size else REL_MAX_DIST + 1)
    return tuple(thr)


_THR = _bucket_thresholds()


def _bias_of_dist(dist, tab_ref, head):
    val = jnp.full(dist.shape, tab_ref[head, 0], F32)
    for b in range(1, REL_BUCKETS):
        val = jnp.where(dist >= _THR[b], tab_ref[head, b], val)
    return val


def _sigmoid(x):
    return 1.0 / (1.0 + jnp.exp(-x))


def _bf(x):
    return x.astype(BF16)


def _dot(a, b, **kw):
    return jnp.dot(a, b, preferred_element_type=F32, **kw)


def _dot_nt(a, b, **kw):
    return lax.dot_general(a, b, (((1,), (1,)), ((), ())), preferred_element_type=F32, **kw)


def _bmm(a, b):
    return jnp.einsum('nij,njk->nik', _bf(a), _bf(b), preferred_element_type=F32)


def _bmm_nt(a, b):
    return jnp.einsum('nid,njd->nij', _bf(a), _bf(b), preferred_element_type=F32)


def _params(*sem):
    return pltpu.CompilerParams(dimension_semantics=sem, vmem_limit_bytes=VMEM_LIMIT)


def _norm_matmul_kernel(x_ref, g_ref, w_ref, o_ref, h_ref):
    @pl.when(pl.program_id(1) == 0)
    def _():
        x = x_ref[...]
        ms = jnp.mean(x * x, axis=-1, keepdims=True)
        h_ref[...] = (x * lax.rsqrt(ms + EPS) * g_ref[...]).astype(BF16)

    o_ref[...] = _dot_nt(h_ref[...], w_ref[...]).astype(o_ref.dtype)


def _norm_matmul(x, g, w_t, layer, tm, tn, out_dtype):
    m, d = x.shape
    n = w_t.shape[1]
    return pl.pallas_call(
        _norm_matmul_kernel,
        grid=(m // tm, n // tn),
        in_specs=[pl.BlockSpec((tm, d), lambda i, j: (i, 0)),
                  pl.BlockSpec((1, d), lambda i, j: (0, 0)),
                  pl.BlockSpec((None, tn, d), lambda i, j: (layer, j, 0))],
        out_specs=pl.BlockSpec((tm, tn), lambda i, j: (i, j)),
        out_shape=jax.ShapeDtypeStruct((m, n), out_dtype),
        scratch_shapes=[pltpu.VMEM((tm, d), BF16)],
        compiler_params=_params("parallel", "arbitrary"),
        name="in_proj",
    )(x, g.reshape(1, d), w_t)


def _out_proj_kernel(a_ref, b_ref, c_ref, wa_ref, wb_ref, wc_ref, x_ref, o_ref):
    acc = _dot(a_ref[...], wa_ref[...])
    acc = acc + _dot(b_ref[...], wb_ref[...])
    acc = acc + _dot(c_ref[...], wc_ref[...])
    o_ref[...] = x_ref[...] + acc


def _out_proj(o_dn, o_dil, o_nsa, w_out, x, tm):
    m, d = x.shape
    ka, kb, kc = o_dn.shape[1], o_dil.shape[1], o_nsa.shape[1]
    wa, wb, wc = w_out[:ka], w_out[ka:ka + kb], w_out[ka + kb:]
    row = lambda i: (i, 0)
    full = lambda i: (0, 0)
    return pl.pallas_call(
        _out_proj_kernel,
        grid=(m // tm,),
        in_specs=[pl.BlockSpec((tm, ka), row), pl.BlockSpec((tm, kb), row), pl.BlockSpec((tm, kc), row),
                  pl.BlockSpec((ka, d), full), pl.BlockSpec((kb, d), full), pl.BlockSpec((kc, d), full),
                  pl.BlockSpec((tm, d), row)],
        out_specs=pl.BlockSpec((tm, d), row),
        out_shape=jax.ShapeDtypeStruct((m, d), F32),
        compiler_params=_params("parallel"),
        name="out_proj",
    )(o_dn, o_dil, o_nsa, wa, wb, wc, x)


def _ffn_up_kernel(x_ref, xp_ref, g_ref, wg_ref, wv_ref, cg_ref, cv_ref, o_ref, h_ref, *, tiles_per_seq):
    tm = x_ref.shape[0]
    halo = xp_ref.shape[0]

    @pl.when(pl.program_id(1) == 0)
    def _():
        def norm(x):
            ms = jnp.mean(x * x, axis=-1, keepdims=True)
            return (x * lax.rsqrt(ms + EPS) * g_ref[...]).astype(BF16)
        first = (pl.program_id(0) % tiles_per_seq) == 0
        hp = norm(xp_ref[...])
        h_ref[0:halo, :] = jnp.where(first, jnp.zeros_like(hp), hp)
        h_ref[halo:halo + tm, :] = norm(x_ref[...])

    h = h_ref[...]

    def conv(w_ref, c_ref):
        u = _dot(h, w_ref[...])
        y = u * c_ref[2:3, :]
        y = y + pltpu.roll(u, 1, axis=0) * c_ref[1:2, :]
        y = y + pltpu.roll(u, 2, axis=0) * c_ref[0:1, :]
        return y[halo:, :]

    gate = conv(wg_ref, cg_ref)
    val = conv(wv_ref, cv_ref)
    o_ref[...] = (gate * _sigmoid(gate) * val).astype(o_ref.dtype)


def _ffn_up(x, g, w_up, conv_w, layer, seq, tm, tn):
    m, d = x.shape
    f = w_up.shape[2] // 2
    taps = conv_w.shape[1]
    halo = 8
    nf = f // tn
    kern = functools.partial(_ffn_up_kernel, tiles_per_seq=seq // tm)
    return pl.pallas_call(
        kern,
        grid=(m // tm, nf),
        in_specs=[pl.BlockSpec((tm, d), lambda i, j: (i, 0)),
                  pl.BlockSpec((halo, d), lambda i, j: (jnp.maximum(i * (tm // halo) - 1, 0), 0)),
                  pl.BlockSpec((1, d), lambda i, j: (0, 0)),
                  pl.BlockSpec((None, d, tn), lambda i, j: (layer, 0, j)),
                  pl.BlockSpec((None, d, tn), lambda i, j: (layer, 0, j + nf)),
                  pl.BlockSpec((None, taps, tn), lambda i, j: (layer, 0, j)),
                  pl.BlockSpec((None, taps, tn), lambda i, j: (layer, 0, j + nf))],
        out_specs=pl.BlockSpec((tm, tn), lambda i, j: (i, j)),
        out_shape=jax.ShapeDtypeStruct((m, f), BF16),
        scratch_shapes=[pltpu.VMEM((tm + halo, d), BF16)],
        compiler_params=_params("parallel", "arbitrary"),
        name="ffn_up",
    )(x, x, g.reshape(1, d), w_up, w_up, conv_w, conv_w)


def _ffn_down_kernel(a_ref, w_ref, x_ref, o_ref):
    o_ref[...] = x_ref[...] + _dot(a_ref[...], w_ref[...])


def _ffn_down(act, w_down, layer, x, tm, tn):
    m, f = act.shape
    d = w_down.shape[2]
    return pl.pallas_call(
        _ffn_down_kernel,
        grid=(d // tn, m // tm),
        in_specs=[pl.BlockSpec((tm, f), lambda j, i: (i, 0)),
                  pl.BlockSpec((None, f, tn), lambda j, i: (layer, 0, j)),
                  pl.BlockSpec((tm, tn), lambda j, i: (i, j))],
        out_specs=pl.BlockSpec((tm, tn), lambda j, i: (i, j)),
        out_shape=jax.ShapeDtypeStruct((m, d), F32),
        compiler_params=_params("parallel", "parallel"),
        name="ffn_down",
    )(act, w_down, x)


def _rmsnorm_kernel(x_ref, g_ref, o_ref):
    x = x_ref[...]
    ms = jnp.mean(x * x, axis=-1, keepdims=True)
    o_ref[...] = x * lax.rsqrt(ms + EPS) * g_ref[...]


def _rmsnorm(x, g, tm):
    m, d = x.shape
    return pl.pallas_call(
        _rmsnorm_kernel,
        grid=(m // tm,),
        in_specs=[pl.BlockSpec((tm, d), lambda i: (i, 0)), pl.BlockSpec((1, d), lambda i: (0, 0))],
        out_specs=pl.BlockSpec((tm, d), lambda i: (i, 0)),
        out_shape=jax.ShapeDtypeStruct((m, d), F32),
        compiler_params=_params("parallel"),
        name="final_norm",
    )(x, g.reshape(1, d))


def _dn_kernel(alog_ref, dtb_ref, q_ref, k_ref, v_ref, z_ref, s_ref, cq_ref, ck_ref, cv_ref, ng_ref,
               o_ref, lhs_s, n_s, op_s, egl_s, o_s):
    h = pl.program_id(1)
    seq = q_ref.shape[0]
    c = DN_CHUNK
    nc = seq // c
    row = lax.broadcasted_iota(jnp.int32, (seq, 1), 0)

    def conv_silu(x_ref, c_ref):
        x = x_ref[...]
        taps = c_ref.shape[0]
        acc = x * c_ref[taps - 1:taps, :]
        for s in range(1, taps):
            xs = jnp.where(row >= s, pltpu.roll(x, s, axis=0), 0.0)
            acc = acc + xs * c_ref[taps - 1 - s:taps - s, :]
        return acc * _sigmoid(acc)

    def l2norm(x):
        return x * lax.rsqrt(jnp.sum(x * x, axis=-1, keepdims=True) + EPS)

    q = l2norm(conv_silu(q_ref, cq_ref)) * (HEAD_DIM ** -0.5)
    k = l2norm(conv_silu(k_ref, ck_ref))
    v = conv_silu(v_ref, cv_ref)

    lane = lax.broadcasted_iota(jnp.int32, (1, LANES), 1)
    sm = s_ref[...]
    b_col = jnp.sum(jnp.where(lane == LANE_B + h, sm, 0.0), axis=-1, keepdims=True)
    a_col = jnp.sum(jnp.where(lane == LANE_A + h, sm, 0.0), axis=-1, keepdims=True)
    beta = _sigmoid(b_col)
    ap = a_col + dtb_ref[h]
    softplus = jnp.maximum(ap, 0.0) + jnp.log1p(jnp.exp(-jnp.abs(ap)))
    neg_rate = -jnp.exp(jnp.full((1, 1), alog_ref[h], F32))
    g = jnp.broadcast_to(neg_rate * softplus, (seq, LANES))

    pos = row & (c - 1)
    gc = g
    s = 1
    while s < c:
        gc = gc + jnp.where(pos >= s, pltpu.roll(gc, s, axis=0), 0.0)
        s *= 2

    egc = jnp.exp(gc)
    gc3 = gc.reshape(nc, c, LANES)
    glast3 = jnp.broadcast_to(gc3[:, c - 1:c, :], (nc, c, LANES))
    kb = k * beta
    k3 = k.reshape(nc, c, HEAD_DIM)
    kdt = jnp.swapaxes(k3 * jnp.exp(glast3 - gc3), 1, 2)
    egl_s[...] = jnp.exp(glast3[:, 0:8, :])

    ii = lax.broadcasted_iota(jnp.int32, (c, c), 0)
    jj = lax.broadcasted_iota(jnp.int32, (c, c), 1)
    gcol = gc3[:, :, 0:c]
    grow = jnp.swapaxes(gc3, 1, 2)[:, 0:c, :]
    decay = jnp.where(ii >= jj, jnp.exp(jnp.minimum(gcol - grow, 0.0)), 0.0)

    attn = _bmm_nt(q.reshape(nc, c, HEAD_DIM), k3) * decay
    a_mat = jnp.where(ii > jj, _bmm_nt(kb.reshape(nc, c, HEAD_DIM), k3) * decay, 0.0)

    same = (ii & -DN_SUB) == (jj & -DN_SUB)
    eye = jnp.where(ii == jj, 1.0, 0.0).astype(F32)
    x1 = jnp.where(same, -a_mat, 0.0)
    a_off = jnp.where(same, 0.0, a_mat)
    t_d = eye + x1
    xp = x1
    p = 2
    while p < DN_SUB:
        xp = _bmm(xp, xp)
        t_d = t_d + _bmm(t_d, xp)
        p *= 2
    y1 = -_bmm(t_d, a_off)
    t_o = eye + y1
    yp = y1
    p = 2
    while p < c // DN_SUB:
        yp = _bmm(yp, yp)
        t_o = t_o + _bmm(t_o, yp)
        p *= 2
    t_mat = _bmm(t_o, t_d)

    u = _bmm(t_mat, (v * beta).reshape(nc, c, HEAD_DIM))
    w = _bmm(t_mat, (kb * egc).reshape(nc, c, HEAD_DIM))

    lhs_s[:, 0:HEAD_DIM, :] = _bf(-_bmm(kdt, w))
    lhs_s[:, HEAD_DIM:HEAD_DIM + c, :] = _bf((q * egc).reshape(nc, c, HEAD_DIM) - _bmm(attn, w))
    n_s[...] = _bmm(kdt, u)
    op_s[...] = _bmm(attn, u)

    def chunk_step(n, state):
        r = _dot(lhs_s[n], _bf(state))
        o_s[pl.ds(pl.multiple_of(n * c, c), c), :] = r[HEAD_DIM:, :] + op_s[n]
        return state * egl_s[n][0:1, :] + r[0:HEAD_DIM, :] + n_s[n]

    lax.fori_loop(0, nc, chunk_step, jnp.zeros((HEAD_DIM, HEAD_DIM), F32))

    o = o_s[...]
    y = o * lax.rsqrt(jnp.mean(o * o, axis=-1, keepdims=True) + EPS) * ng_ref[...]
    z = z_ref[...]
    o_ref[...] = (y * (z * _sigmoid(z))).astype(o_ref.dtype)


def _deltanet(proj, dn_conv, a_log, dt_bias, norm_g):
    bsz, seq, _ = proj.shape
    nc = seq // DN_CHUNK
    col = lambda off: pl.BlockSpec((None, seq, LANES), lambda b, h: (b, 0, off + h))
    cw = lambda off: pl.BlockSpec((dn_conv.shape[0], LANES), lambda b, h: (0, off + h))
    smem = pl.BlockSpec(memory_space=pltpu.SMEM)
    return pl.pallas_call(
        _dn_kernel,
        grid=(bsz, DN_HEADS),
        in_specs=[smem, smem, col(CB_DN_Q), col(CB_DN_K), col(CB_DN_V), col(CB_DN_Z),
                  pl.BlockSpec((None, seq, LANES), lambda b, h: (b, 0, CB_SMALL)),
                  cw(CB_DN_Q), cw(CB_DN_K), cw(CB_DN_V),
                  pl.BlockSpec((1, HEAD_DIM), lambda b, h: (0, 0))],
        out_specs=pl.BlockSpec((None, seq, HEAD_DIM), lambda b, h: (b, 0, h)),
        out_shape=jax.ShapeDtypeStruct((bsz, seq, DN_HEADS * HEAD_DIM), BF16),
        scratch_shapes=[pltpu.VMEM((nc, HEAD_DIM + DN_CHUNK, HEAD_DIM), BF16),
                        pltpu.VMEM((nc, HEAD_DIM, HEAD_DIM), F32),
                        pltpu.VMEM((nc, DN_CHUNK, HEAD_DIM), F32),
                        pltpu.VMEM((nc, 8, LANES), F32),
                        pltpu.VMEM((seq, HEAD_DIM), F32)],
        compiler_params=_params("parallel", "parallel"),
        name="deltanet",
    )(a_log, dt_bias, proj, proj, proj, proj, proj, dn_conv, dn_conv, dn_conv, norm_g.reshape(1, HEAD_DIM))


def _dil_kernel(tab_ref, q_ref, k_ref, v_ref, o_ref, num_s, m_s, l_s):
    head = pl.program_id(1)
    seq = q_ref.shape[0]
    qb = Q_BLOCK
    scale = HEAD_DIM ** -0.5

    r = lax.broadcasted_iota(jnp.int32, (qb, 2 * qb), 0)
    cidx = lax.broadcasted_iota(jnp.int32, (qb, 2 * qb), 1)
    sub = qb + r - cidx

    def band_tile(window, dil):
        valid = (sub >= 0) & (sub <= window // dil)
        return jnp.where(valid, _bias_of_dist(jnp.maximum(sub, 0) * dil, tab_ref, head), NEG)

    def block(qv, kv, vv, tile):
        s = _dot_nt((qv * scale).astype(BF16), kv.astype(BF16)) + tile
        m = jnp.max(s, axis=-1, keepdims=True)
        p = jnp.exp(s - m)
        l = jnp.sum(p, axis=-1, keepdims=True)
        return _dot(p.astype(BF16), vv.astype(BF16)), m, l

    for pat, (window, dil) in enumerate(DIL_PATTERNS):
        assert window // dil == qb
        tile = band_tile(window, dil)
        sub_len = seq // dil
        for res in range(dil):
            for bi in range(sub_len // qb):
                def rows(first_blk, nblk):
                    start = res + dil * qb * first_blk
                    return pl.ds(start, nblk * qb, stride=dil) if dil > 1 else pl.ds(start, nblk * qb)
                qr = rows(bi, 1)
                if bi == 0:
                    kr, t = rows(0, 1), tile[:, qb:]
                else:
                    kr, t = rows(bi - 1, 2), tile
                num, m, l = block(q_ref[qr, :], k_ref[kr, :], v_ref[kr, :], t)
                if pat == 0:
                    num_s[qr, :] = num
                    m_s[qr, :] = jnp.broadcast_to(m, (qb, LANES))
                    l_s[qr, :] = jnp.broadcast_to(l, (qb, LANES))
                else:
                    m_old = m_s[qr, :]
                    m_new = jnp.maximum(m_old, m)
                    a_old = jnp.exp(m_old - m_new)
                    a_new = jnp.exp(m - m_new)
                    num_s[qr, :] = num_s[qr, :] * a_old + num * a_new
                    l_s[qr, :] = l_s[qr, :] * a_old + l * a_new
                    m_s[qr, :] = m_new

    o_ref[...] = (num_s[...] / l_s[...]).astype(o_ref.dtype)


def _dilated(proj, dil_tab):
    bsz, seq, _ = proj.shape
    col = lambda off: pl.BlockSpec((None, seq, LANES), lambda b, h: (b, 0, off + h))
    return pl.pallas_call(
        _dil_kernel,
        grid=(bsz, DIL_HEADS),
        in_specs=[pl.BlockSpec(memory_space=pltpu.SMEM), col(CB_DIL_Q), col(CB_DIL_K), col(CB_DIL_V)],
        out_specs=pl.BlockSpec((None, seq, HEAD_DIM), lambda b, h: (b, 0, h)),
        out_shape=jax.ShapeDtypeStruct((bsz, seq, DIL_HEADS * HEAD_DIM), BF16),
        scratch_shapes=[pltpu.VMEM((seq, HEAD_DIM), F32), pltpu.VMEM((seq, LANES), F32),
                        pltpu.VMEM((seq, LANES), F32)],
        compiler_params=_params("parallel", "parallel"),
        name="dilated",
    )(dil_tab, proj, proj, proj)


def _cmp_kernel(x_ref, pos_ref, w1_ref, w2_ref, o_ref):
    nblk = o_ref.shape[0]
    half = CMP_LEN // 2
    ha = jnp.zeros((nblk, CMP_HIDDEN), F32)
    hb = jnp.zeros((nblk, CMP_HIDDEN), F32)
    for l in range(half):
        xl = x_ref[pl.ds(l, nblk, stride=CMP_STRIDE), :]
        wa = w1_ref[l * HEAD_DIM:(l + 1) * HEAD_DIM, :]
        wb = w1_ref[(half + l) * HEAD_DIM:(half + l + 1) * HEAD_DIM, :]
        ha = ha + _dot((xl + pos_ref[l:l + 1, :]).astype(BF16), wa)
        hb = hb + _dot((xl + pos_ref[half + l:half + l + 1, :]).astype(BF16), wb)
    hmid = ha + pltpu.roll(hb, nblk - 1, axis=0)
    hmid = 0.5 * hmid * (1.0 + jnp.tanh(math.sqrt(2.0 / math.pi) * (hmid + 0.044715 * hmid * hmid * hmid)))
    out = _dot(hmid.astype(BF16), w2_ref[...])
    rowi = lax.broadcasted_iota(jnp.int32, out.shape, 0)
    o_ref[...] = jnp.where(rowi < nblk - 1, out, 0.0)


def _compress(proj, cmp_pos, w1, w2):
    bsz, seq, _ = proj.shape
    nblk = seq // CMP_STRIDE
    return pl.pallas_call(
        _cmp_kernel,
        grid=(2, bsz, NSA_GROUPS),
        in_specs=[pl.BlockSpec((None, seq, LANES), lambda i, b, g: (b, 0, CB_NSA_KV + i * NSA_GROUPS + g)),
                  pl.BlockSpec((None, CMP_LEN, HEAD_DIM), lambda i, b, g: (i, 0, 0)),
                  pl.BlockSpec((None, CMP_LEN * HEAD_DIM, CMP_HIDDEN), lambda i, b, g: (i, 0, 0)),
                  pl.BlockSpec((None, CMP_HIDDEN, HEAD_DIM), lambda i, b, g: (i, 0, 0))],
        out_specs=pl.BlockSpec((None, None, None, nblk, HEAD_DIM), lambda i, b, g: (b, g, i, 0, 0)),
        out_shape=jax.ShapeDtypeStruct((bsz, NSA_GROUPS, 2, nblk, HEAD_DIM), F32),
        compiler_params=_params("parallel", "parallel", "parallel"),
        name="nsa_compress",
    )(proj, cmp_pos, w1, w2)


def _nsa_kernel(tab_ref, q0_ref, q1_ref, q2_ref, ks_ref, vs_ref, kw_ref, vw_ref, cmp_ref, s_ref,
                o_ref, strip_s, cbias_s, wtile_s, q_s, kv_s, oslc_s):
    grp = pl.program_id(0)
    step = pl.program_id(2)
    seq = q0_ref.shape[0]
    qb = Q_BLOCK
    nsub = o_ref.shape[0] // qb
    nqb = seq // qb
    n_slc = seq // SLC_BLOCK
    ncmp = cmp_ref.shape[1]
    nwin = WIN // qb
    win_keys = WIN + qb
    scale = HEAD_DIM ** -0.5
    q_refs = (q0_ref, q1_ref, q2_ref)
    head_rows = lambda x, rr: x[rr * qb:(rr + 1) * qb, :]

    @pl.when(step == 0)
    def _():
        for rr in range(NSA_REP):
            q_s[rr] = (q_refs[rr][...] * scale).astype(BF16)
        for n, ref in enumerate((ks_ref, vs_ref, kw_ref, vw_ref)):
            kv_s[n] = ref[...].astype(BF16)

    @pl.when(jnp.logical_and(pl.program_id(1) == 0, step == 0))
    def _():
        for rr in range(NSA_REP):
            head = DIL_HEADS + grp * NSA_REP + rr
            shape = strip_s.shape[1:]
            dist = (lax.broadcasted_iota(jnp.int32, shape, 0) - lax.broadcasted_iota(jnp.int32, shape, 1)
                    + (seq - qb))
            strip_s[rr] = jnp.where(dist >= 0, _bias_of_dist(jnp.maximum(dist, 0), tab_ref, head), NEG)
            shape = cbias_s.shape[1:]
            dist = (lax.broadcasted_iota(jnp.int32, shape, 0)
                    - (lax.broadcasted_iota(jnp.int32, shape, 1) * CMP_STRIDE + CMP_LEN - 1))
            cbias_s[rr] = jnp.where(dist >= 0, _bias_of_dist(jnp.maximum(dist, 0), tab_ref, head), NEG)
            shape = wtile_s.shape[1:]
            dist = lax.broadcasted_iota(jnp.int32, shape, 0) + WIN - lax.broadcasted_iota(jnp.int32, shape, 1)
            wtile_s[rr] = jnp.where((dist >= 0) & (dist < WIN),
                                    _bias_of_dist(jnp.clip(dist, 0, WIN), tab_ref, head), NEG)

    lane = lax.broadcasted_iota(jnp.int32, (1, LANES), 1)
    oj = lax.broadcasted_iota(jnp.int32, (n_slc, ncmp), 0) * SLC_BLOCK
    oc = lax.broadcasted_iota(jnp.int32, (n_slc, ncmp), 1) * CMP_STRIDE
    overlap_t = (jnp.maximum(jnp.minimum(oc + CMP_LEN, oj + SLC_BLOCK) - jnp.maximum(oc, oj), 0)
                 .astype(F32) / CMP_STRIDE)
    jrow = lax.broadcasted_iota(jnp.int32, (n_slc, qb), 0)
    tcol = lax.broadcasted_iota(jnp.int32, (n_slc, qb), 1)

    def softmax_rows(s):
        m = jnp.max(s, axis=-1, keepdims=True)
        p = jnp.exp(s - m)
        return p, jnp.sum(p, axis=-1, keepdims=True)

    def front(i):
        rows = pl.ds(pl.multiple_of(i * qb, qb), qb)
        q3 = jnp.concatenate([q_s[rr, rows, :] for rr in range(NSA_REP)], axis=0)
        sm = s_ref[rows, :]

        def gate(rr, branch):
            ln = LANE_GATE + (grp * NSA_REP + rr) * 3 + branch
            return _sigmoid(jnp.sum(jnp.where(lane == ln, sm, 0.0), axis=-1, keepdims=True))

        s_all = _dot_nt(q3, _bf(cmp_ref[0]))
        p_sum = jnp.zeros((qb, ncmp), F32)
        probs = []
        for rr in range(NSA_REP):
            cb = cbias_s[rr, rows, :]
            p, l = softmax_rows(head_rows(s_all, rr) + cb)
            p = jnp.where(cb > 0.5 * NEG, p, 0.0)
            p = p / jnp.maximum(jnp.sum(p, axis=-1, keepdims=True), 1e-30)
            p_sum = p_sum + p
            probs.append(_bf(p))
        o_cmp = _dot(jnp.concatenate(probs, axis=0), _bf(cmp_ref[1]))
        acc = [gate(rr, 0) * head_rows(o_cmp, rr) for rr in range(NSA_REP)]

        imp = _dot_nt(overlap_t, p_sum, precision=HIGHEST)
        cur = jnp.right_shift(i * qb + tcol, SLC_SHIFT)
        forced = (jrow == 0) | ((jrow <= cur) & (jrow > cur - 2))
        imp = jnp.where(forced, jnp.inf, jnp.where(jrow <= cur, imp, -jnp.inf))
        cnt = jnp.zeros((n_slc, qb), F32)
        for j2 in range(n_slc):
            other = imp[j2:j2 + 1, :]
            ge = jnp.where(other >= imp, 1.0, 0.0)
            gt = jnp.where(other > imp, 1.0, 0.0)
            cnt = cnt + jnp.where(jrow > j2, ge, gt)
        sel_t = jnp.where((cnt < SLC_TOPK) & (jrow <= cur), 1.0, 0.0)
        sel_t = jnp.concatenate([sel_t, jnp.zeros((LANES - n_slc, qb), F32)], axis=0)
        sel = _bf(sel_t.T)

        j0 = jnp.maximum(i - nwin, 0)
        krows = pl.ds(pl.multiple_of(j0 * qb, qb), win_keys)
        tile_start = pl.multiple_of((j0 - i + nwin) * qb, qb)
        s_all = _dot_nt(q3, kv_s[2, krows, :])
        probs, dens = [], []
        for rr in range(NSA_REP):
            p, l = softmax_rows(head_rows(s_all, rr) + wtile_s[rr, :, pl.ds(tile_start, win_keys)])
            probs.append(_bf(p))
            dens.append(l)
        pv = _dot(jnp.concatenate(probs, axis=0), kv_s[3, krows, :])
        acc = [acc[rr] + gate(rr, 2) * (head_rows(pv, rr) / dens[rr]) for rr in range(NSA_REP)]
        return q3, sel, acc, [gate(rr, 1) for rr in range(NSA_REP)]

    def selected(u, i, q3, sel, nk):
        strip_start = pl.multiple_of((nqb - 1 - i) * qb, qb)
        s_all = _dot_nt(q3, kv_s[0, 0:nk, :])
        ej = lax.broadcasted_iota(jnp.int32, (LANES, nk), 0)
        ep = lax.broadcasted_iota(jnp.int32, (LANES, nk), 1)
        expand = jnp.where(ej == jnp.right_shift(ep, SLC_SHIFT), 1.0, 0.0).astype(BF16)
        keep = _dot(sel, expand) > 0.5
        probs, dens = [], []
        for rr in range(NSA_REP):
            s = head_rows(s_all, rr) + strip_s[rr, :, pl.ds(strip_start, nk)]
            p, l = softmax_rows(jnp.where(keep, s, NEG))
            probs.append(_bf(p))
            dens.append(l)
        pv = _dot(jnp.concatenate(probs, axis=0), kv_s[1, 0:nk, :])
        for rr in range(NSA_REP):
            oslc_s[u, rr] = head_rows(pv, rr) / dens[rr]

    blocks = [step * nsub + u for u in range(nsub)]
    fronts = [front(i) for i in blocks]
    blocks_per_tile = SLC_KEY_TILE // qb
    assert blocks_per_tile % nsub == 0
    for kq in range(seq // SLC_KEY_TILE):
        @pl.when(blocks[0] // blocks_per_tile == kq)
        def _(kq=kq):
            for u, i in enumerate(blocks):
                selected(u, i, fronts[u][0], fronts[u][1], (kq + 1) * SLC_KEY_TILE)

    for u in range(nsub):
        _, _, acc, gate_slc = fronts[u]
        for rr in range(NSA_REP):
            out = acc[rr] + gate_slc[rr] * oslc_s[u, rr]
            o_ref[u * qb:(u + 1) * qb, rr * HEAD_DIM:(rr + 1) * HEAD_DIM] = out.astype(o_ref.dtype)


def _nsa(proj, cmp_kv, rel_bias):
    bsz, seq, _ = proj.shape
    ncmp = cmp_kv.shape[3]
    nqb = seq // Q_BLOCK
    strip_w = (nqb - 1) * Q_BLOCK + SLC_KEY_TILE
    wtile_w = WIN + WIN + Q_BLOCK
    col = lambda fn: pl.BlockSpec((None, seq, LANES), lambda g, b, i: (b, 0, fn(g)))
    kv = lambda branch, which: col(lambda g: CB_NSA_KV + (branch * 2 + which) * NSA_GROUPS + g)
    return pl.pallas_call(
        _nsa_kernel,
        grid=(NSA_GROUPS, bsz, nqb // NSA_BLOCKS_PER_STEP),
        in_specs=[pl.BlockSpec(memory_space=pltpu.SMEM),
                  col(lambda g: CB_NSA_Q + g * NSA_REP), col(lambda g: CB_NSA_Q + g * NSA_REP + 1),
                  col(lambda g: CB_NSA_Q + g * NSA_REP + 2),
                  kv(1, 0), kv(1, 1), kv(2, 0), kv(2, 1),
                  pl.BlockSpec((None, None, 2, ncmp, HEAD_DIM), lambda g, b, i: (b, g, 0, 0, 0)),
                  pl.BlockSpec((None, seq, LANES), lambda g, b, i: (b, 0, CB_SMALL))],
        out_specs=pl.BlockSpec((None, NSA_BLOCKS_PER_STEP * Q_BLOCK, NSA_REP * HEAD_DIM),
                               lambda g, b, i: (b, i, g)),
        out_shape=jax.ShapeDtypeStruct((bsz, seq, NSA_GROUPS * NSA_REP * HEAD_DIM), BF16),
        scratch_shapes=[pltpu.VMEM((NSA_REP, Q_BLOCK, strip_w), F32),
                        pltpu.VMEM((NSA_REP, seq, ncmp), F32),
                        pltpu.VMEM((NSA_REP, Q_BLOCK, wtile_w), F32),
                        pltpu.VMEM((NSA_REP, seq, HEAD_DIM), BF16),
                        pltpu.VMEM((4, seq, HEAD_DIM), BF16),
                        pltpu.VMEM((NSA_BLOCKS_PER_STEP, NSA_REP, Q_BLOCK, HEAD_DIM), F32)],
        compiler_params=_params("arbitrary", "arbitrary", "arbitrary"),
        name="nsa_attention",
    )(rel_bias, proj, proj, proj, proj, proj, proj, proj, cmp_kv, proj)


IN_WIDE_A = 3072
IN_SMALL_A = 12
IN_WIDE_B = 3840
IN_SMALL_B = 18


def _w_in_prep_kernel(a_ref, b_ref, c_ref, o_ref):
    j = pl.program_id(0)
    depth = o_ref.shape[0]
    blk = o_ref.shape[1]
    n_a = IN_WIDE_A // blk
    n_b = IN_WIDE_B // blk
    s = IN_SMALL_A

    def put(lo, hi, src_ref, src_lo):
        for l in range(depth):
            o_ref[l, lo:hi, :] = src_ref[src_lo:src_lo + hi - lo, l, :].astype(o_ref.dtype)

    @pl.when(j < n_a)
    def _():
        put(0, blk, a_ref, 0)

    @pl.when(jnp.logical_and(j >= n_a, j < n_a + n_b))
    def _():
        put(0, blk - s, a_ref, s)
        put(blk - s, blk, b_ref, 0)

    @pl.when(j == n_a + n_b)
    def _():
        put(0, s, c_ref, 0)
        put(s, s + IN_SMALL_B, a_ref, s)
        o_ref[:, s + IN_SMALL_B:, :] = jnp.zeros((depth, blk - s - IN_SMALL_B, o_ref.shape[2]), o_ref.dtype)

    @pl.when(j > n_a + n_b)
    def _():
        o_ref[...] = jnp.zeros(o_ref.shape, o_ref.dtype)


def _prep_w_in(w):
    depth, d, cols = w.shape
    blk = LANES
    assert cols == IN_WIDE_A + IN_SMALL_A + IN_WIDE_B + IN_SMALL_B
    wt = jnp.transpose(w, (2, 0, 1))
    last = (cols - 1) // blk
    src = lambda off: pl.BlockSpec((blk, depth, d), lambda j: (jnp.minimum(j + off, last), 0, 0))
    return pl.pallas_call(
        _w_in_prep_kernel,
        grid=(PROJ_COLS // blk,),
        in_specs=[src(0), src(1), pl.BlockSpec((blk, depth, d), lambda j: (IN_WIDE_A // blk, 0, 0))],
        out_specs=pl.BlockSpec((depth, blk, d), lambda j: (0, j, 0)),
        out_shape=jax.ShapeDtypeStruct((depth, PROJ_COLS, d), BF16),
        compiler_params=_params("parallel"),
        name="w_in_prep",
    )(wt, wt, wt)


_DENSE_TILES = {
    "in_proj": (1024, 512),
    "out_proj": 512,
    "ffn_up": (1024, 512),
    "ffn_down": (512, 512),
    "final_norm": 512,
}


def kernel(x, norm1_g, w_in, dn_conv, dn_a_log, dn_dt_bias, dn_norm_g, cmp_pos, cmp_w1, cmp_w2, w_out,
           norm2_g, ffn_up, ffn_conv, ffn_down, rel_bias, final_g):
    bsz, seq, d = x.shape
    depth = w_in.shape[0]
    w_in_b = _prep_w_in(w_in)
    cmp_w1_b, cmp_w2_b = cmp_w1.astype(BF16), cmp_w2.astype(BF16)
    w_out_b, ffn_up_b, ffn_down_b = w_out.astype(BF16), ffn_up.astype(BF16), ffn_down.astype(BF16)
    xf = x.reshape(bsz * seq, d)
    t = _DENSE_TILES
    for l in range(depth):
        proj = _norm_matmul(xf, norm1_g[l], w_in_b, l, *t["in_proj"], F32).reshape(bsz, seq, PROJ_COLS)
        o_dn = _deltanet(proj, dn_conv[l], dn_a_log[l], dn_dt_bias[l], dn_norm_g[l])
        o_dil = _dilated(proj, rel_bias)
        cmp_kv = _compress(proj, cmp_pos[l], cmp_w1_b[l], cmp_w2_b[l])
        o_nsa = _nsa(proj, cmp_kv, rel_bias)
        xf = _out_proj(o_dn.reshape(bsz * seq, -1), o_dil.reshape(bsz * seq, -1), o_nsa.reshape(bsz * seq, -1),
                       w_out_b[l], xf, t["out_proj"])
        act = _ffn_up(xf, norm2_g[l], ffn_up_b, ffn_conv, l, seq, *t["ffn_up"])
        xf = _ffn_down(act, ffn_down_b, l, xf, *t["ffn_down"])
    return _rmsnorm(xf, final_g, t["final_norm"]).reshape(bsz, seq, d)
```

```python
import functools
import math

import numpy as np
import jax
import jax.numpy as jnp
from jax import lax
from jax.experimental import pallas as pl
from jax.experimental.pallas import tpu as pltpu

F32 = jnp.float32
BF16 = jnp.bfloat16
HIGHEST = lax.Precision.HIGHEST

LANES = 128
HEAD_DIM = 128
DN_HEADS = 6
DIL_HEADS = 4
NSA_GROUPS = 2
NSA_REP = 3
DN_CHUNK = 64
DN_SUB = 16
DN_HEADS_PER_STEP = 2
CMP_LEN = 32
CMP_STRIDE = 16
CMP_HIDDEN = 256
SLC_BLOCK = 64
SLC_SHIFT = 6
SLC_TOPK = 16
SLC_KEY_TILE = 512
NSA_BLOCKS_PER_STEP = 4
WIN = 512
Q_BLOCK = 128
FFN_SUB_COLS = 512
DIL_PATTERNS = ((128, 1), (512, 4), (2048, 16))
REL_BUCKETS = 32
REL_MAX_DIST = 2048
EPS = 1e-6
NEG = -1e30
VMEM_LIMIT = 56 * 1024 * 1024

CB_DN_Q, CB_DN_K, CB_DN_V, CB_DN_Z = 0, 6, 12, 18
CB_DIL_Q, CB_DIL_K, CB_DIL_V = 24, 28, 32
CB_NSA_Q = 36
CB_NSA_KV = 42
CB_SMALL = 54
PROJ_COLS = 7168
LANE_B, LANE_A, LANE_GATE = 0, 6, 12


def _bucket_thresholds():
    n = np.arange(0, REL_MAX_DIST + 1)
    max_exact = REL_BUCKETS // 2
    out = []
    for dt in (np.float32, np.float64):
        nf = np.maximum(n, 1).astype(dt)
        large = max_exact + (np.log(nf / dt(max_exact)) / dt(math.log(REL_MAX_DIST / max_exact))
                             * dt(REL_BUCKETS - max_exact)).astype(np.int32)
        large = np.minimum(large, REL_BUCKETS - 1)
        out.append(np.where(n < max_exact, n, large))
    assert np.array_equal(out[0], out[1])
    bucket = out[1]
    assert np.all(np.diff(bucket) >= 0)
    thr = [0]
    for b in range(1, REL_BUCKETS):
        idx = np.nonzero(bucket >= b)[0]
        thr.append(int(idx[0]) if idx.size else REL_MAX_DIST + 1)
    return tuple(thr)


_THR = _bucket_thresholds()


def _bias_of_dist(dist, tab_ref, head):
    val = jnp.full(dist.shape, tab_ref[head, 0], F32)
    for b in range(1, REL_BUCKETS):
        val = jnp.where(dist >= _THR[b], tab_ref[head, b], val)
    return val


def _sigmoid(x):
    return 1.0 / (1.0 + jnp.exp(-x))


def _bf(x):
    return x.astype(BF16)


def _dot(a, b, **kw):
    return jnp.dot(a, b, preferred_element_type=F32, **kw)


def _dot_nt(a, b, **kw):
    return lax.dot_general(a, b, (((1,), (1,)), ((), ())), preferred_element_type=F32, **kw)


def _bmm(a, b):
    return jnp.einsum('nij,njk->nik', _bf(a), _bf(b), preferred_element_type=F32)


def _bmm_nt(a, b):
    return jnp.einsum('nid,njd->nij', _bf(a), _bf(b), preferred_element_type=F32)


def _params(*sem):
    return pltpu.CompilerParams(dimension_semantics=sem, vmem_limit_bytes=VMEM_LIMIT)


def _norm_matmul_kernel(x_ref, g_ref, w_ref, o_ref, h_ref):
    @pl.when(pl.program_id(1) == 0)
    def _():
        x = x_ref[...]
        ms = jnp.mean(x * x, axis=-1, keepdims=True)
        h_ref[...] = (x * lax.rsqrt(ms + EPS) * g_ref[...]).astype(BF16)

    o_ref[...] = _dot_nt(h_ref[...], w_ref[...]).astype(o_ref.dtype)


def _norm_matmul(x, g, w_t, layer, tm, tn, out_dtype):
    m, d = x.shape
    n = w_t.shape[1]
    return pl.pallas_call(
        _norm_matmul_kernel,
        grid=(m // tm, n // tn),
        in_specs=[pl.BlockSpec((tm, d), lambda i, j: (i, 0)),
                  pl.BlockSpec((1, d), lambda i, j: (0, 0)),
                  pl.BlockSpec((None, tn, d), lambda i, j: (layer, j, 0))],
        out_specs=pl.BlockSpec((tm, tn), lambda i, j: (i, j)),
        out_shape=jax.ShapeDtypeStruct((m, n), out_dtype),
        scratch_shapes=[pltpu.VMEM((tm, d), BF16)],
        compiler_params=_params("parallel", "arbitrary"),
        name="in_proj",
    )(x, g.reshape(1, d), w_t)


def _out_proj_kernel(a_ref, b_ref, c_ref, wa_ref, wb_ref, wc_ref, x_ref, o_ref):
    acc = _dot(a_ref[...], wa_ref[...])
    acc = acc + _dot(b_ref[...], wb_ref[...])
    acc = acc + _dot(c_ref[...], wc_ref[...])
    o_ref[...] = x_ref[...] + acc


def _out_proj(o_dn, o_dil, o_nsa, w_out, x, tm):
    m, d = x.shape
    ka, kb, kc = o_dn.shape[1], o_dil.shape[1], o_nsa.shape[1]
    wa, wb, wc = w_out[:ka], w_out[ka:ka + kb], w_out[ka + kb:]
    row = lambda i: (i, 0)
    full = lambda i: (0, 0)
    return pl.pallas_call(
        _out_proj_kernel,
        grid=(m // tm,),
        in_specs=[pl.BlockSpec((tm, ka), row), pl.BlockSpec((tm, kb), row), pl.BlockSpec((tm, kc), row),
                  pl.BlockSpec((ka, d), full), pl.BlockSpec((kb, d), full), pl.BlockSpec((kc, d), full),
                  pl.BlockSpec((tm, d), row)],
        out_specs=pl.BlockSpec((tm, d), row),
        out_shape=jax.ShapeDtypeStruct((m, d), F32),
        compiler_params=_params("parallel"),
        name="out_proj",
    )(o_dn, o_dil, o_nsa, wa, wb, wc, x)


def _ffn_up_kernel(x_ref, xp_ref, g_ref, wg_ref, wv_ref, cg_ref, cv_ref, o_ref, h_ref, *, tiles_per_seq):
    tm = x_ref.shape[0]
    halo = xp_ref.shape[0]
    tn = o_ref.shape[1]

    @pl.when(pl.program_id(1) == 0)
    def _():
        def norm(x):
            ms = jnp.mean(x * x, axis=-1, keepdims=True)
            return (x * lax.rsqrt(ms + EPS) * g_ref[...]).astype(BF16)
        first = (pl.program_id(0) % tiles_per_seq) == 0
        hp = norm(xp_ref[...])
        h_ref[0:halo, :] = jnp.where(first, jnp.zeros_like(hp), hp)
        h_ref[halo:halo + tm, :] = norm(x_ref[...])

    h = h_ref[...]

    def conv(w_ref, c_ref, cols):
        u = _dot(h, w_ref[:, cols])
        y = u * c_ref[2:3, cols]
        y = y + pltpu.roll(u, 1, axis=0) * c_ref[1:2, cols]
        y = y + pltpu.roll(u, 2, axis=0) * c_ref[0:1, cols]
        return y[halo:, :]

    sub = min(tn, FFN_SUB_COLS)
    for s in range(tn // sub):
        cols = slice(s * sub, (s + 1) * sub)
        gate = conv(wg_ref, cg_ref, cols)
        val = conv(wv_ref, cv_ref, cols)
        o_ref[:, cols] = (gate * _sigmoid(gate) * val).astype(o_ref.dtype)


def _ffn_up(x, g, w_up, conv_w, layer, seq, tm, tn):
    m, d = x.shape
    f = w_up.shape[2] // 2
    taps = conv_w.shape[1]
    halo = 8
    nf = f // tn
    kern = functools.partial(_ffn_up_kernel, tiles_per_seq=seq // tm)
    return pl.pallas_call(
        kern,
        grid=(m // tm, nf),
        in_specs=[pl.BlockSpec((tm, d), lambda i, j: (i, 0)),
                  pl.BlockSpec((halo, d), lambda i, j: (jnp.maximum(i * (tm // halo) - 1, 0), 0)),
                  pl.BlockSpec((1, d), lambda i, j: (0, 0)),
                  pl.BlockSpec((None, d, tn), lambda i, j: (layer, 0, j)),
                  pl.BlockSpec((None, d, tn), lambda i, j: (layer, 0, j + nf)),
                  pl.BlockSpec((None, taps, tn), lambda i, j: (layer, 0, j)),
                  pl.BlockSpec((None, taps, tn), lambda i, j: (layer, 0, j + nf))],
        out_specs=pl.BlockSpec((tm, tn), lambda i, j: (i, j)),
        out_shape=jax.ShapeDtypeStruct((m, f), BF16),
        scratch_shapes=[pltpu.VMEM((tm + halo, d), BF16)],
        compiler_params=_params("parallel", "arbitrary"),
        name="ffn_up",
    )(x, x, g.reshape(1, d), w_up, w_up, conv_w, conv_w)


def _ffn_down_kernel(a_ref, w_ref, x_ref, o_ref):
    o_ref[...] = x_ref[...] + _dot(a_ref[...], w_ref[...])


def _ffn_down(act, w_down, layer, x, tm, tn):
    m, f = act.shape
    d = w_down.shape[2]
    return pl.pallas_call(
        _ffn_down_kernel,
        grid=(d // tn, m // tm),
        in_specs=[pl.BlockSpec((tm, f), lambda j, i: (i, 0)),
                  pl.BlockSpec((None, f, tn), lambda j, i: (layer, 0, j)),
                  pl.BlockSpec((tm, tn), lambda j, i: (i, j))],
        out_specs=pl.BlockSpec((tm, tn), lambda j, i: (i, j)),
        out_shape=jax.ShapeDtypeStruct((m, d), F32),
        compiler_params=_params("parallel", "parallel"),
        name="ffn_down",
    )(act, w_down, x)


def _rmsnorm_kernel(x_ref, g_ref, o_ref):
    x = x_ref[...]
    ms = jnp.mean(x * x, axis=-1, keepdims=True)
    o_ref[...] = x * lax.rsqrt(ms + EPS) * g_ref[...]


def _rmsnorm(x, g, tm):
    m, d = x.shape
    return pl.pallas_call(
        _rmsnorm_kernel,
        grid=(m // tm,),
        in_specs=[pl.BlockSpec((tm, d), lambda i: (i, 0)), pl.BlockSpec((1, d), lambda i: (0, 0))],
        out_specs=pl.BlockSpec((tm, d), lambda i: (i, 0)),
        out_shape=jax.ShapeDtypeStruct((m, d), F32),
        compiler_params=_params("parallel"),
        name="final_norm",
    )(x, g.reshape(1, d))


def _dn_prepare(h, ls, alog_ref, dtb_ref, q_ref, k_ref, v_ref, s_ref, cq_ref, ck_ref, cv_ref,
                lhs_s, n_s, op_s, egl_s):
    seq = q_ref.shape[0]
    c = DN_CHUNK
    nc = seq // c
    row = lax.broadcasted_iota(jnp.int32, (seq, 1), 0)

    def conv_silu(x_ref, c_ref):
        x = x_ref[:, ls]
        taps = c_ref.shape[0]
        acc = x * c_ref[taps - 1:taps, ls]
        for s in range(1, taps):
            xs = jnp.where(row >= s, pltpu.roll(x, s, axis=0), 0.0)
            acc = acc + xs * c_ref[taps - 1 - s:taps - s, ls]
        return acc * _sigmoid(acc)

    def l2norm(x):
        return x * lax.rsqrt(jnp.sum(x * x, axis=-1, keepdims=True) + EPS)

    q = l2norm(conv_silu(q_ref, cq_ref)) * (HEAD_DIM ** -0.5)
    k = l2norm(conv_silu(k_ref, ck_ref))
    v = conv_silu(v_ref, cv_ref)

    lane = lax.broadcasted_iota(jnp.int32, (1, LANES), 1)
    sm = s_ref[...]
    b_col = jnp.sum(jnp.where(lane == LANE_B + h, sm, 0.0), axis=-1, keepdims=True)
    a_col = jnp.sum(jnp.where(lane == LANE_A + h, sm, 0.0), axis=-1, keepdims=True)
    beta = _sigmoid(b_col)
    ap = a_col + dtb_ref[h]
    softplus = jnp.maximum(ap, 0.0) + jnp.log1p(jnp.exp(-jnp.abs(ap)))
    neg_rate = -jnp.exp(jnp.full((1, 1), alog_ref[h], F32))
    g = jnp.broadcast_to(neg_rate * softplus, (seq, LANES))

    pos = row & (c - 1)
    gc = g
    s = 1
    while s < c:
        gc = gc + jnp.where(pos >= s, pltpu.roll(gc, s, axis=0), 0.0)
        s *= 2

    egc = jnp.exp(gc)
    gc3 = gc.reshape(nc, c, LANES)
    glast3 = jnp.broadcast_to(gc3[:, c - 1:c, :], (nc, c, LANES))
    kb = k * beta
    k3 = k.reshape(nc, c, HEAD_DIM)
    kdt = jnp.swapaxes(k3 * jnp.exp(glast3 - gc3), 1, 2)
    egl_s[...] = jnp.exp(glast3[:, 0:8, :])

    ii = lax.broadcasted_iota(jnp.int32, (c, c), 0)
    jj = lax.broadcasted_iota(jnp.int32, (c, c), 1)
    gcol = gc3[:, :, 0:c]
    grow = jnp.swapaxes(gc3, 1, 2)[:, 0:c, :]
    decay = jnp.where(ii >= jj, jnp.exp(jnp.minimum(gcol - grow, 0.0)), 0.0)

    attn = _bmm_nt(q.reshape(nc, c, HEAD_DIM), k3) * decay
    a_mat = jnp.where(ii > jj, _bmm_nt(kb.reshape(nc, c, HEAD_DIM), k3) * decay, 0.0)

    same = (ii & -DN_SUB) == (jj & -DN_SUB)
    eye = jnp.where(ii == jj, 1.0, 0.0).astype(F32)
    x1 = jnp.where(same, -a_mat, 0.0)
    a_off = jnp.where(same, 0.0, a_mat)
    t_d = eye + x1
    xp = x1
    p = 2
    while p < DN_SUB:
        xp = _bmm(xp, xp)
        t_d = t_d + _bmm(t_d, xp)
        p *= 2
    y1 = -_bmm(t_d, a_off)
    t_o = eye + y1
    yp = y1
    p = 2
    while p < c // DN_SUB:
        yp = _bmm(yp, yp)
        t_o = t_o + _bmm(t_o, yp)
        p *= 2
    t_mat = _bmm(t_o, t_d)

    u = _bmm(t_mat, (v * beta).reshape(nc, c, HEAD_DIM))
    w = _bmm(t_mat, (kb * egc).reshape(nc, c, HEAD_DIM))

    lhs_s[:, 0:HEAD_DIM, :] = _bf(-_bmm(kdt, w))
    lhs_s[:, HEAD_DIM:HEAD_DIM + c, :] = _bf((q * egc).reshape(nc, c, HEAD_DIM) - _bmm(attn, w))
    n_s[...] = _bmm(kdt, u)
    op_s[...] = _bmm(attn, u)


def _dn_kernel(alog_ref, dtb_ref, q_ref, k_ref, v_ref, z_ref, s_ref, cq_ref, ck_ref, cv_ref, ng_ref,
               o_ref, lhs_s, n_s, op_s, egl_s, o_s):
    hs = DN_HEADS_PER_STEP
    seq = q_ref.shape[0]
    c = DN_CHUNK
    lanes = [slice(u * HEAD_DIM, (u + 1) * HEAD_DIM) for u in range(hs)]
    for u in range(hs):
        _dn_prepare(pl.program_id(1) * hs + u, lanes[u], alog_ref, dtb_ref, q_ref, k_ref, v_ref, s_ref,
                    cq_ref, ck_ref, cv_ref, lhs_s.at[u], n_s.at[u], op_s.at[u], egl_s.at[u])

    def chunk_step(n, states):
        rows = pl.ds(pl.multiple_of(n * c, c), c)
        out = []
        for u in range(hs):
            r = _dot(lhs_s[u, n], _bf(states[u]))
            o_s[u, rows, :] = r[HEAD_DIM:, :] + op_s[u, n]
            out.append(states[u] * egl_s[u, n][0:1, :] + r[0:HEAD_DIM, :] + n_s[u, n])
        return tuple(out)

    lax.fori_loop(0, seq // c, chunk_step, tuple(jnp.zeros((HEAD_DIM, HEAD_DIM), F32) for _ in range(hs)))

    for u in range(hs):
        o = o_s[u]
        y = o * lax.rsqrt(jnp.mean(o * o, axis=-1, keepdims=True) + EPS) * ng_ref[...]
        z = z_ref[:, lanes[u]]
        o_ref[:, lanes[u]] = (y * (z * _sigmoid(z))).astype(o_ref.dtype)


def _deltanet(proj, dn_conv, a_log, dt_bias, norm_g):
    bsz, seq, _ = proj.shape
    nc = seq // DN_CHUNK
    hs = DN_HEADS_PER_STEP
    width = hs * HEAD_DIM
    assert DN_HEADS % hs == 0 and all(cb % hs == 0 for cb in (CB_DN_Q, CB_DN_K, CB_DN_V, CB_DN_Z))
    col = lambda off: pl.BlockSpec((None, seq, width), lambda b, p: (b, 0, off // hs + p))
    cw = lambda off: pl.BlockSpec((dn_conv.shape[0], width), lambda b, p: (0, off // hs + p))
    smem = pl.BlockSpec(memory_space=pltpu.SMEM)
    return pl.pallas_call(
        _dn_kernel,
        grid=(bsz, DN_HEADS // hs),
        in_specs=[smem, smem, col(CB_DN_Q), col(CB_DN_K), col(CB_DN_V), col(CB_DN_Z),
                  pl.BlockSpec((None, seq, LANES), lambda b, p: (b, 0, CB_SMALL)),
                  cw(CB_DN_Q), cw(CB_DN_K), cw(CB_DN_V),
                  pl.BlockSpec((1, HEAD_DIM), lambda b, p: (0, 0))],
        out_specs=pl.BlockSpec((None, seq, width), lambda b, p: (b, 0, p)),
        out_shape=jax.ShapeDtypeStruct((bsz, seq, DN_HEADS * HEAD_DIM), BF16),
        scratch_shapes=[pltpu.VMEM((hs, nc, HEAD_DIM + DN_CHUNK, HEAD_DIM), BF16),
                        pltpu.VMEM((hs, nc, HEAD_DIM, HEAD_DIM), F32),
                        pltpu.VMEM((hs, nc, DN_CHUNK, HEAD_DIM), F32),
                        pltpu.VMEM((hs, nc, 8, LANES), F32),
                        pltpu.VMEM((hs, seq, HEAD_DIM), F32)],
        compiler_params=_params("parallel", "parallel"),
        name="deltanet",
    )(a_log, dt_bias, proj, proj, proj, proj, proj, dn_conv, dn_conv, dn_conv, norm_g.reshape(1, HEAD_DIM))


def _dil_kernel(tab_ref, q_ref, k_ref, v_ref, o_ref, num_s, m_s, l_s):
    head = pl.program_id(1)
    seq = q_ref.shape[0]
    qb = Q_BLOCK
    scale = HEAD_DIM ** -0.5

    r = lax.broadcasted_iota(jnp.int32, (qb, 2 * qb), 0)
    cidx = lax.broadcasted_iota(jnp.int32, (qb, 2 * qb), 1)
    sub = qb + r - cidx

    def band_tile(window, dil):
        valid = (sub >= 0) & (sub <= window // dil)
        return jnp.where(valid, _bias_of_dist(jnp.maximum(sub, 0) * dil, tab_ref, head), NEG)

    def block(qv, kv, vv, tile):
        s = _dot_nt((qv * scale).astype(BF16), kv.astype(BF16)) + tile
        m = jnp.max(s, axis=-1, keepdims=True)
        p = jnp.exp(s - m)
        l = jnp.sum(p, axis=-1, keepdims=True)
        return _dot(p.astype(BF16), vv.astype(BF16)), m, l

    for pat, (window, dil) in enumerate(DIL_PATTERNS):
        assert window // dil == qb
        tile = band_tile(window, dil)
        sub_len = seq // dil
        for res in range(dil):
            for bi in range(sub_len // qb):
                def rows(first_blk, nblk):
                    start = res + dil * qb * first_blk
                    return pl.ds(start, nblk * qb, stride=dil) if dil > 1 else pl.ds(start, nblk * qb)
                qr = rows(bi, 1)
                if bi == 0:
                    kr, t = rows(0, 1), tile[:, qb:]
                else:
                    kr, t = rows(bi - 1, 2), tile
                num, m, l = block(q_ref[qr, :], k_ref[kr, :], v_ref[kr, :], t)
                if pat == 0:
                    num_s[qr, :] = num
                    m_s[qr, :] = jnp.broadcast_to(m, (qb, LANES))
                    l_s[qr, :] = jnp.broadcast_to(l, (qb, LANES))
                else:
                    m_old = m_s[qr, :]
                    m_new = jnp.maximum(m_old, m)
                    a_old = jnp.exp(m_old - m_new)
                    a_new = jnp.exp(m - m_new)
                    num_s[qr, :] = num_s[qr, :] * a_old + num * a_new
                    l_s[qr, :] = l_s[qr, :] * a_old + l * a_new
                    m_s[qr, :] = m_new

    o_ref[...] = (num_s[...] / l_s[...]).astype(o_ref.dtype)


def _dilated(proj, dil_tab):
    bsz, seq, _ = proj.shape
    col = lambda off: pl.BlockSpec((None, seq, LANES), lambda b, h: (b, 0, off + h))
    return pl.pallas_call(
        _dil_kernel,
        grid=(bsz, DIL_HEADS),
        in_specs=[pl.BlockSpec(memory_space=pltpu.SMEM), col(CB_DIL_Q), col(CB_DIL_K), col(CB_DIL_V)],
        out_specs=pl.BlockSpec((None, seq, HEAD_DIM), lambda b, h: (b, 0, h)),
        out_shape=jax.ShapeDtypeStruct((bsz, seq, DIL_HEADS * HEAD_DIM), BF16),
        scratch_shapes=[pltpu.VMEM((seq, HEAD_DIM), F32), pltpu.VMEM((seq, LANES), F32),
                        pltpu.VMEM((seq, LANES), F32)],
        compiler_params=_params("parallel", "parallel"),
        name="dilated",
    )(dil_tab, proj, proj, proj)


def _cmp_kernel(x_ref, pos_ref, w1_ref, w2_ref, o_ref):
    nblk = o_ref.shape[0]
    half = CMP_LEN // 2
    ha = jnp.zeros((nblk, CMP_HIDDEN), F32)
    hb = jnp.zeros((nblk, CMP_HIDDEN), F32)
    for l in range(half):
        xl = x_ref[pl.ds(l, nblk, stride=CMP_STRIDE), :]
        wa = w1_ref[l * HEAD_DIM:(l + 1) * HEAD_DIM, :]
        wb = w1_ref[(half + l) * HEAD_DIM:(half + l + 1) * HEAD_DIM, :]
        ha = ha + _dot((xl + pos_ref[l:l + 1, :]).astype(BF16), wa)
        hb = hb + _dot((xl + pos_ref[half + l:half + l + 1, :]).astype(BF16), wb)
    hmid = ha + pltpu.roll(hb, nblk - 1, axis=0)
    hmid = 0.5 * hmid * (1.0 + jnp.tanh(math.sqrt(2.0 / math.pi) * (hmid + 0.044715 * hmid * hmid * hmid)))
    out = _dot(hmid.astype(BF16), w2_ref[...])
    rowi = lax.broadcasted_iota(jnp.int32, out.shape, 0)
    o_ref[...] = jnp.where(rowi < nblk - 1, out, 0.0)


def _compress(proj, cmp_pos, w1, w2):
    bsz, seq, _ = proj.shape
    nblk = seq // CMP_STRIDE
    return pl.pallas_call(
        _cmp_kernel,
        grid=(2, bsz, NSA_GROUPS),
        in_specs=[pl.BlockSpec((None, seq, LANES), lambda i, b, g: (b, 0, CB_NSA_KV + i * NSA_GROUPS + g)),
                  pl.BlockSpec((None, CMP_LEN, HEAD_DIM), lambda i, b, g: (i, 0, 0)),
                  pl.BlockSpec((None, CMP_LEN * HEAD_DIM, CMP_HIDDEN), lambda i, b, g: (i, 0, 0)),
                  pl.BlockSpec((None, CMP_HIDDEN, HEAD_DIM), lambda i, b, g: (i, 0, 0))],
        out_specs=pl.BlockSpec((None, None, None, nblk, HEAD_DIM), lambda i, b, g: (b, g, i, 0, 0)),
        out_shape=jax.ShapeDtypeStruct((bsz, NSA_GROUPS, 2, nblk, HEAD_DIM), F32),
        compiler_params=_params("parallel", "parallel", "parallel"),
        name="nsa_compress",
    )(proj, cmp_pos, w1, w2)


def _nsa_kernel(tab_ref, q0_ref, q1_ref, q2_ref, ks_ref, vs_ref, kw_ref, vw_ref, cmp_ref, s_ref,
                o_ref, strip_s, cbias_s, wtile_s, q_s, kv_s, oslc_s):
    grp = pl.program_id(0)
    step = pl.program_id(2)
    seq = q0_ref.shape[0]
    qb = Q_BLOCK
    nsub = o_ref.shape[0] // qb
    nqb = seq // qb
    n_slc = seq // SLC_BLOCK
    ncmp = cmp_ref.shape[1]
    nwin = WIN // qb
    win_keys = WIN + qb
    scale = HEAD_DIM ** -0.5
    q_refs = (q0_ref, q1_ref, q2_ref)
    head_rows = lambda x, rr: x[rr * qb:(rr + 1) * qb, :]

    @pl.when(step == 0)
    def _():
        for rr in range(NSA_REP):
            q_s[rr] = (q_refs[rr][...] * scale).astype(BF16)
        for n, ref in enumerate((ks_ref, vs_ref, kw_ref, vw_ref)):
            kv_s[n] = ref[...].astype(BF16)

    @pl.when(jnp.logical_and(pl.program_id(1) == 0, step == 0))
    def _():
        for rr in range(NSA_REP):
            head = DIL_HEADS + grp * NSA_REP + rr
            shape = strip_s.shape[1:]
            dist = (lax.broadcasted_iota(jnp.int32, shape, 0) - lax.broadcasted_iota(jnp.int32, shape, 1)
                    + (seq - qb))
            strip_s[rr] = jnp.where(dist >= 0, _bias_of_dist(jnp.maximum(dist, 0), tab_ref, head), NEG)
            shape = cbias_s.shape[1:]
            dist = (lax.broadcasted_iota(jnp.int32, shape, 0)
                    - (lax.broadcasted_iota(jnp.int32, shape, 1) * CMP_STRIDE + CMP_LEN - 1))
            cbias_s[rr] = jnp.where(dist >= 0, _bias_of_dist(jnp.maximum(dist, 0), tab_ref, head), NEG)
            shape = wtile_s.shape[1:]
            dist = lax.broadcasted_iota(jnp.int32, shape, 0) + WIN - lax.broadcasted_iota(jnp.int32, shape, 1)
            wtile_s[rr] = jnp.where((dist >= 0) & (dist < WIN),
                                    _bias_of_dist(jnp.clip(dist, 0, WIN), tab_ref, head), NEG)

    lane = lax.broadcasted_iota(jnp.int32, (1, LANES), 1)
    oj = lax.broadcasted_iota(jnp.int32, (n_slc, ncmp), 0) * SLC_BLOCK
    oc = lax.broadcasted_iota(jnp.int32, (n_slc, ncmp), 1) * CMP_STRIDE
    overlap_t = (jnp.maximum(jnp.minimum(oc + CMP_LEN, oj + SLC_BLOCK) - jnp.maximum(oc, oj), 0)
                 .astype(F32) / CMP_STRIDE)
    jrow = lax.broadcasted_iota(jnp.int32, (n_slc, qb), 0)
    tcol = lax.broadcasted_iota(jnp.int32, (n_slc, qb), 1)

    def softmax_rows(s):
        m = jnp.max(s, axis=-1, keepdims=True)
        p = jnp.exp(s - m)
        return p, jnp.sum(p, axis=-1, keepdims=True)

    def front(i):
        rows = pl.ds(pl.multiple_of(i * qb, qb), qb)
        q3 = jnp.concatenate([q_s[rr, rows, :] for rr in range(NSA_REP)], axis=0)
        sm = s_ref[rows, :]

        def gate(rr, branch):
            ln = LANE_GATE + (grp * NSA_REP + rr) * 3 + branch
            return _sigmoid(jnp.sum(jnp.where(lane == ln, sm, 0.0), axis=-1, keepdims=True))

        s_all = _dot_nt(q3, _bf(cmp_ref[0]))
        p_sum = jnp.zeros((qb, ncmp), F32)
        probs = []
        for rr in range(NSA_REP):
            cb = cbias_s[rr, rows, :]
            p, l = softmax_rows(head_rows(s_all, rr) + cb)
            p = jnp.where(cb > 0.5 * NEG, p, 0.0)
            p = p / jnp.maximum(jnp.sum(p, axis=-1, keepdims=True), 1e-30)
            p_sum = p_sum + p
            probs.append(_bf(p))
        o_cmp = _dot(jnp.concatenate(probs, axis=0), _bf(cmp_ref[1]))
        acc = [gate(rr, 0) * head_rows(o_cmp, rr) for rr in range(NSA_REP)]

        imp = _dot_nt(overlap_t, p_sum, precision=HIGHEST)
        cur = jnp.right_shift(i * qb + tcol, SLC_SHIFT)
        forced = (jrow == 0) | ((jrow <= cur) & (jrow > cur - 2))
        imp = jnp.where(forced, jnp.inf, jnp.where(jrow <= cur, imp, -jnp.inf))
        cnt = jnp.zeros((n_slc, qb), F32)
        for j2 in range(n_slc):
            other = imp[j2:j2 + 1, :]
            ge = jnp.where(other >= imp, 1.0, 0.0)
            gt = jnp.where(other > imp, 1.0, 0.0)
            cnt = cnt + jnp.where(jrow > j2, ge, gt)
        sel_t = jnp.where((cnt < SLC_TOPK) & (jrow <= cur), 1.0, 0.0)
        sel_t = jnp.concatenate([sel_t, jnp.zeros((LANES - n_slc, qb), F32)], axis=0)
        sel = _bf(sel_t.T)

        j0 = jnp.maximum(i - nwin, 0)
        krows = pl.ds(pl.multiple_of(j0 * qb, qb), win_keys)
        tile_start = pl.multiple_of((j0 - i + nwin) * qb, qb)
        s_all = _dot_nt(q3, kv_s[2, krows, :])
        probs, dens = [], []
        for rr in range(NSA_REP):
            p, l = softmax_rows(head_rows(s_all, rr) + wtile_s[rr, :, pl.ds(tile_start, win_keys)])
            probs.append(_bf(p))
            dens.append(l)
        pv = _dot(jnp.concatenate(probs, axis=0), kv_s[3, krows, :])
        acc = [acc[rr] + gate(rr, 2) * (head_rows(pv, rr) / dens[rr]) for rr in range(NSA_REP)]
        return q3, sel, acc, [gate(rr, 1) for rr in range(NSA_REP)]

    def selected(u, i, q3, sel, nk):
        strip_start = pl.multiple_of((nqb - 1 - i) * qb, qb)
        s_all = _dot_nt(q3, kv_s[0, 0:nk, :])
        ej = lax.broadcasted_iota(jnp.int32, (LANES, nk), 0)
        ep = lax.broadcasted_iota(jnp.int32, (LANES, nk), 1)
        expand = jnp.where(ej == jnp.right_shift(ep, SLC_SHIFT), 1.0, 0.0).astype(BF16)
        keep = _dot(sel, expand) > 0.5
        probs, dens = [], []
        for rr in range(NSA_REP):
            s = head_rows(s_all, rr) + strip_s[rr, :, pl.ds(strip_start, nk)]
            p, l = softmax_rows(jnp.where(keep, s, NEG))
            probs.append(_bf(p))
            dens.append(l)
        pv = _dot(jnp.concatenate(probs, axis=0), kv_s[1, 0:nk, :])
        for rr in range(NSA_REP):
            oslc_s[u, rr] = head_rows(pv, rr) / dens[rr]

    blocks = [step * nsub + u for u in range(nsub)]
    fronts = [front(i) for i in blocks]
    blocks_per_tile = SLC_KEY_TILE // qb
    assert blocks_per_tile % nsub == 0
    for kq in range(seq // SLC_KEY_TILE):
        @pl.when(blocks[0] // blocks_per_tile == kq)
        def _(kq=kq):
            for u, i in enumerate(blocks):
                selected(u, i, fronts[u][0], fronts[u][1], (kq + 1) * SLC_KEY_TILE)

    for u in range(nsub):
        _, _, acc, gate_slc = fronts[u]
        for rr in range(NSA_REP):
            out = acc[rr] + gate_slc[rr] * oslc_s[u, rr]
            o_ref[u * qb:(u + 1) * qb, rr * HEAD_DIM:(rr + 1) * HEAD_DIM] = out.astype(o_ref.dtype)


def _nsa(proj, cmp_kv, rel_bias):
    bsz, seq, _ = proj.shape
    ncmp = cmp_kv.shape[3]
    nqb = seq // Q_BLOCK
    strip_w = (nqb - 1) * Q_BLOCK + SLC_KEY_TILE
    wtile_w = WIN + WIN + Q_BLOCK
    col = lambda fn: pl.BlockSpec((None, seq, LANES), lambda g, b, i: (b, 0, fn(g)))
    kv = lambda branch, which: col(lambda g: CB_NSA_KV + (branch * 2 + which) * NSA_GROUPS + g)
    return pl.pallas_call(
        _nsa_kernel,
        grid=(NSA_GROUPS, bsz, nqb // NSA_BLOCKS_PER_STEP),
        in_specs=[pl.BlockSpec(memory_space=pltpu.SMEM),
                  col(lambda g: CB_NSA_Q + g * NSA_REP), col(lambda g: CB_NSA_Q + g * NSA_REP + 1),
                  col(lambda g: CB_NSA_Q + g * NSA_REP + 2),
                  kv(1, 0), kv(1, 1), kv(2, 0), kv(2, 1),
                  pl.BlockSpec((None, None, 2, ncmp, HEAD_DIM), lambda g, b, i: (b, g, 0, 0, 0)),
                  pl.BlockSpec((None, seq, LANES), lambda g, b, i: (b, 0, CB_SMALL))],
        out_specs=pl.BlockSpec((None, NSA_BLOCKS_PER_STEP * Q_BLOCK, NSA_REP * HEAD_DIM),
                               lambda g, b, i: (b, i, g)),
        out_shape=jax.ShapeDtypeStruct((bsz, seq, NSA_GROUPS * NSA_REP * HEAD_DIM), BF16),
        scratch_shapes=[pltpu.VMEM((NSA_REP, Q_BLOCK, strip_w), F32),
                        pltpu.VMEM((NSA_REP, seq, ncmp), F32),
                        pltpu.VMEM((NSA_REP, Q_BLOCK, wtile_w), F32),
                        pltpu.VMEM((NSA_REP, seq, HEAD_DIM), BF16),
                        pltpu.VMEM((4, seq, HEAD_DIM), BF16),
                        pltpu.VMEM((NSA_BLOCKS_PER_STEP, NSA_REP, Q_BLOCK, HEAD_DIM), F32)],
        compiler_params=_params("arbitrary", "arbitrary", "arbitrary"),
        name="nsa_attention",
    )(rel_bias, proj, proj, proj, proj, proj, proj, proj, cmp_kv, proj)


IN_WIDE_A = 3072
IN_SMALL_A = 12
IN_WIDE_B = 3840
IN_SMALL_B = 18


def _w_in_prep_kernel(a_ref, b_ref, c_ref, o_ref):
    j = pl.program_id(0)
    depth = o_ref.shape[0]
    blk = o_ref.shape[1]
    n_a = IN_WIDE_A // blk
    n_b = IN_WIDE_B // blk
    s = IN_SMALL_A

    def put(lo, hi, src_ref, src_lo):
        for l in range(depth):
            o_ref[l, lo:hi, :] = src_ref[src_lo:src_lo + hi - lo, l, :].astype(o_ref.dtype)

    @pl.when(j < n_a)
    def _():
        put(0, blk, a_ref, 0)

    @pl.when(jnp.logical_and(j >= n_a, j < n_a + n_b))
    def _():
        put(0, blk - s, a_ref, s)
        put(blk - s, blk, b_ref, 0)

    @pl.when(j == n_a + n_b)
    def _():
        put(0, s, c_ref, 0)
        put(s, s + IN_SMALL_B, a_ref, s)
        o_ref[:, s + IN_SMALL_B:, :] = jnp.zeros((depth, blk - s - IN_SMALL_B, o_ref.shape[2]), o_ref.dtype)

    @pl.when(j > n_a + n_b)
    def _():
        o_ref[...] = jnp.zeros(o_ref.shape, o_ref.dtype)


def _prep_w_in(w):
    depth, d, cols = w.shape
    blk = LANES
    assert cols == IN_WIDE_A + IN_SMALL_A + IN_WIDE_B + IN_SMALL_B
    wt = jnp.transpose(w, (2, 0, 1))
    last = (cols - 1) // blk
    nxt = 16
    assert IN_SMALL_A <= nxt and blk % nxt == 0
    last_nxt = (cols - 1) // nxt
    return pl.pallas_call(
        _w_in_prep_kernel,
        grid=(PROJ_COLS // blk,),
        in_specs=[pl.BlockSpec((blk, depth, d), lambda j: (jnp.minimum(j, last), 0, 0)),
                  pl.BlockSpec((nxt, depth, d), lambda j: (jnp.minimum((j + 1) * (blk // nxt), last_nxt), 0, 0)),
                  pl.BlockSpec((nxt, depth, d), lambda j: (IN_WIDE_A // nxt, 0, 0))],
        out_specs=pl.BlockSpec((depth, blk, d), lambda j: (0, j, 0)),
        out_shape=jax.ShapeDtypeStruct((depth, PROJ_COLS, d), BF16),
        compiler_params=_params("parallel"),
        name="w_in_prep",
    )(wt, wt, wt)


_DENSE_TILES = {
    "in_proj": (1024, 512),
    "out_proj": 512,
    "ffn_up": (1024, 512),
    "ffn_down": (512, 512),
    "final_norm": 512,
}


def kernel(x, norm1_g, w_in, dn_conv, dn_a_log, dn_dt_bias, dn_norm_g, cmp_pos, cmp_w1, cmp_w2, w_out,
           norm2_g, ffn_up, ffn_conv, ffn_down, rel_bias, final_g):
    bsz, seq, d = x.shape
    depth = w_in.shape[0]
    w_in_b = _prep_w_in(w_in)
    cmp_w1_b, cmp_w2_b = cmp_w1.astype(BF16), cmp_w2.astype(BF16)
    w_out_b, ffn_up_b, ffn_down_b = w_out.astype(BF16), ffn_up.astype(BF16), ffn_down.astype(BF16)
    xf = x.reshape(bsz * seq, d)
    t = _DENSE_TILES
    for l in range(depth):
        proj = _norm_matmul(xf, norm1_g[l], w_in_b, l, *t["in_proj"], F32).reshape(bsz, seq, PROJ_COLS)
        o_dn = _deltanet(proj, dn_conv[l], dn_a_log[l], dn_dt_bias[l], dn_norm_g[l])
        o_dil = _dilated(proj, rel_bias)
        cmp_kv = _compress(proj, cmp_pos[l], cmp_w1_b[l], cmp_w2_b[l])
        o_nsa = _nsa(proj, cmp_kv, rel_bias)
        xf = _out_proj(o_dn.reshape(bsz * seq, -1), o_dil.reshape(bsz * seq, -1), o_nsa.reshape(bsz * seq, -1),
                       w_out_b[l], xf, t["out_proj"])
        act = _ffn_up(xf, norm2_g[l], ffn_up_b, ffn_conv, l, seq, *t["ffn_up"])
        xf = _ffn_down(act, ffn_down_b, l, xf, *t["ffn_down"])
    return _rmsnorm(xf, final_g, t["final_norm"]).reshape(bsz, seq, d)
```

```python
import functools
import math

import numpy as np
import jax
import jax.numpy as jnp
from jax import lax
from jax.experimental import pallas as pl
from jax.experimental.pallas import tpu as pltpu

F32 = jnp.float32
BF16 = jnp.bfloat16
HIGHEST = lax.Precision.HIGHEST

LANES = 128
HEAD_DIM = 128
DN_HEADS = 6
DIL_HEADS = 4
NSA_GROUPS = 2
NSA_REP = 3
DN_CHUNK = 64
DN_SUB = 16
DN_HEADS_PER_STEP = 2
CMP_LEN = 32
CMP_STRIDE = 16
CMP_HIDDEN = 256
SLC_BLOCK = 64
SLC_SHIFT = 6
SLC_TOPK = 16
SLC_KEY_TILE = 512
NSA_BLOCKS_PER_STEP = 4
WIN = 512
Q_BLOCK = 128
FFN_SUB_COLS = 512
DIL_PATTERNS = ((128, 1), (512, 4), (2048, 16))
REL_BUCKETS = 32
REL_MAX_DIST = 2048
EPS = 1e-6
NEG = -1e30
VMEM_LIMIT = 56 * 1024 * 1024

CB_DN_Q, CB_DN_K, CB_DN_V, CB_DN_Z = 0, 6, 12, 18
CB_DIL_Q, CB_DIL_K, CB_DIL_V = 24, 28, 32
CB_NSA_Q = 36
CB_NSA_KV = 42
CB_SMALL = 54
PROJ_COLS = 7168
LANE_B, LANE_A, LANE_GATE = 0, 6, 12


def _bucket_thresholds():
    n = np.arange(0, REL_MAX_DIST + 1)
    max_exact = REL_BUCKETS // 2
    out = []
    for dt in (np.float32, np.float64):
        nf = np.maximum(n, 1).astype(dt)
        large = max_exact + (np.log(nf / dt(max_exact)) / dt(math.log(REL_MAX_DIST / max_exact))
                             * dt(REL_BUCKETS - max_exact)).astype(np.int32)
        large = np.minimum(large, REL_BUCKETS - 1)
        out.append(np.where(n < max_exact, n, large))
    assert np.array_equal(out[0], out[1])
    bucket = out[1]
    assert np.all(np.diff(bucket) >= 0)
    thr = [0]
    for b in range(1, REL_BUCKETS):
        idx = np.nonzero(bucket >= b)[0]
        thr.append(int(idx[0]) if idx.size else REL_MAX_DIST + 1)
    return tuple(thr)


_THR = _bucket_thresholds()


def _bias_of_dist(dist, tab_ref, head):
    val = jnp.full(dist.shape, tab_ref[head, 0], F32)
    for b in range(1, REL_BUCKETS):
        val = jnp.where(dist >= _THR[b], tab_ref[head, b], val)
    return val


def _sigmoid(x):
    return 1.0 / (1.0 + jnp.exp(-x))


def _bf(x):
    return x.astype(BF16)


def _dot(a, b, **kw):
    return jnp.dot(a, b, preferred_element_type=F32, **kw)


def _dot_nt(a, b, **kw):
    return lax.dot_general(a, b, (((1,), (1,)), ((), ())), preferred_element_type=F32, **kw)


def _bmm(a, b):
    return jnp.einsum('nij,njk->nik', _bf(a), _bf(b), preferred_element_type=F32)


def _bmm_nt(a, b):
    return jnp.einsum('nid,njd->nij', _bf(a), _bf(b), preferred_element_type=F32)


def _params(*sem):
    return pltpu.CompilerParams(dimension_semantics=sem, vmem_limit_bytes=VMEM_LIMIT)


def _norm_matmul_kernel(x_ref, g_ref, w_ref, o_ref, h_ref):
    @pl.when(pl.program_id(1) == 0)
    def _():
        x = x_ref[...]
        ms = jnp.mean(x * x, axis=-1, keepdims=True)
        h_ref[...] = (x * lax.rsqrt(ms + EPS) * g_ref[...]).astype(BF16)

    o_ref[...] = _dot_nt(h_ref[...], w_ref[...]).astype(o_ref.dtype)


def _norm_matmul(x, g, w_t, layer, tm, tn, out_dtype):
    m, d = x.shape
    n = w_t.shape[1]
    return pl.pallas_call(
        _norm_matmul_kernel,
        grid=(m // tm, n // tn),
        in_specs=[pl.BlockSpec((tm, d), lambda i, j: (i, 0)),
                  pl.BlockSpec((1, d), lambda i, j: (0, 0)),
                  pl.BlockSpec((None, tn, d), lambda i, j: (layer, j, 0))],
        out_specs=pl.BlockSpec((tm, tn), lambda i, j: (i, j)),
        out_shape=jax.ShapeDtypeStruct((m, n), out_dtype),
        scratch_shapes=[pltpu.VMEM((tm, d), BF16)],
        compiler_params=_params("parallel", "arbitrary"),
        name="in_proj",
    )(x, g.reshape(1, d), w_t)


def _out_proj_kernel(a_ref, b_ref, c_ref, wa_ref, wb_ref, wc_ref, x_ref, o_ref):
    acc = _dot(a_ref[...], wa_ref[...])
    acc = acc + _dot(b_ref[...], wb_ref[...])
    acc = acc + _dot(c_ref[...], wc_ref[...])
    o_ref[...] = x_ref[...] + acc


def _out_proj(o_dn, o_dil, o_nsa, w_out, x, tm):
    m, d = x.shape
    ka, kb, kc = o_dn.shape[1], o_dil.shape[1], o_nsa.shape[1]
    wa, wb, wc = w_out[:ka], w_out[ka:ka + kb], w_out[ka + kb:]
    row = lambda i: (i, 0)
    full = lambda i: (0, 0)
    return pl.pallas_call(
        _out_proj_kernel,
        grid=(m // tm,),
        in_specs=[pl.BlockSpec((tm, ka), row), pl.BlockSpec((tm, kb), row), pl.BlockSpec((tm, kc), row),
                  pl.BlockSpec((ka, d), full), pl.BlockSpec((kb, d), full), pl.BlockSpec((kc, d), full),
                  pl.BlockSpec((tm, d), row)],
        out_specs=pl.BlockSpec((tm, d), row),
        out_shape=jax.ShapeDtypeStruct((m, d), F32),
        compiler_params=_params("parallel"),
        name="out_proj",
    )(o_dn, o_dil, o_nsa, wa, wb, wc, x)


def _ffn_up_kernel(x_ref, xp_ref, g_ref, wg_ref, wv_ref, cg_ref, cv_ref, o_ref, h_ref, *, tiles_per_seq):
    tm = x_ref.shape[0]
    halo = xp_ref.shape[0]
    tn = o_ref.shape[1]

    @pl.when(pl.program_id(1) == 0)
    def _():
        def norm(x):
            ms = jnp.mean(x * x, axis=-1, keepdims=True)
            return (x * lax.rsqrt(ms + EPS) * g_ref[...]).astype(BF16)
        first = (pl.program_id(0) % tiles_per_seq) == 0
        hp = norm(xp_ref[...])
        h_ref[0:halo, :] = jnp.where(first, jnp.zeros_like(hp), hp)
        h_ref[halo:halo + tm, :] = norm(x_ref[...])

    h = h_ref[...]

    def conv(w_ref, c_ref, cols):
        u = _dot(h, w_ref[:, cols])
        y = u * c_ref[2:3, cols]
        y = y + pltpu.roll(u, 1, axis=0) * c_ref[1:2, cols]
        y = y + pltpu.roll(u, 2, axis=0) * c_ref[0:1, cols]
        return y[halo:, :]

    sub = min(tn, FFN_SUB_COLS)
    for s in range(tn // sub):
        cols = slice(s * sub, (s + 1) * sub)
        gate = conv(wg_ref, cg_ref, cols)
        val = conv(wv_ref, cv_ref, cols)
        o_ref[:, cols] = (gate * _sigmoid(gate) * val).astype(o_ref.dtype)


def _ffn_up(x, g, w_up, conv_w, layer, seq, tm, tn):
    m, d = x.shape
    f = w_up.shape[2] // 2
    taps = conv_w.shape[1]
    halo = 8
    nf = f // tn
    kern = functools.partial(_ffn_up_kernel, tiles_per_seq=seq // tm)
    return pl.pallas_call(
        kern,
        grid=(m // tm, nf),
        in_specs=[pl.BlockSpec((tm, d), lambda i, j: (i, 0)),
                  pl.BlockSpec((halo, d), lambda i, j: (jnp.maximum(i * (tm // halo) - 1, 0), 0)),
                  pl.BlockSpec((1, d), lambda i, j: (0, 0)),
                  pl.BlockSpec((None, d, tn), lambda i, j: (layer, 0, j)),
                  pl.BlockSpec((None, d, tn), lambda i, j: (layer, 0, j + nf)),
                  pl.BlockSpec((None, taps, tn), lambda i, j: (layer, 0, j)),
                  pl.BlockSpec((None, taps, tn), lambda i, j: (layer, 0, j + nf))],
        out_specs=pl.BlockSpec((tm, tn), lambda i, j: (i, j)),
        out_shape=jax.ShapeDtypeStruct((m, f), BF16),
        scratch_shapes=[pltpu.VMEM((tm + halo, d), BF16)],
        compiler_params=_params("parallel", "arbitrary"),
        name="ffn_up",
    )(x, x, g.reshape(1, d), w_up, w_up, conv_w, conv_w)


def _ffn_down_kernel(a_ref, w_ref, x_ref, o_ref):
    o_ref[...] = x_ref[...] + _dot(a_ref[...], w_ref[...])


def _ffn_down(act, w_down, layer, x, tm, tn):
    m, f = act.shape
    d = w_down.shape[2]
    return pl.pallas_call(
        _ffn_down_kernel,
        grid=(d // tn, m // tm),
        in_specs=[pl.BlockSpec((tm, f), lambda j, i: (i, 0)),
                  pl.BlockSpec((None, f, tn), lambda j, i: (layer, 0, j)),
                  pl.BlockSpec((tm, tn), lambda j, i: (i, j))],
        out_specs=pl.BlockSpec((tm, tn), lambda j, i: (i, j)),
        out_shape=jax.ShapeDtypeStruct((m, d), F32),
        compiler_params=_params("parallel", "parallel"),
        name="ffn_down",
    )(act, w_down, x)


def _rmsnorm_kernel(x_ref, g_ref, o_ref):
    x = x_ref[...]
    ms = jnp.mean(x * x, axis=-1, keepdims=True)
    o_ref[...] = x * lax.rsqrt(ms + EPS) * g_ref[...]


def _rmsnorm(x, g, tm):
    m, d = x.shape
    return pl.pallas_call(
        _rmsnorm_kernel,
        grid=(m // tm,),
        in_specs=[pl.BlockSpec((tm, d), lambda i: (i, 0)), pl.BlockSpec((1, d), lambda i: (0, 0))],
        out_specs=pl.BlockSpec((tm, d), lambda i: (i, 0)),
        out_shape=jax.ShapeDtypeStruct((m, d), F32),
        compiler_params=_params("parallel"),
        name="final_norm",
    )(x, g.reshape(1, d))


def _dn_prepare(h, ls, alog_ref, dtb_ref, q_ref, k_ref, v_ref, s_ref, cq_ref, ck_ref, cv_ref,
                lhs_s, n_s, op_s, egl_s):
    seq = q_ref.shape[0]
    c = DN_CHUNK
    nc = seq // c
    row = lax.broadcasted_iota(jnp.int32, (seq, 1), 0)

    def conv_silu(x_ref, c_ref):
        x = x_ref[:, ls]
        taps = c_ref.shape[0]
        acc = x * c_ref[taps - 1:taps, ls]
        for s in range(1, taps):
            xs = jnp.where(row >= s, pltpu.roll(x, s, axis=0), 0.0)
            acc = acc + xs * c_ref[taps - 1 - s:taps - s, ls]
        return acc * _sigmoid(acc)

    def l2norm(x):
        return x * lax.rsqrt(jnp.sum(x * x, axis=-1, keepdims=True) + EPS)

    q = l2norm(conv_silu(q_ref, cq_ref)) * (HEAD_DIM ** -0.5)
    k = l2norm(conv_silu(k_ref, ck_ref))
    v = conv_silu(v_ref, cv_ref)

    lane = lax.broadcasted_iota(jnp.int32, (1, LANES), 1)
    sm = s_ref[...]
    b_col = jnp.sum(jnp.where(lane == LANE_B + h, sm, 0.0), axis=-1, keepdims=True)
    a_col = jnp.sum(jnp.where(lane == LANE_A + h, sm, 0.0), axis=-1, keepdims=True)
    beta = _sigmoid(b_col)
    ap = a_col + dtb_ref[h]
    softplus = jnp.maximum(ap, 0.0) + jnp.log1p(jnp.exp(-jnp.abs(ap)))
    neg_rate = -jnp.exp(jnp.full((1, 1), alog_ref[h], F32))
    g = jnp.broadcast_to(neg_rate * softplus, (seq, LANES))

    pos = row & (c - 1)
    gc = g
    s = 1
    while s < c:
        gc = gc + jnp.where(pos >= s, pltpu.roll(gc, s, axis=0), 0.0)
        s *= 2

    egc = jnp.exp(gc)
    gc3 = gc.reshape(nc, c, LANES)
    glast3 = jnp.broadcast_to(gc3[:, c - 1:c, :], (nc, c, LANES))
    kb = k * beta
    k3 = k.reshape(nc, c, HEAD_DIM)
    kdt = jnp.swapaxes(k3 * jnp.exp(glast3 - gc3), 1, 2)
    egl_s[...] = jnp.exp(glast3[:, 0:8, :])

    ii = lax.broadcasted_iota(jnp.int32, (c, c), 0)
    jj = lax.broadcasted_iota(jnp.int32, (c, c), 1)
    gcol = gc3[:, :, 0:c]
    grow = jnp.swapaxes(gc3, 1, 2)[:, 0:c, :]
    decay = jnp.where(ii >= jj, jnp.exp(jnp.minimum(gcol - grow, 0.0)), 0.0)

    attn = _bmm_nt(q.reshape(nc, c, HEAD_DIM), k3) * decay
    a_mat = jnp.where(ii > jj, _bmm_nt(kb.reshape(nc, c, HEAD_DIM), k3) * decay, 0.0)

    same = (ii & -DN_SUB) == (jj & -DN_SUB)
    eye = jnp.where(ii == jj, 1.0, 0.0).astype(F32)
    x1 = jnp.where(same, -a_mat, 0.0)
    a_off = jnp.where(same, 0.0, a_mat)
    t_d = eye + x1
    xp = x1
    p = 2
    while p < DN_SUB:
        xp = _bmm(xp, xp)
        t_d = t_d + _bmm(t_d, xp)
        p *= 2
    y1 = -_bmm(t_d, a_off)
    t_o = eye + y1
    yp = y1
    p = 2
    while p < c // DN_SUB:
        yp = _bmm(yp, yp)
        t_o = t_o + _bmm(t_o, yp)
        p *= 2
    t_mat = _bmm(t_o, t_d)

    u = _bmm(t_mat, (v * beta).reshape(nc, c, HEAD_DIM))
    w = _bmm(t_mat, (kb * egc).reshape(nc, c, HEAD_DIM))

    lhs_s[:, 0:HEAD_DIM, :] = _bf(-_bmm(kdt, w))
    lhs_s[:, HEAD_DIM:HEAD_DIM + c, :] = _bf((q * egc).reshape(nc, c, HEAD_DIM) - _bmm(attn, w))
    n_s[...] = _bmm(kdt, u)
    op_s[...] = _bmm(attn, u)


def _dn_kernel(alog_ref, dtb_ref, q_ref, k_ref, v_ref, z_ref, s_ref, cq_ref, ck_ref, cv_ref, ng_ref,
               o_ref, lhs_s, n_s, op_s, egl_s, o_s):
    hs = DN_HEADS_PER_STEP
    seq = q_ref.shape[0]
    c = DN_CHUNK
    lanes = [slice(u * HEAD_DIM, (u + 1) * HEAD_DIM) for u in range(hs)]
    for u in range(hs):
        _dn_prepare(pl.program_id(1) * hs + u, lanes[u], alog_ref, dtb_ref, q_ref, k_ref, v_ref, s_ref,
                    cq_ref, ck_ref, cv_ref, lhs_s.at[u], n_s.at[u], op_s.at[u], egl_s.at[u])

    def chunk_step(n, states):
        rows = pl.ds(pl.multiple_of(n * c, c), c)
        out = []
        for u in range(hs):
            r = _dot(lhs_s[u, n], _bf(states[u]))
            o_s[u, rows, :] = r[HEAD_DIM:, :] + op_s[u, n]
            out.append(states[u] * egl_s[u, n][0:1, :] + r[0:HEAD_DIM, :] + n_s[u, n])
        return tuple(out)

    lax.fori_loop(0, seq // c, chunk_step, tuple(jnp.zeros((HEAD_DIM, HEAD_DIM), F32) for _ in range(hs)))

    for u in range(hs):
        o = o_s[u]
        y = o * lax.rsqrt(jnp.mean(o * o, axis=-1, keepdims=True) + EPS) * ng_ref[...]
        z = z_ref[:, lanes[u]]
        o_ref[:, lanes[u]] = (y * (z * _sigmoid(z))).astype(o_ref.dtype)


def _deltanet(proj, dn_conv, a_log, dt_bias, norm_g):
    bsz, seq, _ = proj.shape
    nc = seq // DN_CHUNK
    hs = DN_HEADS_PER_STEP
    width = hs * HEAD_DIM
    assert DN_HEADS % hs == 0 and all(cb % hs == 0 for cb in (CB_DN_Q, CB_DN_K, CB_DN_V, CB_DN_Z))
    col = lambda off: pl.BlockSpec((None, seq, width), lambda b, p: (b, 0, off // hs + p))
    cw = lambda off: pl.BlockSpec((dn_conv.shape[0], width), lambda b, p: (0, off // hs + p))
    smem = pl.BlockSpec(memory_space=pltpu.SMEM)
    return pl.pallas_call(
        _dn_kernel,
        grid=(bsz, DN_HEADS // hs),
        in_specs=[smem, smem, col(CB_DN_Q), col(CB_DN_K), col(CB_DN_V), col(CB_DN_Z),
                  pl.BlockSpec((None, seq, LANES), lambda b, p: (b, 0, CB_SMALL)),
                  cw(CB_DN_Q), cw(CB_DN_K), cw(CB_DN_V),
                  pl.BlockSpec((1, HEAD_DIM), lambda b, p: (0, 0))],
        out_specs=pl.BlockSpec((None, seq, width), lambda b, p: (b, 0, p)),
        out_shape=jax.ShapeDtypeStruct((bsz, seq, DN_HEADS * HEAD_DIM), BF16),
        scratch_shapes=[pltpu.VMEM((hs, nc, HEAD_DIM + DN_CHUNK, HEAD_DIM), BF16),
                        pltpu.VMEM((hs, nc, HEAD_DIM, HEAD_DIM), F32),
                        pltpu.VMEM((hs, nc, DN_CHUNK, HEAD_DIM), F32),
                        pltpu.VMEM((hs, nc, 8, LANES), F32),
                        pltpu.VMEM((hs, seq, HEAD_DIM), F32)],
        compiler_params=_params("parallel", "parallel"),
        name="deltanet",
    )(a_log, dt_bias, proj, proj, proj, proj, proj, dn_conv, dn_conv, dn_conv, norm_g.reshape(1, HEAD_DIM))


def _dil_kernel(tab_ref, q_ref, k_ref, v_ref, o_ref, num_s, m_s, l_s, tile_s):
    head = pl.program_id(0)
    seq = q_ref.shape[0]
    qb = Q_BLOCK
    scale = HEAD_DIM ** -0.5

    @pl.when(pl.program_id(1) == 0)
    def _():
        r = lax.broadcasted_iota(jnp.int32, (qb, 2 * qb), 0)
        cidx = lax.broadcasted_iota(jnp.int32, (qb, 2 * qb), 1)
        sub = qb + r - cidx
        for pat, (window, dil) in enumerate(DIL_PATTERNS):
            valid = (sub >= 0) & (sub <= window // dil)
            tile_s[pat] = jnp.where(valid, _bias_of_dist(jnp.maximum(sub, 0) * dil, tab_ref, head), NEG)

    def block(qv, kv, vv, tile):
        s = _dot_nt((qv * scale).astype(BF16), kv.astype(BF16)) + tile
        m = jnp.max(s, axis=-1, keepdims=True)
        p = jnp.exp(s - m)
        l = jnp.sum(p, axis=-1, keepdims=True)
        return _dot(p.astype(BF16), vv.astype(BF16)), m, l

    for pat, (window, dil) in enumerate(DIL_PATTERNS):
        assert window // dil == qb
        tile = tile_s[pat]
        sub_len = seq // dil
        for res in range(dil):
            for bi in range(sub_len // qb):
                def rows(first_blk, nblk):
                    start = res + dil * qb * first_blk
                    return pl.ds(start, nblk * qb, stride=dil) if dil > 1 else pl.ds(start, nblk * qb)
                qr = rows(bi, 1)
                if bi == 0:
                    kr, t = rows(0, 1), tile[:, qb:]
                else:
                    kr, t = rows(bi - 1, 2), tile
                num, m, l = block(q_ref[qr, :], k_ref[kr, :], v_ref[kr, :], t)
                if pat == 0:
                    num_s[qr, :] = num
                    m_s[qr, :] = jnp.broadcast_to(m, (qb, LANES))
                    l_s[qr, :] = jnp.broadcast_to(l, (qb, LANES))
                else:
                    m_old = m_s[qr, :]
                    m_new = jnp.maximum(m_old, m)
                    a_old = jnp.exp(m_old - m_new)
                    a_new = jnp.exp(m - m_new)
                    num_s[qr, :] = num_s[qr, :] * a_old + num * a_new
                    l_s[qr, :] = l_s[qr, :] * a_old + l * a_new
                    m_s[qr, :] = m_new

    o_ref[...] = (num_s[...] / l_s[...]).astype(o_ref.dtype)


def _dilated(proj, dil_tab):
    bsz, seq, _ = proj.shape
    col = lambda off: pl.BlockSpec((None, seq, LANES), lambda h, b: (b, 0, off + h))
    return pl.pallas_call(
        _dil_kernel,
        grid=(DIL_HEADS, bsz),
        in_specs=[pl.BlockSpec(memory_space=pltpu.SMEM), col(CB_DIL_Q), col(CB_DIL_K), col(CB_DIL_V)],
        out_specs=pl.BlockSpec((None, seq, HEAD_DIM), lambda h, b: (b, 0, h)),
        out_shape=jax.ShapeDtypeStruct((bsz, seq, DIL_HEADS * HEAD_DIM), BF16),
        scratch_shapes=[pltpu.VMEM((seq, HEAD_DIM), F32), pltpu.VMEM((seq, LANES), F32),
                        pltpu.VMEM((seq, LANES), F32),
                        pltpu.VMEM((len(DIL_PATTERNS), Q_BLOCK, 2 * Q_BLOCK), F32)],
        compiler_params=_params("arbitrary", "arbitrary"),
        name="dilated",
    )(dil_tab, proj, proj, proj)


def _cmp_kernel(x_ref, pos_ref, w1_ref, w2_ref, o_ref):
    nblk = o_ref.shape[0]
    half = CMP_LEN // 2
    ha = jnp.zeros((nblk, CMP_HIDDEN), F32)
    hb = jnp.zeros((nblk, CMP_HIDDEN), F32)
    for l in range(half):
        xl = x_ref[pl.ds(l, nblk, stride=CMP_STRIDE), :]
        wa = w1_ref[l * HEAD_DIM:(l + 1) * HEAD_DIM, :]
        wb = w1_ref[(half + l) * HEAD_DIM:(half + l + 1) * HEAD_DIM, :]
        ha = ha + _dot((xl + pos_ref[l:l + 1, :]).astype(BF16), wa)
        hb = hb + _dot((xl + pos_ref[half + l:half + l + 1, :]).astype(BF16), wb)
    hmid = ha + pltpu.roll(hb, nblk - 1, axis=0)
    hmid = 0.5 * hmid * (1.0 + jnp.tanh(math.sqrt(2.0 / math.pi) * (hmid + 0.044715 * hmid * hmid * hmid)))
    out = _dot(hmid.astype(BF16), w2_ref[...])
    rowi = lax.broadcasted_iota(jnp.int32, out.shape, 0)
    o_ref[...] = jnp.where(rowi < nblk - 1, out, 0.0)


def _compress(proj, cmp_pos, w1, w2):
    bsz, seq, _ = proj.shape
    nblk = seq // CMP_STRIDE
    return pl.pallas_call(
        _cmp_kernel,
        grid=(2, bsz, NSA_GROUPS),
        in_specs=[pl.BlockSpec((None, seq, LANES), lambda i, b, g: (b, 0, CB_NSA_KV + i * NSA_GROUPS + g)),
                  pl.BlockSpec((None, CMP_LEN, HEAD_DIM), lambda i, b, g: (i, 0, 0)),
                  pl.BlockSpec((None, CMP_LEN * HEAD_DIM, CMP_HIDDEN), lambda i, b, g: (i, 0, 0)),
                  pl.BlockSpec((None, CMP_HIDDEN, HEAD_DIM), lambda i, b, g: (i, 0, 0))],
        out_specs=pl.BlockSpec((None, None, None, nblk, HEAD_DIM), lambda i, b, g: (b, g, i, 0, 0)),
        out_shape=jax.ShapeDtypeStruct((bsz, NSA_GROUPS, 2, nblk, HEAD_DIM), F32),
        compiler_params=_params("parallel", "parallel", "parallel"),
        name="nsa_compress",
    )(proj, cmp_pos, w1, w2)


def _nsa_kernel(tab_ref, q0_ref, q1_ref, q2_ref, ks_ref, vs_ref, kw_ref, vw_ref, cmp_ref, s_ref,
                o_ref, strip_s, cbias_s, wtile_s, q_s, kv_s, oslc_s):
    grp = pl.program_id(0)
    step = pl.program_id(2)
    seq = q0_ref.shape[0]
    qb = Q_BLOCK
    nsub = o_ref.shape[0] // qb
    nqb = seq // qb
    n_slc = seq // SLC_BLOCK
    ncmp = cmp_ref.shape[1]
    nwin = WIN // qb
    win_keys = WIN + qb
    scale = HEAD_DIM ** -0.5
    q_refs = (q0_ref, q1_ref, q2_ref)
    head_rows = lambda x, rr: x[rr * qb:(rr + 1) * qb, :]

    @pl.when(step == 0)
    def _():
        for rr in range(NSA_REP):
            q_s[rr] = (q_refs[rr][...] * scale).astype(BF16)
        for n, ref in enumerate((ks_ref, vs_ref, kw_ref, vw_ref)):
            kv_s[n] = ref[...].astype(BF16)

    @pl.when(jnp.logical_and(pl.program_id(1) == 0, step == 0))
    def _():
        for rr in range(NSA_REP):
            head = DIL_HEADS + grp * NSA_REP + rr
            shape = strip_s.shape[1:]
            dist = (lax.broadcasted_iota(jnp.int32, shape, 0) - lax.broadcasted_iota(jnp.int32, shape, 1)
                    + (seq - qb))
            strip_s[rr] = jnp.where(dist >= 0, _bias_of_dist(jnp.maximum(dist, 0), tab_ref, head), NEG)
            shape = cbias_s.shape[1:]
            dist = (lax.broadcasted_iota(jnp.int32, shape, 0)
                    - (lax.broadcasted_iota(jnp.int32, shape, 1) * CMP_STRIDE + CMP_LEN - 1))
            cbias_s[rr] = jnp.where(dist >= 0, _bias_of_dist(jnp.maximum(dist, 0), tab_ref, head), NEG)
            shape = wtile_s.shape[1:]
            dist = lax.broadcasted_iota(jnp.int32, shape, 0) + WIN - lax.broadcasted_iota(jnp.int32, shape, 1)
            wtile_s[rr] = jnp.where((dist >= 0) & (dist < WIN),
                                    _bias_of_dist(jnp.clip(dist, 0, WIN), tab_ref, head), NEG)

    lane = lax.broadcasted_iota(jnp.int32, (1, LANES), 1)
    oj = lax.broadcasted_iota(jnp.int32, (n_slc, ncmp), 0) * SLC_BLOCK
    oc = lax.broadcasted_iota(jnp.int32, (n_slc, ncmp), 1) * CMP_STRIDE
    overlap_t = (jnp.maximum(jnp.minimum(oc + CMP_LEN, oj + SLC_BLOCK) - jnp.maximum(oc, oj), 0)
                 .astype(F32) / CMP_STRIDE)
    jrow = lax.broadcasted_iota(jnp.int32, (n_slc, qb), 0)
    tcol = lax.broadcasted_iota(jnp.int32, (n_slc, qb), 1)

    def softmax_rows(s):
        m = jnp.max(s, axis=-1, keepdims=True)
        p = jnp.exp(s - m)
        return p, jnp.sum(p, axis=-1, keepdims=True)

    def front(i):
        rows = pl.ds(pl.multiple_of(i * qb, qb), qb)
        q3 = jnp.concatenate([q_s[rr, rows, :] for rr in range(NSA_REP)], axis=0)
        sm = _sigmoid(s_ref[rows, :])

        def gate(rr, branch):
            ln = LANE_GATE + (grp * NSA_REP + rr) * 3 + branch
            return jnp.sum(jnp.where(lane == ln, sm, 0.0), axis=-1, keepdims=True)

        s_all = _dot_nt(q3, _bf(cmp_ref[0]))
        p_sum = jnp.zeros((qb, ncmp), F32)
        probs = []
        for rr in range(NSA_REP):
            cb = cbias_s[rr, rows, :]
            p, l = softmax_rows(head_rows(s_all, rr) + cb)
            p = jnp.where(cb > 0.5 * NEG, p, 0.0)
            p = p / jnp.maximum(jnp.sum(p, axis=-1, keepdims=True), 1e-30)
            p_sum = p_sum + p
            probs.append(_bf(p))
        o_cmp = _dot(jnp.concatenate(probs, axis=0), _bf(cmp_ref[1]))
        acc = [gate(rr, 0) * head_rows(o_cmp, rr) for rr in range(NSA_REP)]

        imp = _dot_nt(overlap_t, p_sum, precision=HIGHEST)
        cur = jnp.right_shift(i * qb + tcol, SLC_SHIFT)
        forced = (jrow == 0) | ((jrow <= cur) & (jrow > cur - 2))
        imp = jnp.where(forced, jnp.inf, jnp.where(jrow <= cur, imp, -jnp.inf))
        cnt = jnp.zeros((n_slc, qb), F32)
        for j2 in range(n_slc):
            other = imp[j2:j2 + 1, :]
            ge = jnp.where(other >= imp, 1.0, 0.0)
            gt = jnp.where(other > imp, 1.0, 0.0)
            cnt = cnt + jnp.where(jrow > j2, ge, gt)
        sel_t = jnp.where((cnt < SLC_TOPK) & (jrow <= cur), 1.0, 0.0)
        sel_t = jnp.concatenate([sel_t, jnp.zeros((LANES - n_slc, qb), F32)], axis=0)
        sel = _bf(sel_t.T)

        j0 = jnp.maximum(i - nwin, 0)
        krows = pl.ds(pl.multiple_of(j0 * qb, qb), win_keys)
        tile_start = pl.multiple_of((j0 - i + nwin) * qb, qb)
        s_all = _dot_nt(q3, kv_s[2, krows, :])
        probs, dens = [], []
        for rr in range(NSA_REP):
            p, l = softmax_rows(head_rows(s_all, rr) + wtile_s[rr, :, pl.ds(tile_start, win_keys)])
            probs.append(_bf(p))
            dens.append(l)
        pv = _dot(jnp.concatenate(probs, axis=0), kv_s[3, krows, :])
        acc = [acc[rr] + gate(rr, 2) * (head_rows(pv, rr) / dens[rr]) for rr in range(NSA_REP)]
        return q3, sel, acc, [gate(rr, 1) for rr in range(NSA_REP)]

    def selected(u, i, q3, sel, nk):
        strip_start = pl.multiple_of((nqb - 1 - i) * qb, qb)
        s_all = _dot_nt(q3, kv_s[0, 0:nk, :])
        ej = lax.broadcasted_iota(jnp.int32, (LANES, nk), 0)
        ep = lax.broadcasted_iota(jnp.int32, (LANES, nk), 1)
        expand = jnp.where(ej == jnp.right_shift(ep, SLC_SHIFT), 1.0, 0.0).astype(BF16)
        keep = _dot(sel, expand) > 0.5
        probs, dens = [], []
        for rr in range(NSA_REP):
            s = head_rows(s_all, rr) + strip_s[rr, :, pl.ds(strip_start, nk)]
            p, l = softmax_rows(jnp.where(keep, s, NEG))
            probs.append(_bf(p))
            dens.append(l)
        pv = _dot(jnp.concatenate(probs, axis=0), kv_s[1, 0:nk, :])
        for rr in range(NSA_REP):
            oslc_s[u, rr] = head_rows(pv, rr) / dens[rr]

    blocks = [step * nsub + u for u in range(nsub)]
    fronts = [front(i) for i in blocks]
    blocks_per_tile = SLC_KEY_TILE // qb
    assert blocks_per_tile % nsub == 0
    for kq in range(seq // SLC_KEY_TILE):
        @pl.when(blocks[0] // blocks_per_tile == kq)
        def _(kq=kq):
            for u, i in enumerate(blocks):
                selected(u, i, fronts[u][0], fronts[u][1], (kq + 1) * SLC_KEY_TILE)

    for u in range(nsub):
        _, _, acc, gate_slc = fronts[u]
        for rr in range(NSA_REP):
            out = acc[rr] + gate_slc[rr] * oslc_s[u, rr]
            o_ref[u * qb:(u + 1) * qb, rr * HEAD_DIM:(rr + 1) * HEAD_DIM] = out.astype(o_ref.dtype)


def _nsa(proj, cmp_kv, rel_bias):
    bsz, seq, _ = proj.shape
    ncmp = cmp_kv.shape[3]
    nqb = seq // Q_BLOCK
    strip_w = (nqb - 1) * Q_BLOCK + SLC_KEY_TILE
    wtile_w = WIN + WIN + Q_BLOCK
    col = lambda fn: pl.BlockSpec((None, seq, LANES), lambda g, b, i: (b, 0, fn(g)))
    kv = lambda branch, which: col(lambda g: CB_NSA_KV + (branch * 2 + which) * NSA_GROUPS + g)
    return pl.pallas_call(
        _nsa_kernel,
        grid=(NSA_GROUPS, bsz, nqb // NSA_BLOCKS_PER_STEP),
        in_specs=[pl.BlockSpec(memory_space=pltpu.SMEM),
                  col(lambda g: CB_NSA_Q + g * NSA_REP), col(lambda g: CB_NSA_Q + g * NSA_REP + 1),
                  col(lambda g: CB_NSA_Q + g * NSA_REP + 2),
                  kv(1, 0), kv(1, 1), kv(2, 0), kv(2, 1),
                  pl.BlockSpec((None, None, 2, ncmp, HEAD_DIM), lambda g, b, i: (b, g, 0, 0, 0)),
                  pl.BlockSpec((None, seq, LANES), lambda g, b, i: (b, 0, CB_SMALL))],
        out_specs=pl.BlockSpec((None, NSA_BLOCKS_PER_STEP * Q_BLOCK, NSA_REP * HEAD_DIM),
                               lambda g, b, i: (b, i, g)),
        out_shape=jax.ShapeDtypeStruct((bsz, seq, NSA_GROUPS * NSA_REP * HEAD_DIM), BF16),
        scratch_shapes=[pltpu.VMEM((NSA_REP, Q_BLOCK, strip_w), F32),
                        pltpu.VMEM((NSA_REP, seq, ncmp), F32),
                        pltpu.VMEM((NSA_REP, Q_BLOCK, wtile_w), F32),
                        pltpu.VMEM((NSA_REP, seq, HEAD_DIM), BF16),
                        pltpu.VMEM((4, seq, HEAD_DIM), BF16),
                        pltpu.VMEM((NSA_BLOCKS_PER_STEP, NSA_REP, Q_BLOCK, HEAD_DIM), F32)],
        compiler_params=_params("arbitrary", "arbitrary", "arbitrary"),
        name="nsa_attention",
    )(rel_bias, proj, proj, proj, proj, proj, proj, proj, cmp_kv, proj)


IN_WIDE_A = 3072
IN_SMALL_A = 12
IN_WIDE_B = 3840
IN_SMALL_B = 18


def _w_in_prep_kernel(a_ref, b_ref, c_ref, o_ref):
    j = pl.program_id(0)
    depth = o_ref.shape[0]
    blk = o_ref.shape[1]
    n_a = IN_WIDE_A // blk
    n_b = IN_WIDE_B // blk
    s = IN_SMALL_A

    def put(lo, hi, src_ref, src_lo):
        for l in range(depth):
            o_ref[l, lo:hi, :] = src_ref[src_lo:src_lo + hi - lo, l, :].astype(o_ref.dtype)

    @pl.when(j < n_a)
    def _():
        put(0, blk, a_ref, 0)

    @pl.when(jnp.logical_and(j >= n_a, j < n_a + n_b))
    def _():
        put(0, blk - s, a_ref, s)
        put(blk - s, blk, b_ref, 0)

    @pl.when(j == n_a + n_b)
    def _():
        put(0, s, c_ref, 0)
        put(s, s + IN_SMALL_B, a_ref, s)
        o_ref[:, s + IN_SMALL_B:, :] = jnp.zeros((depth, blk - s - IN_SMALL_B, o_ref.shape[2]), o_ref.dtype)

    @pl.when(j > n_a + n_b)
    def _():
        o_ref[...] = jnp.zeros(o_ref.shape, o_ref.dtype)


def _prep_w_in(w):
    depth, d, cols = w.shape
    blk = LANES
    assert cols == IN_WIDE_A + IN_SMALL_A + IN_WIDE_B + IN_SMALL_B
    wt = jnp.transpose(w, (2, 0, 1))
    last = (cols - 1) // blk
    nxt = 16
    assert IN_SMALL_A <= nxt and blk % nxt == 0
    last_nxt = (cols - 1) // nxt
    return pl.pallas_call(
        _w_in_prep_kernel,
        grid=(PROJ_COLS // blk,),
        in_specs=[pl.BlockSpec((blk, depth, d), lambda j: (jnp.minimum(j, last), 0, 0)),
                  pl.BlockSpec((nxt, depth, d), lambda j: (jnp.minimum((j + 1) * (blk // nxt), last_nxt), 0, 0)),
                  pl.BlockSpec((nxt, depth, d), lambda j: (IN_WIDE_A // nxt, 0, 0))],
        out_specs=pl.BlockSpec((depth, blk, d), lambda j: (0, j, 0)),
        out_shape=jax.ShapeDtypeStruct((depth, PROJ_COLS, d), BF16),
        compiler_params=_params("parallel"),
        name="w_in_prep",
    )(wt, wt, wt)


_DENSE_TILES = {
    "in_proj": (1024, 1024),
    "out_proj": 512,
    "ffn_up": (1024, 512),
    "ffn_down": (1024, 512),
    "final_norm": 512,
}


def kernel(x, norm1_g, w_in, dn_conv, dn_a_log, dn_dt_bias, dn_norm_g, cmp_pos, cmp_w1, cmp_w2, w_out,
           norm2_g, ffn_up, ffn_conv, ffn_down, rel_bias, final_g):
    bsz, seq, d = x.shape
    depth = w_in.shape[0]
    w_in_b = _prep_w_in(w_in)
    cmp_w1_b, cmp_w2_b = cmp_w1.astype(BF16), cmp_w2.astype(BF16)
    w_out_b, ffn_up_b, ffn_down_b = w_out.astype(BF16), ffn_up.astype(BF16), ffn_down.astype(BF16)
    xf = x.reshape(bsz * seq, d)
    t = _DENSE_TILES
    for l in range(depth):
        proj = _norm_matmul(xf, norm1_g[l], w_in_b, l, *t["in_proj"], F32).reshape(bsz, seq, PROJ_COLS)
        o_dn = _deltanet(proj, dn_conv[l], dn_a_log[l], dn_dt_bias[l], dn_norm_g[l])
        o_dil = _dilated(proj, rel_bias)
        cmp_kv = _compress(proj, cmp_pos[l], cmp_w1_b[l], cmp_w2_b[l])
        o_nsa = _nsa(proj, cmp_kv, rel_bias)
        xf = _out_proj(o_dn.reshape(bsz * seq, -1), o_dil.reshape(bsz * seq, -1), o_nsa.reshape(bsz * seq, -1),
                       w_out_b[l], xf, t["out_proj"])
        act = _ffn_up(xf, norm2_g[l], ffn_up_b, ffn_conv, l, seq, *t["ffn_up"])
        xf = _ffn_down(act, ffn_down_b, l, xf, *t["ffn_down"])
    return _rmsnorm(xf, final_g, t["final_norm"]).reshape(bsz, seq, d)
```

```python
import functools
import math

import numpy as np
import jax
import jax.numpy as jnp
from jax import lax
from jax.experimental import pallas as pl
from jax.experimental.pallas import tpu as pltpu

F32 = jnp.float32
BF16 = jnp.bfloat16
HIGHEST = lax.Precision.HIGHEST

LANES = 128
HEAD_DIM = 128
DN_HEADS = 6
DIL_HEADS = 4
NSA_GROUPS = 2
NSA_REP = 3
DN_CHUNK = 64
DN_SUB = 16
DN_HEADS_PER_STEP = 2
CMP_LEN = 32
CMP_STRIDE = 16
CMP_HIDDEN = 256
SLC_BLOCK = 64
SLC_SHIFT = 6
SLC_TOPK = 16
SLC_KEY_TILE = 512
NSA_BLOCKS_PER_STEP = 4
WIN = 512
Q_BLOCK = 128
FFN_ROW_CHUNKS = 1
DIL_PATTERNS = ((128, 1), (512, 4), (2048, 16))
REL_BUCKETS = 32
REL_MAX_DIST = 2048
EPS = 1e-6
NEG = -1e30
VMEM_LIMIT = 56 * 1024 * 1024

CB_DN_Q, CB_DN_K, CB_DN_V, CB_DN_Z = 0, 6, 12, 18
CB_DIL_Q, CB_DIL_K, CB_DIL_V = 24, 28, 32
CB_NSA_Q = 36
CB_NSA_KV = 42
CB_SMALL = 54
PROJ_COLS = 7168
LANE_B, LANE_A, LANE_GATE = 0, 6, 12


def _bucket_thresholds():
    n = np.arange(0, REL_MAX_DIST + 1)
    max_exact = REL_BUCKETS // 2
    out = []
    for dt in (np.float32, np.float64):
        nf = np.maximum(n, 1).astype(dt)
        large = max_exact + (np.log(nf / dt(max_exact)) / dt(math.log(REL_MAX_DIST / max_exact))
                             * dt(REL_BUCKETS - max_exact)).astype(np.int32)
        large = np.minimum(large, REL_BUCKETS - 1)
        out.append(np.where(n < max_exact, n, large))
    assert np.array_equal(out[0], out[1])
    bucket = out[1]
    assert np.all(np.diff(bucket) >= 0)
    thr = [0]
    for b in range(1, REL_BUCKETS):
        idx = np.nonzero(bucket >= b)[0]
        thr.append(int(idx[0]) if idx.size else REL_MAX_DIST + 1)
    return tuple(thr)


_THR = _bucket_thresholds()


def _bias_of_dist(dist, tab_ref, head):
    val = jnp.full(dist.shape, tab_ref[head, 0], F32)
    for b in range(1, REL_BUCKETS):
        val = jnp.where(dist >= _THR[b], tab_ref[head, b], val)
    return val


def _sigmoid(x):
    return 1.0 / (1.0 + jnp.exp(-x))


def _bf(x):
    return x.astype(BF16)


def _dot(a, b, **kw):
    return jnp.dot(a, b, preferred_element_type=F32, **kw)


def _dot_nt(a, b, **kw):
    return lax.dot_general(a, b, (((1,), (1,)), ((), ())), preferred_element_type=F32, **kw)


def _bmm(a, b):
    return jnp.einsum('nij,njk->nik', _bf(a), _bf(b), preferred_element_type=F32)


def _bmm_nt(a, b):
    return jnp.einsum('nid,njd->nij', _bf(a), _bf(b), preferred_element_type=F32)


def _params(*sem):
    return pltpu.CompilerParams(dimension_semantics=sem, vmem_limit_bytes=VMEM_LIMIT)


def _norm_matmul_kernel(x_ref, g_ref, w_ref, o_ref, h_ref):
    @pl.when(pl.program_id(1) == 0)
    def _():
        x = x_ref[...]
        ms = jnp.mean(x * x, axis=-1, keepdims=True)
        h_ref[...] = (x * lax.rsqrt(ms + EPS) * g_ref[...]).astype(BF16)

    o_ref[...] = _dot_nt(h_ref[...], w_ref[...]).astype(o_ref.dtype)


def _norm_matmul(x, g, w_t, layer, tm, tn, out_dtype):
    m, d = x.shape
    n = w_t.shape[1]
    return pl.pallas_call(
        _norm_matmul_kernel,
        grid=(m // tm, n // tn),
        in_specs=[pl.BlockSpec((tm, d), lambda i, j: (i, 0)),
                  pl.BlockSpec((1, d), lambda i, j: (0, 0)),
                  pl.BlockSpec((None, tn, d), lambda i, j: (layer, j, 0))],
        out_specs=pl.BlockSpec((tm, tn), lambda i, j: (i, j)),
        out_shape=jax.ShapeDtypeStruct((m, n), out_dtype),
        scratch_shapes=[pltpu.VMEM((tm, d), BF16)],
        compiler_params=_params("parallel", "arbitrary"),
        name="in_proj",
    )(x, g.reshape(1, d), w_t)


def _out_proj_kernel(a_ref, b_ref, c_ref, wa_ref, wb_ref, wc_ref, x_ref, o_ref):
    acc = _dot(a_ref[...], wa_ref[...])
    acc = acc + _dot(b_ref[...], wb_ref[...])
    acc = acc + _dot(c_ref[...], wc_ref[...])
    o_ref[...] = x_ref[...] + acc


def _out_proj(o_dn, o_dil, o_nsa, w_out, x, tm):
    m, d = x.shape
    ka, kb, kc = o_dn.shape[1], o_dil.shape[1], o_nsa.shape[1]
    wa, wb, wc = w_out[:ka], w_out[ka:ka + kb], w_out[ka + kb:]
    row = lambda i: (i, 0)
    full = lambda i: (0, 0)
    return pl.pallas_call(
        _out_proj_kernel,
        grid=(m // tm,),
        in_specs=[pl.BlockSpec((tm, ka), row), pl.BlockSpec((tm, kb), row), pl.BlockSpec((tm, kc), row),
                  pl.BlockSpec((ka, d), full), pl.BlockSpec((kb, d), full), pl.BlockSpec((kc, d), full),
                  pl.BlockSpec((tm, d), row)],
        out_specs=pl.BlockSpec((tm, d), row),
        out_shape=jax.ShapeDtypeStruct((m, d), F32),
        compiler_params=_params("parallel"),
        name="out_proj",
    )(o_dn, o_dil, o_nsa, wa, wb, wc, x)


def _ffn_up_kernel(x_ref, xp_ref, g_ref, wg_ref, wv_ref, cg_ref, cv_ref, o_ref, h_ref, *, tiles_per_seq):
    tm = x_ref.shape[0]
    halo = xp_ref.shape[0]
    tn = o_ref.shape[1]

    @pl.when(pl.program_id(1) == 0)
    def _():
        def norm(x):
            ms = jnp.mean(x * x, axis=-1, keepdims=True)
            return (x * lax.rsqrt(ms + EPS) * g_ref[...]).astype(BF16)
        first = (pl.program_id(0) % tiles_per_seq) == 0
        hp = norm(xp_ref[...])
        h_ref[0:halo, :] = jnp.where(first, jnp.zeros_like(hp), hp)
        h_ref[halo:halo + tm, :] = norm(x_ref[...])

    def conv(h, w_ref, c_ref):
        u = _dot(h, w_ref[...])
        y = u * c_ref[2:3, :]
        y = y + pltpu.roll(u, 1, axis=0) * c_ref[1:2, :]
        y = y + pltpu.roll(u, 2, axis=0) * c_ref[0:1, :]
        return y[halo:, :]

    rows = tm // FFN_ROW_CHUNKS
    for s in range(FFN_ROW_CHUNKS):
        h = h_ref[s * rows:s * rows + halo + rows, :]
        gate = conv(h, wg_ref, cg_ref)
        val = conv(h, wv_ref, cv_ref)
        o_ref[s * rows:(s + 1) * rows, :] = (gate * _sigmoid(gate) * val).astype(o_ref.dtype)


def _ffn_up(x, g, w_up, conv_w, layer, seq, tm, tn):
    m, d = x.shape
    f = w_up.shape[2] // 2
    taps = conv_w.shape[1]
    halo = 8
    nf = f // tn
    kern = functools.partial(_ffn_up_kernel, tiles_per_seq=seq // tm)
    return pl.pallas_call(
        kern,
        grid=(m // tm, nf),
        in_specs=[pl.BlockSpec((tm, d), lambda i, j: (i, 0)),
                  pl.BlockSpec((halo, d), lambda i, j: (jnp.maximum(i * (tm // halo) - 1, 0), 0)),
                  pl.BlockSpec((1, d), lambda i, j: (0, 0)),
                  pl.BlockSpec((None, d, tn), lambda i, j: (layer, 0, j)),
                  pl.BlockSpec((None, d, tn), lambda i, j: (layer, 0, j + nf)),
                  pl.BlockSpec((None, taps, tn), lambda i, j: (layer, 0, j)),
                  pl.BlockSpec((None, taps, tn), lambda i, j: (layer, 0, j + nf))],
        out_specs=pl.BlockSpec((tm, tn), lambda i, j: (i, j)),
        out_shape=jax.ShapeDtypeStruct((m, f), BF16),
        scratch_shapes=[pltpu.VMEM((tm + halo, d), BF16)],
        compiler_params=_params("parallel", "arbitrary"),
        name="ffn_up",
    )(x, x, g.reshape(1, d), w_up, w_up, conv_w, conv_w)


def _ffn_down_kernel(a_ref, w_ref, x_ref, o_ref):
    o_ref[...] = x_ref[...] + _dot(a_ref[...], w_ref[...])


def _ffn_down(act, w_down, layer, x, tm, tn):
    m, f = act.shape
    d = w_down.shape[2]
    return pl.pallas_call(
        _ffn_down_kernel,
        grid=(d // tn, m // tm),
        in_specs=[pl.BlockSpec((tm, f), lambda j, i: (i, 0)),
                  pl.BlockSpec((None, f, tn), lambda j, i: (layer, 0, j)),
                  pl.BlockSpec((tm, tn), lambda j, i: (i, j))],
        out_specs=pl.BlockSpec((tm, tn), lambda j, i: (i, j)),
        out_shape=jax.ShapeDtypeStruct((m, d), F32),
        compiler_params=_params("parallel", "parallel"),
        name="ffn_down",
    )(act, w_down, x)


def _rmsnorm_kernel(x_ref, g_ref, o_ref):
    x = x_ref[...]
    ms = jnp.mean(x * x, axis=-1, keepdims=True)
    o_ref[...] = x * lax.rsqrt(ms + EPS) * g_ref[...]


def _rmsnorm(x, g, tm):
    m, d = x.shape
    return pl.pallas_call(
        _rmsnorm_kernel,
        grid=(m // tm,),
        in_specs=[pl.BlockSpec((tm, d), lambda i: (i, 0)), pl.BlockSpec((1, d), lambda i: (0, 0))],
        out_specs=pl.BlockSpec((tm, d), lambda i: (i, 0)),
        out_shape=jax.ShapeDtypeStruct((m, d), F32),
        compiler_params=_params("parallel"),
        name="final_norm",
    )(x, g.reshape(1, d))


def _dn_prepare(h, ls, alog_ref, dtb_ref, q_ref, k_ref, v_ref, s_ref, cq_ref, ck_ref, cv_ref,
                lhs_s, n_s, op_s, egl_s):
    seq = q_ref.shape[0]
    c = DN_CHUNK
    nc = seq // c

    def conv_silu(x_ref, c_ref):
        taps = c_ref.shape[0]

        def conv(x, mask_rows):
            acc = x * c_ref[taps - 1:taps, ls]
            for s in range(1, taps):
                xs = pltpu.roll(x, s, axis=0)
                if mask_rows is not None:
                    xs = jnp.where(mask_rows >= s, xs, 0.0)
                acc = acc + xs * c_ref[taps - 1 - s:taps - s, ls]
            return acc

        x = x_ref[:, ls]
        first = conv(x[0:8, :], lax.broadcasted_iota(jnp.int32, (8, 1), 0))
        acc = jnp.concatenate([first, conv(x, None)[8:, :]], axis=0)
        return acc * _sigmoid(acc)

    def l2norm(x):
        return x * lax.rsqrt(jnp.sum(x * x, axis=-1, keepdims=True) + EPS)

    q = l2norm(conv_silu(q_ref, cq_ref)) * (HEAD_DIM ** -0.5)
    k = l2norm(conv_silu(k_ref, ck_ref))
    v = conv_silu(v_ref, cv_ref)

    lane = lax.broadcasted_iota(jnp.int32, (1, LANES), 1)
    sm = s_ref[...]
    b_col = jnp.sum(jnp.where(lane == LANE_B + h, sm, 0.0), axis=-1, keepdims=True)
    a_col = jnp.sum(jnp.where(lane == LANE_A + h, sm, 0.0), axis=-1, keepdims=True)
    beta = _sigmoid(b_col)
    ap = a_col + dtb_ref[h]
    softplus = jnp.maximum(ap, 0.0) + jnp.log1p(jnp.exp(-jnp.abs(ap)))
    neg_rate = -jnp.exp(jnp.full((1, 1), alog_ref[h], F32))
    g = jnp.broadcast_to(neg_rate * softplus, (seq, LANES))

    ii = lax.broadcasted_iota(jnp.int32, (c, c), 0)
    jj = lax.broadcasted_iota(jnp.int32, (c, c), 1)
    tri = jnp.broadcast_to(jnp.where(ii >= jj, 1.0, 0.0).astype(F32), (nc, c, c))
    gc3 = jnp.einsum('nij,njk->nik', tri, g.reshape(nc, c, LANES),
                     preferred_element_type=F32, precision=HIGHEST)
    gc = gc3.reshape(seq, LANES)
    egc = jnp.exp(gc)
    glast3 = jnp.broadcast_to(gc3[:, c - 1:c, :], (nc, c, LANES))
    kb = k * beta
    k3 = k.reshape(nc, c, HEAD_DIM)
    kdt = jnp.swapaxes(k3 * jnp.exp(glast3 - gc3), 1, 2)
    egl_s[...] = jnp.exp(glast3[:, 0:8, :])

    gcol = gc3[:, :, 0:c]
    grow = jnp.swapaxes(gc3, 1, 2)[:, 0:c, :]
    decay = jnp.where(ii >= jj, jnp.exp(jnp.minimum(gcol - grow, 0.0)), 0.0)

    attn = _bmm_nt(q.reshape(nc, c, HEAD_DIM), k3) * decay
    a_mat = jnp.where(ii > jj, _bmm_nt(kb.reshape(nc, c, HEAD_DIM), k3) * decay, 0.0)

    same = (ii & -DN_SUB) == (jj & -DN_SUB)
    eye = jnp.where(ii == jj, 1.0, 0.0).astype(F32)
    x1 = jnp.where(same, -a_mat, 0.0)
    a_off = jnp.where(same, 0.0, a_mat)
    t_d = eye + x1
    xp = x1
    p = 2
    while p < DN_SUB:
        xp = _bmm(xp, xp)
        t_d = t_d + _bmm(t_d, xp)
        p *= 2
    y1 = -_bmm(t_d, a_off)
    t_o = eye + y1
    yp = y1
    p = 2
    while p < c // DN_SUB:
        yp = _bmm(yp, yp)
        t_o = t_o + _bmm(t_o, yp)
        p *= 2
    t_mat = _bmm(t_o, t_d)

    u = _bmm(t_mat, (v * beta).reshape(nc, c, HEAD_DIM))
    w = _bmm(t_mat, (kb * egc).reshape(nc, c, HEAD_DIM))

    lhs_s[:, 0:HEAD_DIM, :] = _bf(-_bmm(kdt, w))
    lhs_s[:, HEAD_DIM:HEAD_DIM + c, :] = _bf((q * egc).reshape(nc, c, HEAD_DIM) - _bmm(attn, w))
    n_s[...] = _bmm(kdt, u)
    op_s[...] = _bmm(attn, u)


def _dn_kernel(alog_ref, dtb_ref, q_ref, k_ref, v_ref, z_ref, s_ref, cq_ref, ck_ref, cv_ref, ng_ref,
               o_ref, lhs_s, n_s, op_s, egl_s, o_s):
    hs = DN_HEADS_PER_STEP
    seq = q_ref.shape[0]
    c = DN_CHUNK
    lanes = [slice(u * HEAD_DIM, (u + 1) * HEAD_DIM) for u in range(hs)]
    for u in range(hs):
        _dn_prepare(pl.program_id(1) * hs + u, lanes[u], alog_ref, dtb_ref, q_ref, k_ref, v_ref, s_ref,
                    cq_ref, ck_ref, cv_ref, lhs_s.at[u], n_s.at[u], op_s.at[u], egl_s.at[u])

    def chunk_step(n, states):
        rows = pl.ds(pl.multiple_of(n * c, c), c)
        out = []
        for u in range(hs):
            r = _dot(lhs_s[u, n], _bf(states[u]))
            o_s[u, rows, :] = r[HEAD_DIM:, :] + op_s[u, n]
            out.append(states[u] * egl_s[u, n][0:1, :] + r[0:HEAD_DIM, :] + n_s[u, n])
        return tuple(out)

    lax.fori_loop(0, seq // c, chunk_step, tuple(jnp.zeros((HEAD_DIM, HEAD_DIM), F32) for _ in range(hs)))

    for u in range(hs):
        o = o_s[u]
        y = o * lax.rsqrt(jnp.mean(o * o, axis=-1, keepdims=True) + EPS) * ng_ref[...]
        z = z_ref[:, lanes[u]]
        o_ref[:, lanes[u]] = (y * (z * _sigmoid(z))).astype(o_ref.dtype)


def _deltanet(proj, dn_conv, a_log, dt_bias, norm_g):
    bsz, seq, _ = proj.shape
    nc = seq // DN_CHUNK
    hs = DN_HEADS_PER_STEP
    width = hs * HEAD_DIM
    assert DN_HEADS % hs == 0 and all(cb % hs == 0 for cb in (CB_DN_Q, CB_DN_K, CB_DN_V, CB_DN_Z))
    col = lambda off: pl.BlockSpec((None, seq, width), lambda b, p: (b, 0, off // hs + p))
    cw = lambda off: pl.BlockSpec((dn_conv.shape[0], width), lambda b, p: (0, off // hs + p))
    smem = pl.BlockSpec(memory_space=pltpu.SMEM)
    return pl.pallas_call(
        _dn_kernel,
        grid=(bsz, DN_HEADS // hs),
        in_specs=[smem, smem, col(CB_DN_Q), col(CB_DN_K), col(CB_DN_V), col(CB_DN_Z),
                  pl.BlockSpec((None, seq, LANES), lambda b, p: (b, 0, CB_SMALL)),
                  cw(CB_DN_Q), cw(CB_DN_K), cw(CB_DN_V),
                  pl.BlockSpec((1, HEAD_DIM), lambda b, p: (0, 0))],
        out_specs=pl.BlockSpec((None, seq, width), lambda b, p: (b, 0, p)),
        out_shape=jax.ShapeDtypeStruct((bsz, seq, DN_HEADS * HEAD_DIM), BF16),
        scratch_shapes=[pltpu.VMEM((hs, nc, HEAD_DIM + DN_CHUNK, HEAD_DIM), BF16),
                        pltpu.VMEM((hs, nc, HEAD_DIM, HEAD_DIM), F32),
                        pltpu.VMEM((hs, nc, DN_CHUNK, HEAD_DIM), F32),
                        pltpu.VMEM((hs, nc, 8, LANES), F32),
                        pltpu.VMEM((hs, seq, HEAD_DIM), F32)],
        compiler_params=_params("parallel", "parallel"),
        name="deltanet",
    )(a_log, dt_bias, proj, proj, proj, proj, proj, dn_conv, dn_conv, dn_conv, norm_g.reshape(1, HEAD_DIM))


def _dil_kernel(tab_ref, q_ref, k_ref, v_ref, o_ref, num_s, m_s, l_s, tile_s, perm_s, pres_s):
    head = pl.program_id(0)
    seq = q_ref.shape[0]
    qb = Q_BLOCK
    scale = HEAD_DIM ** -0.5

    @pl.when(pl.program_id(1) == 0)
    def _():
        r = lax.broadcasted_iota(jnp.int32, (qb, 2 * qb), 0)
        cidx = lax.broadcasted_iota(jnp.int32, (qb, 2 * qb), 1)
        sub = qb + r - cidx
        for pat, (window, dil) in enumerate(DIL_PATTERNS):
            valid = (sub >= 0) & (sub <= window // dil)
            tile_s[pat] = jnp.where(valid, _bias_of_dist(jnp.maximum(sub, 0) * dil, tab_ref, head), NEG)

    def block(qv, kv, vv, tile):
        s = _dot_nt((qv * scale).astype(BF16), kv.astype(BF16)) + tile
        m = jnp.max(s, axis=-1, keepdims=True)
        p = jnp.exp(s - m)
        l = jnp.sum(p, axis=-1, keepdims=True)
        return _dot(p.astype(BF16), vv.astype(BF16)), m, l

    st = DIL_PATTERNS[1][1]
    assert [d for _, d in DIL_PATTERNS] == [1, st, st * st] and all(w // d == qb for w, d in DIL_PATTERNS)
    grp = seq // st
    for rho in range(st):
        src = pl.ds(rho, grp, stride=st)
        for n, ref in enumerate((q_ref, k_ref, v_ref)):
            perm_s[n, rho * grp:(rho + 1) * grp, :] = ref[src, :]

    nat = (num_s, m_s, l_s)

    def store_nat(pat):
        def store(rows_, vals):
            for buf, val in zip(nat, vals):
                buf[pat, rows_, :] = val
        return store

    def store_perm(rows_, vals):
        for n, val in enumerate(vals):
            pres_s[n, rows_, :] = val

    def band(pat, read, n_blocks, out_rows, store):
        tile = tile_s[pat]
        for bi in range(n_blocks):
            if bi == 0:
                kb, t = (0, 1), tile[:, qb:]
            else:
                kb, t = (bi - 1, 2), tile
            num, m, l = block(read(0, bi, 1), read(1, *kb), read(2, *kb), t)
            store(out_rows(bi), (num, jnp.broadcast_to(m, (qb, LANES)), jnp.broadcast_to(l, (qb, LANES))))

    refs = (q_ref, k_ref, v_ref)
    band(0, lambda n, b0, nb: refs[n][b0 * qb:(b0 + nb) * qb, :], seq // qb,
         lambda bi: pl.ds(bi * qb, qb), store_nat(0))
    for rho in range(st):
        base = rho * grp
        band(1, lambda n, b0, nb, base=base: perm_s[n, base + b0 * qb:base + (b0 + nb) * qb, :], grp // qb,
             lambda bi, rho=rho: pl.ds(rho + st * qb * bi, qb, stride=st), store_nat(1))
        for sigma in range(st):
            walk = lambda b0, nb, base=base, sigma=sigma: pl.ds(base + sigma + st * qb * b0, nb * qb, stride=st)
            band(2, lambda n, b0, nb, walk=walk: perm_s[n, walk(b0, nb), :], grp // st // qb,
                 lambda bi, walk=walk: walk(bi, 1), store_perm)
    for rho in range(st):
        for n, buf in enumerate(nat):
            buf[2, pl.ds(rho, grp, stride=st), :] = pres_s[n, rho * grp:(rho + 1) * grp, :]

    m_all = m_s[0]
    for pat in range(1, len(DIL_PATTERNS)):
        m_all = jnp.maximum(m_all, m_s[pat])
    num = jnp.zeros((seq, HEAD_DIM), F32)
    den = jnp.zeros((seq, LANES), F32)
    for pat in range(len(DIL_PATTERNS)):
        wgt = jnp.exp(m_s[pat] - m_all)
        num = num + num_s[pat] * wgt
        den = den + l_s[pat] * wgt
    o_ref[...] = (num / den).astype(o_ref.dtype)


def _dilated(proj, dil_tab):
    bsz, seq, _ = proj.shape
    col = lambda off: pl.BlockSpec((None, seq, LANES), lambda h, b: (b, 0, off + h))
    return pl.pallas_call(
        _dil_kernel,
        grid=(DIL_HEADS, bsz),
        in_specs=[pl.BlockSpec(memory_space=pltpu.SMEM), col(CB_DIL_Q), col(CB_DIL_K), col(CB_DIL_V)],
        out_specs=pl.BlockSpec((None, seq, HEAD_DIM), lambda h, b: (b, 0, h)),
        out_shape=jax.ShapeDtypeStruct((bsz, seq, DIL_HEADS * HEAD_DIM), BF16),
        scratch_shapes=[pltpu.VMEM((len(DIL_PATTERNS), seq, HEAD_DIM), F32),
                        pltpu.VMEM((len(DIL_PATTERNS), seq, LANES), F32),
                        pltpu.VMEM((len(DIL_PATTERNS), seq, LANES), F32),
                        pltpu.VMEM((len(DIL_PATTERNS), Q_BLOCK, 2 * Q_BLOCK), F32),
                        pltpu.VMEM((3, seq, HEAD_DIM), F32),
                        pltpu.VMEM((3, seq, LANES), F32)],
        compiler_params=_params("arbitrary", "arbitrary"),
        name="dilated",
    )(dil_tab, proj, proj, proj)


def _cmp_kernel(x_ref, pos_ref, w1_ref, w2_ref, o_ref):
    nblk = o_ref.shape[0]
    half = CMP_LEN // 2
    ha = jnp.zeros((nblk, CMP_HIDDEN), F32)
    hb = jnp.zeros((nblk, CMP_HIDDEN), F32)
    for l in range(half):
        xl = x_ref[pl.ds(l, nblk, stride=CMP_STRIDE), :]
        wa = w1_ref[l * HEAD_DIM:(l + 1) * HEAD_DIM, :]
        wb = w1_ref[(half + l) * HEAD_DIM:(half + l + 1) * HEAD_DIM, :]
        ha = ha + _dot((xl + pos_ref[l:l + 1, :]).astype(BF16), wa)
        hb = hb + _dot((xl + pos_ref[half + l:half + l + 1, :]).astype(BF16), wb)
    hmid = ha + pltpu.roll(hb, nblk - 1, axis=0)
    hmid = 0.5 * hmid * (1.0 + jnp.tanh(math.sqrt(2.0 / math.pi) * (hmid + 0.044715 * hmid * hmid * hmid)))
    out = _dot(hmid.astype(BF16), w2_ref[...])
    rowi = lax.broadcasted_iota(jnp.int32, out.shape, 0)
    o_ref[...] = jnp.where(rowi < nblk - 1, out, 0.0)


def _compress(proj, cmp_pos, w1, w2):
    bsz, seq, _ = proj.shape
    nblk = seq // CMP_STRIDE
    return pl.pallas_call(
        _cmp_kernel,
        grid=(2, bsz, NSA_GROUPS),
        in_specs=[pl.BlockSpec((None, seq, LANES), lambda i, b, g: (b, 0, CB_NSA_KV + i * NSA_GROUPS + g)),
                  pl.BlockSpec((None, CMP_LEN, HEAD_DIM), lambda i, b, g: (i, 0, 0)),
                  pl.BlockSpec((None, CMP_LEN * HEAD_DIM, CMP_HIDDEN), lambda i, b, g: (i, 0, 0)),
                  pl.BlockSpec((None, CMP_HIDDEN, HEAD_DIM), lambda i, b, g: (i, 0, 0))],
        out_specs=pl.BlockSpec((None, None, None, nblk, HEAD_DIM), lambda i, b, g: (b, g, i, 0, 0)),
        out_shape=jax.ShapeDtypeStruct((bsz, NSA_GROUPS, 2, nblk, HEAD_DIM), F32),
        compiler_params=_params("parallel", "parallel", "parallel"),
        name="nsa_compress",
    )(proj, cmp_pos, w1, w2)


def _nsa_kernel(tab_ref, q0_ref, q1_ref, q2_ref, ks_ref, vs_ref, kw_ref, vw_ref, cmp_ref, s_ref,
                o_ref, strip_s, cbias_s, wtile_s, q_s, kv_s, oslc_s):
    grp = pl.program_id(0)
    step = pl.program_id(2)
    seq = q0_ref.shape[0]
    qb = Q_BLOCK
    nsub = o_ref.shape[0] // qb
    nqb = seq // qb
    n_slc = seq // SLC_BLOCK
    ncmp = cmp_ref.shape[1]
    nwin = WIN // qb
    win_keys = WIN + qb
    scale = HEAD_DIM ** -0.5
    q_refs = (q0_ref, q1_ref, q2_ref)
    head_rows = lambda x, rr: x[rr * qb:(rr + 1) * qb, :]

    @pl.when(step == 0)
    def _():
        for rr in range(NSA_REP):
            q_s[rr] = (q_refs[rr][...] * scale).astype(BF16)
        for n, ref in enumerate((ks_ref, vs_ref, kw_ref, vw_ref)):
            kv_s[n] = ref[...].astype(BF16)

    @pl.when(jnp.logical_and(pl.program_id(1) == 0, step == 0))
    def _():
        for rr in range(NSA_REP):
            head = DIL_HEADS + grp * NSA_REP + rr
            shape = strip_s.shape[1:]
            dist = (lax.broadcasted_iota(jnp.int32, shape, 0) - lax.broadcasted_iota(jnp.int32, shape, 1)
                    + (seq - qb))
            strip_s[rr] = jnp.where(dist >= 0, _bias_of_dist(jnp.maximum(dist, 0), tab_ref, head), NEG)
            shape = cbias_s.shape[1:]
            dist = (lax.broadcasted_iota(jnp.int32, shape, 0)
                    - (lax.broadcasted_iota(jnp.int32, shape, 1) * CMP_STRIDE + CMP_LEN - 1))
            cbias_s[rr] = jnp.where(dist >= 0, _bias_of_dist(jnp.maximum(dist, 0), tab_ref, head), NEG)
            shape = wtile_s.shape[1:]
            dist = lax.broadcasted_iota(jnp.int32, shape, 0) + WIN - lax.broadcasted_iota(jnp.int32, shape, 1)
            wtile_s[rr] = jnp.where((dist >= 0) & (dist < WIN),
                                    _bias_of_dist(jnp.clip(dist, 0, WIN), tab_ref, head), NEG)

    lane = lax.broadcasted_iota(jnp.int32, (1, LANES), 1)
    oj = lax.broadcasted_iota(jnp.int32, (n_slc, ncmp), 0) * SLC_BLOCK
    oc = lax.broadcasted_iota(jnp.int32, (n_slc, ncmp), 1) * CMP_STRIDE
    overlap_t = (jnp.maximum(jnp.minimum(oc + CMP_LEN, oj + SLC_BLOCK) - jnp.maximum(oc, oj), 0)
                 .astype(F32) / CMP_STRIDE)
    jrow = lax.broadcasted_iota(jnp.int32, (n_slc, qb), 0)
    tcol = lax.broadcasted_iota(jnp.int32, (n_slc, qb), 1)

    def softmax_rows(s):
        m = jnp.max(s, axis=-1, keepdims=True)
        p = jnp.exp(s - m)
        return p, jnp.sum(p, axis=-1, keepdims=True)

    def front(i):
        rows = pl.ds(pl.multiple_of(i * qb, qb), qb)
        q3 = jnp.concatenate([q_s[rr, rows, :] for rr in range(NSA_REP)], axis=0)
        sm = _sigmoid(s_ref[rows, :])

        def gate(rr, branch):
            ln = LANE_GATE + (grp * NSA_REP + rr) * 3 + branch
            return jnp.sum(jnp.where(lane == ln, sm, 0.0), axis=-1, keepdims=True)

        s_all = _dot_nt(q3, _bf(cmp_ref[0]))
        p_sum = jnp.zeros((qb, ncmp), F32)
        probs = []
        for rr in range(NSA_REP):
            cb = cbias_s[rr, rows, :]
            p, l = softmax_rows(head_rows(s_all, rr) + cb)
            p = jnp.where(cb > 0.5 * NEG, p, 0.0)
            p = p / jnp.maximum(jnp.sum(p, axis=-1, keepdims=True), 1e-30)
            p_sum = p_sum + p
            probs.append(_bf(p))
        o_cmp = _dot(jnp.concatenate(probs, axis=0), _bf(cmp_ref[1]))
        acc = [gate(rr, 0) * head_rows(o_cmp, rr) for rr in range(NSA_REP)]

        imp = _dot_nt(overlap_t, p_sum, precision=HIGHEST)
        cur = jnp.right_shift(i * qb + tcol, SLC_SHIFT)
        forced = (jrow == 0) | ((jrow <= cur) & (jrow > cur - 2))
        imp = jnp.where(forced, jnp.inf, jnp.where(jrow <= cur, imp, -jnp.inf))
        cnt = jnp.zeros((n_slc, qb), F32)
        for j2 in range(n_slc):
            other = imp[j2:j2 + 1, :]
            ge = jnp.where(other >= imp, 1.0, 0.0)
            gt = jnp.where(other > imp, 1.0, 0.0)
            cnt = cnt + jnp.where(jrow > j2, ge, gt)
        sel_t = jnp.where((cnt < SLC_TOPK) & (jrow <= cur), 1.0, 0.0)
        sel_t = jnp.concatenate([sel_t, jnp.zeros((LANES - n_slc, qb), F32)], axis=0)
        sel = _bf(sel_t.T)

        j0 = jnp.maximum(i - nwin, 0)
        krows = pl.ds(pl.multiple_of(j0 * qb, qb), win_keys)
        tile_start = pl.multiple_of((j0 - i + nwin) * qb, qb)
        s_all = _dot_nt(q3, kv_s[2, krows, :])
        probs, dens = [], []
        for rr in range(NSA_REP):
            p, l = softmax_rows(head_rows(s_all, rr) + wtile_s[rr, :, pl.ds(tile_start, win_keys)])
            probs.append(_bf(p))
            dens.append(l)
        pv = _dot(jnp.concatenate(probs, axis=0), kv_s[3, krows, :])
        acc = [acc[rr] + gate(rr, 2) * (head_rows(pv, rr) / dens[rr]) for rr in range(NSA_REP)]
        return q3, sel, acc, [gate(rr, 1) for rr in range(NSA_REP)]

    def selected(u, i, q3, sel, nk):
        strip_start = pl.multiple_of((nqb - 1 - i) * qb, qb)
        s_all = _dot_nt(q3, kv_s[0, 0:nk, :])
        ej = lax.broadcasted_iota(jnp.int32, (LANES, nk), 0)
        ep = lax.broadcasted_iota(jnp.int32, (LANES, nk), 1)
        expand = jnp.where(ej == jnp.right_shift(ep, SLC_SHIFT), 1.0, 0.0).astype(BF16)
        keep = _dot(sel, expand) > 0.5
        probs, dens = [], []
        for rr in range(NSA_REP):
            s = head_rows(s_all, rr) + strip_s[rr, :, pl.ds(strip_start, nk)]
            p, l = softmax_rows(jnp.where(keep, s, NEG))
            probs.append(_bf(p))
            dens.append(l)
        pv = _dot(jnp.concatenate(probs, axis=0), kv_s[1, 0:nk, :])
        for rr in range(NSA_REP):
            oslc_s[u, rr] = head_rows(pv, rr) / dens[rr]

    blocks = [step * nsub + u for u in range(nsub)]
    fronts = [front(i) for i in blocks]
    blocks_per_tile = SLC_KEY_TILE // qb
    assert blocks_per_tile % nsub == 0
    for kq in range(seq // SLC_KEY_TILE):
        @pl.when(blocks[0] // blocks_per_tile == kq)
        def _(kq=kq):
            for u, i in enumerate(blocks):
                selected(u, i, fronts[u][0], fronts[u][1], (kq + 1) * SLC_KEY_TILE)

    for u in range(nsub):
        _, _, acc, gate_slc = fronts[u]
        for rr in range(NSA_REP):
            out = acc[rr] + gate_slc[rr] * oslc_s[u, rr]
            o_ref[u * qb:(u + 1) * qb, rr * HEAD_DIM:(rr + 1) * HEAD_DIM] = out.astype(o_ref.dtype)


def _nsa(proj, cmp_kv, rel_bias):
    bsz, seq, _ = proj.shape
    ncmp = cmp_kv.shape[3]
    nqb = seq // Q_BLOCK
    strip_w = (nqb - 1) * Q_BLOCK + SLC_KEY_TILE
    wtile_w = WIN + WIN + Q_BLOCK
    col = lambda fn: pl.BlockSpec((None, seq, LANES), lambda g, b, i: (b, 0, fn(g)))
    kv = lambda branch, which: col(lambda g: CB_NSA_KV + (branch * 2 + which) * NSA_GROUPS + g)
    return pl.pallas_call(
        _nsa_kernel,
        grid=(NSA_GROUPS, bsz, nqb // NSA_BLOCKS_PER_STEP),
        in_specs=[pl.BlockSpec(memory_space=pltpu.SMEM),
                  col(lambda g: CB_NSA_Q + g * NSA_REP), col(lambda g: CB_NSA_Q + g * NSA_REP + 1),
                  col(lambda g: CB_NSA_Q + g * NSA_REP + 2),
                  kv(1, 0), kv(1, 1), kv(2, 0), kv(2, 1),
                  pl.BlockSpec((None, None, 2, ncmp, HEAD_DIM), lambda g, b, i: (b, g, 0, 0, 0)),
                  pl.BlockSpec((None, seq, LANES), lambda g, b, i: (b, 0, CB_SMALL))],
        out_specs=pl.BlockSpec((None, NSA_BLOCKS_PER_STEP * Q_BLOCK, NSA_REP * HEAD_DIM),
                               lambda g, b, i: (b, i, g)),
        out_shape=jax.ShapeDtypeStruct((bsz, seq, NSA_GROUPS * NSA_REP * HEAD_DIM), BF16),
        scratch_shapes=[pltpu.VMEM((NSA_REP, Q_BLOCK, strip_w), F32),
                        pltpu.VMEM((NSA_REP, seq, ncmp), F32),
                        pltpu.VMEM((NSA_REP, Q_BLOCK, wtile_w), F32),
                        pltpu.VMEM((NSA_REP, seq, HEAD_DIM), BF16),
                        pltpu.VMEM((4, seq, HEAD_DIM), BF16),
                        pltpu.VMEM((NSA_BLOCKS_PER_STEP, NSA_REP, Q_BLOCK, HEAD_DIM), F32)],
        compiler_params=_params("arbitrary", "arbitrary", "arbitrary"),
        name="nsa_attention",
    )(rel_bias, proj, proj, proj, proj, proj, proj, proj, cmp_kv, proj)


IN_WIDE_A = 3072
IN_SMALL_A = 12
IN_WIDE_B = 3840
IN_SMALL_B = 18


def _w_in_prep_kernel(a_ref, b_ref, c_ref, o_ref):
    j = pl.program_id(0)
    depth = o_ref.shape[0]
    blk = o_ref.shape[1]
    n_a = IN_WIDE_A // blk
    n_b = IN_WIDE_B // blk
    s = IN_SMALL_A

    def put(lo, hi, src_ref, src_lo):
        for l in range(depth):
            o_ref[l, lo:hi, :] = src_ref[src_lo:src_lo + hi - lo, l, :].astype(o_ref.dtype)

    @pl.when(j < n_a)
    def _():
        put(0, blk, a_ref, 0)

    @pl.when(jnp.logical_and(j >= n_a, j < n_a + n_b))
    def _():
        put(0, blk - s, a_ref, s)
        put(blk - s, blk, b_ref, 0)

    @pl.when(j == n_a + n_b)
    def _():
        put(0, s, c_ref, 0)
        put(s, s + IN_SMALL_B, a_ref, s)
        o_ref[:, s + IN_SMALL_B:, :] = jnp.zeros((depth, blk - s - IN_SMALL_B, o_ref.shape[2]), o_ref.dtype)

    @pl.when(j > n_a + n_b)
    def _():
        o_ref[...] = jnp.zeros(o_ref.shape, o_ref.dtype)


def _prep_w_in(w):
    depth, d, cols = w.shape
    blk = LANES
    assert cols == IN_WIDE_A + IN_SMALL_A + IN_WIDE_B + IN_SMALL_B
    wt = jnp.transpose(w, (2, 0, 1))
    last = (cols - 1) // blk
    nxt = 16
    assert IN_SMALL_A <= nxt and blk % nxt == 0
    last_nxt = (cols - 1) // nxt
    return pl.pallas_call(
        _w_in_prep_kernel,
        grid=(PROJ_COLS // blk,),
        in_specs=[pl.BlockSpec((blk, depth, d), lambda j: (jnp.minimum(j, last), 0, 0)),
                  pl.BlockSpec((nxt, depth, d), lambda j: (jnp.minimum((j + 1) * (blk // nxt), last_nxt), 0, 0)),
                  pl.BlockSpec((nxt, depth, d), lambda j: (IN_WIDE_A // nxt, 0, 0))],
        out_specs=pl.BlockSpec((depth, blk, d), lambda j: (0, j, 0)),
        out_shape=jax.ShapeDtypeStruct((depth, PROJ_COLS, d), BF16),
        compiler_params=_params("parallel"),
        name="w_in_prep",
    )(wt, wt, wt)


_DENSE_TILES = {
    "in_proj": (1024, 1024),
    "out_proj": 512,
    "ffn_up": (1024, 512),
    "ffn_down": (1024, 512),
    "final_norm": 512,
}


def kernel(x, norm1_g, w_in, dn_conv, dn_a_log, dn_dt_bias, dn_norm_g, cmp_pos, cmp_w1, cmp_w2, w_out,
           norm2_g, ffn_up, ffn_conv, ffn_down, rel_bias, final_g):
    bsz, seq, d = x.shape
    depth = w_in.shape[0]
    w_in_b = _prep_w_in(w_in)
    cmp_w1_b, cmp_w2_b = cmp_w1.astype(BF16), cmp_w2.astype(BF16)
    w_out_b, ffn_up_b, ffn_down_b = w_out.astype(BF16), ffn_up.astype(BF16), ffn_down.astype(BF16)
    xf = x.reshape(bsz * seq, d)
    t = _DENSE_TILES
    for l in range(depth):
        proj = _norm_matmul(xf, norm1_g[l], w_in_b, l, *t["in_proj"], F32).reshape(bsz, seq, PROJ_COLS)
        o_dn = _deltanet(proj, dn_conv[l], dn_a_log[l], dn_dt_bias[l], dn_norm_g[l])
        o_dil = _dilated(proj, rel_bias)
        cmp_kv = _compress(proj, cmp_pos[l], cmp_w1_b[l], cmp_w2_b[l])
        o_nsa = _nsa(proj, cmp_kv, rel_bias)
        xf = _out_proj(o_dn.reshape(bsz * seq, -1), o_dil.reshape(bsz * seq, -1), o_nsa.reshape(bsz * seq, -1),
                       w_out_b[l], xf, t["out_proj"])
        act = _ffn_up(xf, norm2_g[l], ffn_up_b, ffn_conv, l, seq, *t["ffn_up"])
        xf = _ffn_down(act, ffn_down_b, l, xf, *t["ffn_down"])
    return _rmsnorm(xf, final_g, t["final_norm"]).reshape(bsz, seq, d)
```

```python
import functools
import math

import numpy as np
import jax
import jax.numpy as jnp
from jax import lax
from jax.experimental import pallas as pl
from jax.experimental.pallas import tpu as pltpu

F32 = jnp.float32
BF16 = jnp.bfloat16
HIGHEST = lax.Precision.HIGHEST

LANES = 128
HEAD_DIM = 128
DN_HEADS = 6
DIL_HEADS = 4
NSA_GROUPS = 2
NSA_REP = 3
DN_CHUNK = 64
DN_SUB = 16
DN_HEADS_PER_STEP = 2
CMP_LEN = 32
CMP_STRIDE = 16
CMP_HIDDEN = 256
SLC_BLOCK = 64
SLC_SHIFT = 6
SLC_TOPK = 16
SLC_KEY_TILE = 512
NSA_BLOCKS_PER_STEP = 4
WIN = 512
Q_BLOCK = 128
FFN_ROW_CHUNKS = 1
DIL_PATTERNS = ((128, 1), (512, 4), (2048, 16))
REL_BUCKETS = 32
REL_MAX_DIST = 2048
EPS = 1e-6
NEG = -1e30
VMEM_LIMIT = 56 * 1024 * 1024

CB_DN_Q, CB_DN_K, CB_DN_V, CB_DN_Z = 0, 6, 12, 18
CB_DIL_Q, CB_DIL_K, CB_DIL_V = 24, 28, 32
CB_NSA_Q = 36
CB_NSA_KV = 42
CB_SMALL = 54
PROJ_COLS = 7168
LANE_B, LANE_A, LANE_GATE = 0, 6, 12


def _bucket_thresholds():
    n = np.arange(0, REL_MAX_DIST + 1)
    max_exact = REL_BUCKETS // 2
    out = []
    for dt in (np.float32, np.float64):
        nf = np.maximum(n, 1).astype(dt)
        large = max_exact + (np.log(nf / dt(max_exact)) / dt(math.log(REL_MAX_DIST / max_exact))
                             * dt(REL_BUCKETS - max_exact)).astype(np.int32)
        large = np.minimum(large, REL_BUCKETS - 1)
        out.append(np.where(n < max_exact, n, large))
    assert np.array_equal(out[0], out[1])
    bucket = out[1]
    assert np.all(np.diff(bucket) >= 0)
    thr = [0]
    for b in range(1, REL_BUCKETS):
        idx = np.nonzero(bucket >= b)[0]
        thr.append(int(idx[0]) if idx.size else REL_MAX_DIST + 1)
    return tuple(thr)


_THR = _bucket_thresholds()


def _bias_of_dist(dist, tab_ref, head):
    val = jnp.full(dist.shape, tab_ref[head, 0], F32)
    for b in range(1, REL_BUCKETS):
        val = jnp.where(dist >= _THR[b], tab_ref[head, b], val)
    return val


def _sigmoid(x):
    return 1.0 / (1.0 + jnp.exp(-x))


def _bf(x):
    return x.astype(BF16)


def _dot(a, b, **kw):
    return jnp.dot(a, b, preferred_element_type=F32, **kw)


def _dot_nt(a, b, **kw):
    return lax.dot_general(a, b, (((1,), (1,)), ((), ())), preferred_element_type=F32, **kw)


def _bmm(a, b):
    return jnp.einsum('nij,njk->nik', _bf(a), _bf(b), preferred_element_type=F32)


def _bmm_nt(a, b):
    return jnp.einsum('nid,njd->nij', _bf(a), _bf(b), preferred_element_type=F32)


def _params(*sem):
    return pltpu.CompilerParams(dimension_semantics=sem, vmem_limit_bytes=VMEM_LIMIT)


def _norm_matmul_kernel(x_ref, g_ref, w_ref, o_ref, h_ref):
    @pl.when(pl.program_id(1) == 0)
    def _():
        x = x_ref[...]
        ms = jnp.mean(x * x, axis=-1, keepdims=True)
        h_ref[...] = (x * lax.rsqrt(ms + EPS) * g_ref[...]).astype(BF16)

    o_ref[...] = _dot_nt(h_ref[...], w_ref[...]).astype(o_ref.dtype)


def _norm_matmul(x, g, w_t, layer, tm, tn, out_dtype):
    m, d = x.shape
    n = w_t.shape[1]
    return pl.pallas_call(
        _norm_matmul_kernel,
        grid=(m // tm, n // tn),
        in_specs=[pl.BlockSpec((tm, d), lambda i, j: (i, 0)),
                  pl.BlockSpec((1, d), lambda i, j: (0, 0)),
                  pl.BlockSpec((None, tn, d), lambda i, j: (layer, j, 0))],
        out_specs=pl.BlockSpec((tm, tn), lambda i, j: (i, j)),
        out_shape=jax.ShapeDtypeStruct((m, n), out_dtype),
        scratch_shapes=[pltpu.VMEM((tm, d), BF16)],
        compiler_params=_params("parallel", "arbitrary"),
        name="in_proj",
    )(x, g.reshape(1, d), w_t)


def _out_proj_kernel(a_ref, b_ref, c_ref, wa_ref, wb_ref, wc_ref, x_ref, o_ref):
    acc = _dot(a_ref[...], wa_ref[...])
    acc = acc + _dot(b_ref[...], wb_ref[...])
    acc = acc + _dot(c_ref[...], wc_ref[...])
    o_ref[...] = x_ref[...] + acc


def _out_proj(o_dn, o_dil, o_nsa, w_out, x, tm):
    m, d = x.shape
    ka, kb, kc = o_dn.shape[1], o_dil.shape[1], o_nsa.shape[1]
    wa, wb, wc = w_out[:ka], w_out[ka:ka + kb], w_out[ka + kb:]
    row = lambda i: (i, 0)
    full = lambda i: (0, 0)
    return pl.pallas_call(
        _out_proj_kernel,
        grid=(m // tm,),
        in_specs=[pl.BlockSpec((tm, ka), row), pl.BlockSpec((tm, kb), row), pl.BlockSpec((tm, kc), row),
                  pl.BlockSpec((ka, d), full), pl.BlockSpec((kb, d), full), pl.BlockSpec((kc, d), full),
                  pl.BlockSpec((tm, d), row)],
        out_specs=pl.BlockSpec((tm, d), row),
        out_shape=jax.ShapeDtypeStruct((m, d), F32),
        compiler_params=_params("parallel"),
        name="out_proj",
    )(o_dn, o_dil, o_nsa, wa, wb, wc, x)


def _ffn_up_kernel(x_ref, xp_ref, g_ref, wg_ref, wv_ref, cg_ref, cv_ref, o_ref, h_ref, *, tiles_per_seq):
    tm = x_ref.shape[0]
    halo = xp_ref.shape[0]
    tn = o_ref.shape[1]

    @pl.when(pl.program_id(1) == 0)
    def _():
        def norm(x):
            ms = jnp.mean(x * x, axis=-1, keepdims=True)
            return (x * lax.rsqrt(ms + EPS) * g_ref[...]).astype(BF16)
        first = (pl.program_id(0) % tiles_per_seq) == 0
        hp = norm(xp_ref[...])
        h_ref[0:halo, :] = jnp.where(first, jnp.zeros_like(hp), hp)
        h_ref[halo:halo + tm, :] = norm(x_ref[...])

    def conv(h, w_ref, c_ref):
        u = _dot(h, w_ref[...])
        y = u * c_ref[2:3, :]
        y = y + pltpu.roll(u, 1, axis=0) * c_ref[1:2, :]
        y = y + pltpu.roll(u, 2, axis=0) * c_ref[0:1, :]
        return y[halo:, :]

    rows = tm // FFN_ROW_CHUNKS
    for s in range(FFN_ROW_CHUNKS):
        h = h_ref[s * rows:s * rows + halo + rows, :]
        gate = conv(h, wg_ref, cg_ref)
        val = conv(h, wv_ref, cv_ref)
        o_ref[s * rows:(s + 1) * rows, :] = (gate * _sigmoid(gate) * val).astype(o_ref.dtype)


def _ffn_up(x, g, w_up, conv_w, layer, seq, tm, tn):
    m, d = x.shape
    f = w_up.shape[2] // 2
    taps = conv_w.shape[1]
    halo = 8
    nf = f // tn
    kern = functools.partial(_ffn_up_kernel, tiles_per_seq=seq // tm)
    return pl.pallas_call(
        kern,
        grid=(m // tm, nf),
        in_specs=[pl.BlockSpec((tm, d), lambda i, j: (i, 0)),
                  pl.BlockSpec((halo, d), lambda i, j: (jnp.maximum(i * (tm // halo) - 1, 0), 0)),
                  pl.BlockSpec((1, d), lambda i, j: (0, 0)),
                  pl.BlockSpec((None, d, tn), lambda i, j: (layer, 0, j)),
                  pl.BlockSpec((None, d, tn), lambda i, j: (layer, 0, j + nf)),
                  pl.BlockSpec((None, taps, tn), lambda i, j: (layer, 0, j)),
                  pl.BlockSpec((None, taps, tn), lambda i, j: (layer, 0, j + nf))],
        out_specs=pl.BlockSpec((tm, tn), lambda i, j: (i, j)),
        out_shape=jax.ShapeDtypeStruct((m, f), BF16),
        scratch_shapes=[pltpu.VMEM((tm + halo, d), BF16)],
        compiler_params=_params("parallel", "arbitrary"),
        name="ffn_up",
    )(x, x, g.reshape(1, d), w_up, w_up, conv_w, conv_w)


def _ffn_down_kernel(a_ref, w_ref, x_ref, o_ref):
    o_ref[...] = x_ref[...] + _dot(a_ref[...], w_ref[...])


def _ffn_down(act, w_down, layer, x, tm, tn):
    m, f = act.shape
    d = w_down.shape[2]
    return pl.pallas_call(
        _ffn_down_kernel,
        grid=(d // tn, m // tm),
        in_specs=[pl.BlockSpec((tm, f), lambda j, i: (i, 0)),
                  pl.BlockSpec((None, f, tn), lambda j, i: (layer, 0, j)),
                  pl.BlockSpec((tm, tn), lambda j, i: (i, j))],
        out_specs=pl.BlockSpec((tm, tn), lambda j, i: (i, j)),
        out_shape=jax.ShapeDtypeStruct((m, d), F32),
        compiler_params=_params("parallel", "parallel"),
        name="ffn_down",
    )(act, w_down, x)


def _rmsnorm_kernel(x_ref, g_ref, o_ref):
    x = x_ref[...]
    ms = jnp.mean(x * x, axis=-1, keepdims=True)
    o_ref[...] = x * lax.rsqrt(ms + EPS) * g_ref[...]


def _rmsnorm(x, g, tm):
    m, d = x.shape
    return pl.pallas_call(
        _rmsnorm_kernel,
        grid=(m // tm,),
        in_specs=[pl.BlockSpec((tm, d), lambda i: (i, 0)), pl.BlockSpec((1, d), lambda i: (0, 0))],
        out_specs=pl.BlockSpec((tm, d), lambda i: (i, 0)),
        out_shape=jax.ShapeDtypeStruct((m, d), F32),
        compiler_params=_params("parallel"),
        name="final_norm",
    )(x, g.reshape(1, d))


def _dn_prepare(h, ls, alog_ref, dtb_ref, q_ref, k_ref, v_ref, s_ref, cq_ref, ck_ref, cv_ref,
                lhs_s, n_s, op_s, egl_s):
    seq = q_ref.shape[0]
    c = DN_CHUNK
    nc = seq // c

    def conv_silu(x_ref, c_ref):
        taps = c_ref.shape[0]

        def conv(x, mask_rows):
            acc = x * c_ref[taps - 1:taps, ls]
            for s in range(1, taps):
                xs = pltpu.roll(x, s, axis=0)
                if mask_rows is not None:
                    xs = jnp.where(mask_rows >= s, xs, 0.0)
                acc = acc + xs * c_ref[taps - 1 - s:taps - s, ls]
            return acc

        x = x_ref[:, ls]
        first = conv(x[0:8, :], lax.broadcasted_iota(jnp.int32, (8, 1), 0))
        acc = jnp.concatenate([first, conv(x, None)[8:, :]], axis=0)
        return acc * _sigmoid(acc)

    def l2norm(x):
        return x * lax.rsqrt(jnp.sum(x * x, axis=-1, keepdims=True) + EPS)

    q = l2norm(conv_silu(q_ref, cq_ref)) * (HEAD_DIM ** -0.5)
    k = l2norm(conv_silu(k_ref, ck_ref))
    v = conv_silu(v_ref, cv_ref)

    lane = lax.broadcasted_iota(jnp.int32, (1, LANES), 1)
    sm = s_ref[...]
    b_col = jnp.sum(jnp.where(lane == LANE_B + h, sm, 0.0), axis=-1, keepdims=True)
    a_col = jnp.sum(jnp.where(lane == LANE_A + h, sm, 0.0), axis=-1, keepdims=True)
    beta = _sigmoid(b_col)
    ap = a_col + dtb_ref[h]
    softplus = jnp.maximum(ap, 0.0) + jnp.log1p(jnp.exp(-jnp.abs(ap)))
    neg_rate = -jnp.exp(jnp.full((1, 1), alog_ref[h], F32))
    g = jnp.broadcast_to(neg_rate * softplus, (seq, LANES))

    ii = lax.broadcasted_iota(jnp.int32, (c, c), 0)
    jj = lax.broadcasted_iota(jnp.int32, (c, c), 1)
    tri = jnp.broadcast_to(jnp.where(ii >= jj, 1.0, 0.0).astype(F32), (nc, c, c))
    gc3 = jnp.einsum('nij,njk->nik', tri, g.reshape(nc, c, LANES),
                     preferred_element_type=F32, precision=HIGHEST)
    gc = gc3.reshape(seq, LANES)
    egc = jnp.exp(gc)
    glast3 = jnp.broadcast_to(gc3[:, c - 1:c, :], (nc, c, LANES))
    kb = k * beta
    k3 = k.reshape(nc, c, HEAD_DIM)
    kdt = jnp.swapaxes(k3 * jnp.exp(glast3 - gc3), 1, 2)
    egl_s[...] = jnp.exp(glast3[:, 0:8, :])

    gcol = gc3[:, :, 0:c]
    grow = jnp.swapaxes(gc3, 1, 2)[:, 0:c, :]
    decay = jnp.where(ii >= jj, jnp.exp(jnp.minimum(gcol - grow, 0.0)), 0.0)

    attn = _bmm_nt(q.reshape(nc, c, HEAD_DIM), k3) * decay
    a_mat = jnp.where(ii > jj, _bmm_nt(kb.reshape(nc, c, HEAD_DIM), k3) * decay, 0.0)

    same = (ii & -DN_SUB) == (jj & -DN_SUB)
    eye = jnp.where(ii == jj, 1.0, 0.0).astype(F32)
    x1 = jnp.where(same, -a_mat, 0.0)
    a_off = jnp.where(same, 0.0, a_mat)
    t_d = eye + x1
    xp = x1
    p = 2
    while p < DN_SUB:
        xp = _bmm(xp, xp)
        t_d = t_d + _bmm(t_d, xp)
        p *= 2
    y1 = -_bmm(t_d, a_off)
    t_o = eye + y1
    yp = y1
    p = 2
    while p < c // DN_SUB:
        yp = _bmm(yp, yp)
        t_o = t_o + _bmm(t_o, yp)
        p *= 2
    t_mat = _bmm(t_o, t_d)

    u = _bmm(t_mat, (v * beta).reshape(nc, c, HEAD_DIM))
    w = _bmm(t_mat, (kb * egc).reshape(nc, c, HEAD_DIM))

    lhs_s[:, 0:HEAD_DIM, :] = _bf(-_bmm(kdt, w))
    lhs_s[:, HEAD_DIM:HEAD_DIM + c, :] = _bf((q * egc).reshape(nc, c, HEAD_DIM) - _bmm(attn, w))
    n_s[...] = _bmm(kdt, u)
    op_s[...] = _bmm(attn, u)


def _dn_kernel(alog_ref, dtb_ref, q_ref, k_ref, v_ref, z_ref, s_ref, cq_ref, ck_ref, cv_ref, ng_ref,
               o_ref, lhs_s, n_s, op_s, egl_s, o_s):
    hs = DN_HEADS_PER_STEP
    seq = q_ref.shape[0]
    c = DN_CHUNK
    lanes = [slice(u * HEAD_DIM, (u + 1) * HEAD_DIM) for u in range(hs)]
    for u in range(hs):
        _dn_prepare(pl.program_id(1) * hs + u, lanes[u], alog_ref, dtb_ref, q_ref, k_ref, v_ref, s_ref,
                    cq_ref, ck_ref, cv_ref, lhs_s.at[u], n_s.at[u], op_s.at[u], egl_s.at[u])

    def chunk_step(n, states):
        rows = pl.ds(pl.multiple_of(n * c, c), c)
        out = []
        for u in range(hs):
            r = _dot(lhs_s[u, n], _bf(states[u]))
            o_s[u, rows, :] = r[HEAD_DIM:, :] + op_s[u, n]
            out.append(states[u] * egl_s[u, n][0:1, :] + r[0:HEAD_DIM, :] + n_s[u, n])
        return tuple(out)

    lax.fori_loop(0, seq // c, chunk_step, tuple(jnp.zeros((HEAD_DIM, HEAD_DIM), F32) for _ in range(hs)))

    for u in range(hs):
        o = o_s[u]
        y = o * lax.rsqrt(jnp.mean(o * o, axis=-1, keepdims=True) + EPS) * ng_ref[...]
        z = z_ref[:, lanes[u]]
        o_ref[:, lanes[u]] = (y * (z * _sigmoid(z))).astype(o_ref.dtype)


def _deltanet(proj, dn_conv, a_log, dt_bias, norm_g):
    bsz, seq, _ = proj.shape
    nc = seq // DN_CHUNK
    hs = DN_HEADS_PER_STEP
    width = hs * HEAD_DIM
    assert DN_HEADS % hs == 0 and all(cb % hs == 0 for cb in (CB_DN_Q, CB_DN_K, CB_DN_V, CB_DN_Z))
    col = lambda off: pl.BlockSpec((None, seq, width), lambda b, p: (b, 0, off // hs + p))
    cw = lambda off: pl.BlockSpec((dn_conv.shape[0], width), lambda b, p: (0, off // hs + p))
    smem = pl.BlockSpec(memory_space=pltpu.SMEM)
    return pl.pallas_call(
        _dn_kernel,
        grid=(bsz, DN_HEADS // hs),
        in_specs=[smem, smem, col(CB_DN_Q), col(CB_DN_K), col(CB_DN_V), col(CB_DN_Z),
                  pl.BlockSpec((None, seq, LANES), lambda b, p: (b, 0, CB_SMALL)),
                  cw(CB_DN_Q), cw(CB_DN_K), cw(CB_DN_V),
                  pl.BlockSpec((1, HEAD_DIM), lambda b, p: (0, 0))],
        out_specs=pl.BlockSpec((None, seq, width), lambda b, p: (b, 0, p)),
        out_shape=jax.ShapeDtypeStruct((bsz, seq, DN_HEADS * HEAD_DIM), BF16),
        scratch_shapes=[pltpu.VMEM((hs, nc, HEAD_DIM + DN_CHUNK, HEAD_DIM), BF16),
                        pltpu.VMEM((hs, nc, HEAD_DIM, HEAD_DIM), F32),
                        pltpu.VMEM((hs, nc, DN_CHUNK, HEAD_DIM), F32),
                        pltpu.VMEM((hs, nc, 8, LANES), F32),
                        pltpu.VMEM((hs, seq, HEAD_DIM), F32)],
        compiler_params=_params("parallel", "parallel"),
        name="deltanet",
    )(a_log, dt_bias, proj, proj, proj, proj, proj, dn_conv, dn_conv, dn_conv, norm_g.reshape(1, HEAD_DIM))


def _dil_kernel(tab_ref, q_ref, k_ref, v_ref, o_ref, tile_s, perm_s, back_s, nat_s):
    head = pl.program_id(0)
    seq = q_ref.shape[0]
    qb = Q_BLOCK
    scale = HEAD_DIM ** -0.5

    @pl.when(pl.program_id(1) == 0)
    def _():
        r = lax.broadcasted_iota(jnp.int32, (qb, 2 * qb), 0)
        cidx = lax.broadcasted_iota(jnp.int32, (qb, 2 * qb), 1)
        sub = qb + r - cidx
        for pat, (window, dil) in enumerate(DIL_PATTERNS):
            valid = (sub >= 0) & (sub <= window // dil)
            tile_s[pat] = jnp.where(valid, _bias_of_dist(jnp.maximum(sub, 0) * dil, tab_ref, head), NEG)

    st = DIL_PATTERNS[1][1]
    nb = seq // qb
    grp = seq // st
    per_class = grp // qb
    assert [d for _, d in DIL_PATTERNS] == [1, st, st * st] and all(w // d == qb for w, d in DIL_PATTERNS)
    assert grp // st == qb and per_class & (per_class - 1) == 0

    def level1_rows(rho):
        return pl.ds(rho, grp, stride=st), slice(rho * grp, (rho + 1) * grp)

    def level2_rows(c):
        rho, sigma = divmod(c, st)
        return pl.ds(rho * grp + sigma, qb, stride=st), slice(c * qb, (c + 1) * qb)

    for n, ref in enumerate((q_ref, k_ref, v_ref)):
        for rho in range(st):
            walk, dense = level1_rows(rho)
            perm_s[0, n, dense, :] = ref[walk, :]
        for c in range(st * st):
            walk, dense = level2_rows(c)
            perm_s[1, n, dense, :] = perm_s[0, n, walk, :]

    unit = lax.broadcasted_iota(jnp.int32, (nb, 1, 1), 0)

    def attend(pat, q, k, v, has_prev):
        q3 = _bf(q * scale).reshape(nb, qb, HEAD_DIM)
        k3 = _bf(k).reshape(nb, qb, HEAD_DIM)
        v3 = _bf(v).reshape(nb, qb, HEAD_DIM)
        tile = tile_s[pat]
        logits = [_bmm_nt(q3, k3) + tile[:, qb:]]
        values = [v3]
        if has_prev is not None:
            shift = lambda x: jnp.concatenate([jnp.zeros_like(x[:1]), x[:-1]], axis=0)
            logits.append(jnp.where(has_prev, _bmm_nt(q3, shift(k3)) + tile[:, :qb], NEG))
            values.append(shift(v3))
        m = functools.reduce(jnp.maximum, [jnp.max(s, axis=-1, keepdims=True) for s in logits])
        probs = [jnp.exp(s - m) for s in logits]
        l = sum(jnp.sum(p, axis=-1, keepdims=True) for p in probs)
        num = sum(_bmm(p, v_) for p, v_ in zip(probs, values))
        wide = lambda x: jnp.broadcast_to(x, (nb, qb, LANES)).reshape(seq, LANES)
        return num.reshape(seq, HEAD_DIM), wide(m), wide(l)

    res0 = attend(0, q_ref[...], k_ref[...], v_ref[...], unit >= 1)
    res1 = attend(1, perm_s[0, 0], perm_s[0, 1], perm_s[0, 2], (unit & (per_class - 1)) != 0)
    res2 = attend(2, perm_s[1, 0], perm_s[1, 1], perm_s[1, 2], None)

    for n in range(3):
        for c in range(st * st):
            walk, dense = level2_rows(c)
            back_s[n, walk, :] = res2[n][dense, :]
        for rho in range(st):
            walk, dense = level1_rows(rho)
            nat_s[1, n, walk, :] = back_s[n, dense, :]
            nat_s[0, n, walk, :] = res1[n][dense, :]

    parts = [res0, tuple(nat_s[0, n] for n in range(3)), tuple(nat_s[1, n] for n in range(3))]
    m_all = functools.reduce(jnp.maximum, [part[1] for part in parts])
    num = jnp.zeros((seq, HEAD_DIM), F32)
    den = jnp.zeros((seq, LANES), F32)
    for part_num, part_m, part_l in parts:
        wgt = jnp.exp(part_m - m_all)
        num = num + part_num * wgt
        den = den + part_l * wgt
    o_ref[...] = (num / den).astype(o_ref.dtype)


def _dilated(proj, dil_tab):
    bsz, seq, _ = proj.shape
    col = lambda off: pl.BlockSpec((None, seq, LANES), lambda h, b: (b, 0, off + h))
    return pl.pallas_call(
        _dil_kernel,
        grid=(DIL_HEADS, bsz),
        in_specs=[pl.BlockSpec(memory_space=pltpu.SMEM), col(CB_DIL_Q), col(CB_DIL_K), col(CB_DIL_V)],
        out_specs=pl.BlockSpec((None, seq, HEAD_DIM), lambda h, b: (b, 0, h)),
        out_shape=jax.ShapeDtypeStruct((bsz, seq, DIL_HEADS * HEAD_DIM), BF16),
        scratch_shapes=[pltpu.VMEM((len(DIL_PATTERNS), Q_BLOCK, 2 * Q_BLOCK), F32),
                        pltpu.VMEM((2, 3, seq, HEAD_DIM), F32),
                        pltpu.VMEM((3, seq, LANES), F32),
                        pltpu.VMEM((2, 3, seq, LANES), F32)],
        compiler_params=_params("arbitrary", "arbitrary"),
        name="dilated",
    )(dil_tab, proj, proj, proj)


def _cmp_kernel(x_ref, pos_ref, w1_ref, w2_ref, o_ref):
    nblk = o_ref.shape[0]
    half = CMP_LEN // 2
    ha = jnp.zeros((nblk, CMP_HIDDEN), F32)
    hb = jnp.zeros((nblk, CMP_HIDDEN), F32)
    for l in range(half):
        xl = x_ref[pl.ds(l, nblk, stride=CMP_STRIDE), :]
        wa = w1_ref[l * HEAD_DIM:(l + 1) * HEAD_DIM, :]
        wb = w1_ref[(half + l) * HEAD_DIM:(half + l + 1) * HEAD_DIM, :]
        ha = ha + _dot((xl + pos_ref[l:l + 1, :]).astype(BF16), wa)
        hb = hb + _dot((xl + pos_ref[half + l:half + l + 1, :]).astype(BF16), wb)
    hmid = ha + pltpu.roll(hb, nblk - 1, axis=0)
    hmid = 0.5 * hmid * (1.0 + jnp.tanh(math.sqrt(2.0 / math.pi) * (hmid + 0.044715 * hmid * hmid * hmid)))
    out = _dot(hmid.astype(BF16), w2_ref[...])
    rowi = lax.broadcasted_iota(jnp.int32, out.shape, 0)
    o_ref[...] = jnp.where(rowi < nblk - 1, out, 0.0)


def _compress(proj, cmp_pos, w1, w2):
    bsz, seq, _ = proj.shape
    nblk = seq // CMP_STRIDE
    return pl.pallas_call(
        _cmp_kernel,
        grid=(2, bsz, NSA_GROUPS),
        in_specs=[pl.BlockSpec((None, seq, LANES), lambda i, b, g: (b, 0, CB_NSA_KV + i * NSA_GROUPS + g)),
                  pl.BlockSpec((None, CMP_LEN, HEAD_DIM), lambda i, b, g: (i, 0, 0)),
                  pl.BlockSpec((None, CMP_LEN * HEAD_DIM, CMP_HIDDEN), lambda i, b, g: (i, 0, 0)),
                  pl.BlockSpec((None, CMP_HIDDEN, HEAD_DIM), lambda i, b, g: (i, 0, 0))],
        out_specs=pl.BlockSpec((None, None, None, nblk, HEAD_DIM), lambda i, b, g: (b, g, i, 0, 0)),
        out_shape=jax.ShapeDtypeStruct((bsz, NSA_GROUPS, 2, nblk, HEAD_DIM), F32),
        compiler_params=_params("parallel", "parallel", "parallel"),
        name="nsa_compress",
    )(proj, cmp_pos, w1, w2)


def _nsa_kernel(tab_ref, q0_ref, q1_ref, q2_ref, ks_ref, vs_ref, kw_ref, vw_ref, cmp_ref, s_ref,
                o_ref, strip_s, cbias_s, wtile_s, q_s, kv_s, oslc_s):
    grp = pl.program_id(0)
    step = pl.program_id(2)
    seq = q0_ref.shape[0]
    qb = Q_BLOCK
    nsub = o_ref.shape[0] // qb
    nqb = seq // qb
    n_slc = seq // SLC_BLOCK
    ncmp = cmp_ref.shape[1]
    nwin = WIN // qb
    win_keys = WIN + qb
    scale = HEAD_DIM ** -0.5
    q_refs = (q0_ref, q1_ref, q2_ref)
    head_rows = lambda x, rr: x[rr * qb:(rr + 1) * qb, :]

    @pl.when(step == 0)
    def _():
        for rr in range(NSA_REP):
            q_s[rr] = (q_refs[rr][...] * scale).astype(BF16)
        for n, ref in enumerate((ks_ref, vs_ref, kw_ref, vw_ref)):
            kv_s[n] = ref[...].astype(BF16)

    @pl.when(jnp.logical_and(pl.program_id(1) == 0, step == 0))
    def _():
        for rr in range(NSA_REP):
            head = DIL_HEADS + grp * NSA_REP + rr
            shape = strip_s.shape[1:]
            dist = (lax.broadcasted_iota(jnp.int32, shape, 0) - lax.broadcasted_iota(jnp.int32, shape, 1)
                    + (seq - qb))
            strip_s[rr] = jnp.where(dist >= 0, _bias_of_dist(jnp.maximum(dist, 0), tab_ref, head), NEG)
            shape = cbias_s.shape[1:]
            dist = (lax.broadcasted_iota(jnp.int32, shape, 0)
                    - (lax.broadcasted_iota(jnp.int32, shape, 1) * CMP_STRIDE + CMP_LEN - 1))
            cbias_s[rr] = jnp.where(dist >= 0, _bias_of_dist(jnp.maximum(dist, 0), tab_ref, head), NEG)
            shape = wtile_s.shape[1:]
            dist = lax.broadcasted_iota(jnp.int32, shape, 0) + WIN - lax.broadcasted_iota(jnp.int32, shape, 1)
            wtile_s[rr] = jnp.where((dist >= 0) & (dist < WIN),
                                    _bias_of_dist(jnp.clip(dist, 0, WIN), tab_ref, head), NEG)

    lane = lax.broadcasted_iota(jnp.int32, (1, LANES), 1)
    oj = lax.broadcasted_iota(jnp.int32, (n_slc, ncmp), 0) * SLC_BLOCK
    oc = lax.broadcasted_iota(jnp.int32, (n_slc, ncmp), 1) * CMP_STRIDE
    overlap_t = (jnp.maximum(jnp.minimum(oc + CMP_LEN, oj + SLC_BLOCK) - jnp.maximum(oc, oj), 0)
                 .astype(F32) / CMP_STRIDE)
    jrow = lax.broadcasted_iota(jnp.int32, (n_slc, qb), 0)
    tcol = lax.broadcasted_iota(jnp.int32, (n_slc, qb), 1)

    def softmax_rows(s):
        m = jnp.max(s, axis=-1, keepdims=True)
        p = jnp.exp(s - m)
        return p, jnp.sum(p, axis=-1, keepdims=True)

    def front(i):
        rows = pl.ds(pl.multiple_of(i * qb, qb), qb)
        q3 = jnp.concatenate([q_s[rr, rows, :] for rr in range(NSA_REP)], axis=0)
        sm = _sigmoid(s_ref[rows, :])

        def gate(rr, branch):
            ln = LANE_GATE + (grp * NSA_REP + rr) * 3 + branch
            return jnp.sum(jnp.where(lane == ln, sm, 0.0), axis=-1, keepdims=True)

        s_all = _dot_nt(q3, _bf(cmp_ref[0]))
        p_sum = jnp.zeros((qb, ncmp), F32)
        probs = []
        for rr in range(NSA_REP):
            cb = cbias_s[rr, rows, :]
            p, l = softmax_rows(head_rows(s_all, rr) + cb)
            p = jnp.where(cb > 0.5 * NEG, p, 0.0)
            p = p / jnp.maximum(jnp.sum(p, axis=-1, keepdims=True), 1e-30)
            p_sum = p_sum + p
            probs.append(_bf(p))
        o_cmp = _dot(jnp.concatenate(probs, axis=0), _bf(cmp_ref[1]))
        acc = [gate(rr, 0) * head_rows(o_cmp, rr) for rr in range(NSA_REP)]

        imp = _dot_nt(overlap_t, p_sum, precision=HIGHEST)
        cur = jnp.right_shift(i * qb + tcol, SLC_SHIFT)
        forced = (jrow == 0) | ((jrow <= cur) & (jrow > cur - 2))
        imp = jnp.where(forced, jnp.inf, jnp.where(jrow <= cur, imp, -jnp.inf))
        cnt = jnp.zeros((n_slc, qb), F32)
        for j2 in range(n_slc):
            other = imp[j2:j2 + 1, :]
            ge = jnp.where(other >= imp, 1.0, 0.0)
            gt = jnp.where(other > imp, 1.0, 0.0)
            cnt = cnt + jnp.where(jrow > j2, ge, gt)
        sel_t = jnp.where((cnt < SLC_TOPK) & (jrow <= cur), 1.0, 0.0)
        sel_t = jnp.concatenate([sel_t, jnp.zeros((LANES - n_slc, qb), F32)], axis=0)
        sel = _bf(sel_t.T)

        j0 = jnp.maximum(i - nwin, 0)
        krows = pl.ds(pl.multiple_of(j0 * qb, qb), win_keys)
        tile_start = pl.multiple_of((j0 - i + nwin) * qb, qb)
        s_all = _dot_nt(q3, kv_s[2, krows, :])
        probs, dens = [], []
        for rr in range(NSA_REP):
            p, l = softmax_rows(head_rows(s_all, rr) + wtile_s[rr, :, pl.ds(tile_start, win_keys)])
            probs.append(_bf(p))
            dens.append(l)
        pv = _dot(jnp.concatenate(probs, axis=0), kv_s[3, krows, :])
        acc = [acc[rr] + gate(rr, 2) * (head_rows(pv, rr) / dens[rr]) for rr in range(NSA_REP)]
        return q3, sel, acc, [gate(rr, 1) for rr in range(NSA_REP)]

    def selected(u, i, q3, sel, nk):
        strip_start = pl.multiple_of((nqb - 1 - i) * qb, qb)
        s_all = _dot_nt(q3, kv_s[0, 0:nk, :])
        ej = lax.broadcasted_iota(jnp.int32, (LANES, nk), 0)
        ep = lax.broadcasted_iota(jnp.int32, (LANES, nk), 1)
        expand = jnp.where(ej == jnp.right_shift(ep, SLC_SHIFT), 1.0, 0.0).astype(BF16)
        keep = _dot(sel, expand) > 0.5
        probs, dens = [], []
        for rr in range(NSA_REP):
            s = head_rows(s_all, rr) + strip_s[rr, :, pl.ds(strip_start, nk)]
            p, l = softmax_rows(jnp.where(keep, s, NEG))
            probs.append(_bf(p))
            dens.append(l)
        pv = _dot(jnp.concatenate(probs, axis=0), kv_s[1, 0:nk, :])
        for rr in range(NSA_REP):
            oslc_s[u, rr] = head_rows(pv, rr) / dens[rr]

    blocks = [step * nsub + u for u in range(nsub)]
    fronts = [front(i) for i in blocks]
    blocks_per_tile = SLC_KEY_TILE // qb
    assert blocks_per_tile % nsub == 0
    for kq in range(seq // SLC_KEY_TILE):
        @pl.when(blocks[0] // blocks_per_tile == kq)
        def _(kq=kq):
            for u, i in enumerate(blocks):
                selected(u, i, fronts[u][0], fronts[u][1], (kq + 1) * SLC_KEY_TILE)

    for u in range(nsub):
        _, _, acc, gate_slc = fronts[u]
        for rr in range(NSA_REP):
            out = acc[rr] + gate_slc[rr] * oslc_s[u, rr]
            o_ref[u * qb:(u + 1) * qb, rr * HEAD_DIM:(rr + 1) * HEAD_DIM] = out.astype(o_ref.dtype)


def _nsa(proj, cmp_kv, rel_bias):
    bsz, seq, _ = proj.shape
    ncmp = cmp_kv.shape[3]
    nqb = seq // Q_BLOCK
    strip_w = (nqb - 1) * Q_BLOCK + SLC_KEY_TILE
    wtile_w = WIN + WIN + Q_BLOCK
    col = lambda fn: pl.BlockSpec((None, seq, LANES), lambda g, b, i: (b, 0, fn(g)))
    kv = lambda branch, which: col(lambda g: CB_NSA_KV + (branch * 2 + which) * NSA_GROUPS + g)
    return pl.pallas_call(
        _nsa_kernel,
        grid=(NSA_GROUPS, bsz, nqb // NSA_BLOCKS_PER_STEP),
        in_specs=[pl.BlockSpec(memory_space=pltpu.SMEM),
                  col(lambda g: CB_NSA_Q + g * NSA_REP), col(lambda g: CB_NSA_Q + g * NSA_REP + 1),
                  col(lambda g: CB_NSA_Q + g * NSA_REP + 2),
                  kv(1, 0), kv(1, 1), kv(2, 0), kv(2, 1),
                  pl.BlockSpec((None, None, 2, ncmp, HEAD_DIM), lambda g, b, i: (b, g, 0, 0, 0)),
                  pl.BlockSpec((None, seq, LANES), lambda g, b, i: (b, 0, CB_SMALL))],
        out_specs=pl.BlockSpec((None, NSA_BLOCKS_PER_STEP * Q_BLOCK, NSA_REP * HEAD_DIM),
                               lambda g, b, i: (b, i, g)),
        out_shape=jax.ShapeDtypeStruct((bsz, seq, NSA_GROUPS * NSA_REP * HEAD_DIM), BF16),
        scratch_shapes=[pltpu.VMEM((NSA_REP, Q_BLOCK, strip_w), F32),
                        pltpu.VMEM((NSA_REP, seq, ncmp), F32),
                        pltpu.VMEM((NSA_REP, Q_BLOCK, wtile_w), F32),
                        pltpu.VMEM((NSA_REP, seq, HEAD_DIM), BF16),
                        pltpu.VMEM((4, seq, HEAD_DIM), BF16),
                        pltpu.VMEM((NSA_BLOCKS_PER_STEP, NSA_REP, Q_BLOCK, HEAD_DIM), F32)],
        compiler_params=_params("arbitrary", "arbitrary", "arbitrary"),
        name="nsa_attention",
    )(rel_bias, proj, proj, proj, proj, proj, proj, proj, cmp_kv, proj)


IN_WIDE_A = 3072
IN_SMALL_A = 12
IN_WIDE_B = 3840
IN_SMALL_B = 18


def _w_in_prep_kernel(a_ref, b_ref, c_ref, o_ref):
    j = pl.program_id(0)
    depth = o_ref.shape[0]
    blk = o_ref.shape[1]
    n_a = IN_WIDE_A // blk
    n_b = IN_WIDE_B // blk
    s = IN_SMALL_A

    def put(lo, hi, src_ref, src_lo):
        for l in range(depth):
            o_ref[l, lo:hi, :] = src_ref[src_lo:src_lo + hi - lo, l, :].astype(o_ref.dtype)

    @pl.when(j < n_a)
    def _():
        put(0, blk, a_ref, 0)

    @pl.when(jnp.logical_and(j >= n_a, j < n_a + n_b))
    def _():
        put(0, blk - s, a_ref, s)
        put(blk - s, blk, b_ref, 0)

    @pl.when(j == n_a + n_b)
    def _():
        put(0, s, c_ref, 0)
        put(s, s + IN_SMALL_B, a_ref, s)
        o_ref[:, s + IN_SMALL_B:, :] = jnp.zeros((depth, blk - s - IN_SMALL_B, o_ref.shape[2]), o_ref.dtype)

    @pl.when(j > n_a + n_b)
    def _():
        o_ref[...] = jnp.zeros(o_ref.shape, o_ref.dtype)


def _prep_w_in(w):
    depth, d, cols = w.shape
    blk = LANES
    assert cols == IN_WIDE_A + IN_SMALL_A + IN_WIDE_B + IN_SMALL_B
    wt = jnp.transpose(w, (2, 0, 1))
    last = (cols - 1) // blk
    nxt = 16
    assert IN_SMALL_A <= nxt and blk % nxt == 0
    last_nxt = (cols - 1) // nxt
    return pl.pallas_call(
        _w_in_prep_kernel,
        grid=(PROJ_COLS // blk,),
        in_specs=[pl.BlockSpec((blk, depth, d), lambda j: (jnp.minimum(j, last), 0, 0)),
                  pl.BlockSpec((nxt, depth, d), lambda j: (jnp.minimum((j + 1) * (blk // nxt), last_nxt), 0, 0)),
                  pl.BlockSpec((nxt, depth, d), lambda j: (IN_WIDE_A // nxt, 0, 0))],
        out_specs=pl.BlockSpec((depth, blk, d), lambda j: (0, j, 0)),
        out_shape=jax.ShapeDtypeStruct((depth, PROJ_COLS, d), BF16),
        compiler_params=_params("parallel"),
        name="w_in_prep",
    )(wt, wt, wt)


_DENSE_TILES = {
    "in_proj": (1024, 1024),
    "out_proj": 512,
    "ffn_up": (1024, 512),
    "ffn_down": (1024, 512),
    "final_norm": 512,
}


def kernel(x, norm1_g, w_in, dn_conv, dn_a_log, dn_dt_bias, dn_norm_g, cmp_pos, cmp_w1, cmp_w2, w_out,
           norm2_g, ffn_up, ffn_conv, ffn_down, rel_bias, final_g):
    bsz, seq, d = x.shape
    depth = w_in.shape[0]
    w_in_b = _prep_w_in(w_in)
    cmp_w1_b, cmp_w2_b = cmp_w1.astype(BF16), cmp_w2.astype(BF16)
    w_out_b, ffn_up_b, ffn_down_b = w_out.astype(BF16), ffn_up.astype(BF16), ffn_down.astype(BF16)
    xf = x.reshape(bsz * seq, d)
    t = _DENSE_TILES
    for l in range(depth):
        proj = _norm_matmul(xf, norm1_g[l], w_in_b, l, *t["in_proj"], F32).reshape(bsz, seq, PROJ_COLS)
        o_dn = _deltanet(proj, dn_conv[l], dn_a_log[l], dn_dt_bias[l], dn_norm_g[l])
        o_dil = _dilated(proj, rel_bias)
        cmp_kv = _compress(proj, cmp_pos[l], cmp_w1_b[l], cmp_w2_b[l])
        o_nsa = _nsa(proj, cmp_kv, rel_bias)
        xf = _out_proj(o_dn.reshape(bsz * seq, -1), o_dil.reshape(bsz * seq, -1), o_nsa.reshape(bsz * seq, -1),
                       w_out_b[l], xf, t["out_proj"])
        act = _ffn_up(xf, norm2_g[l], ffn_up_b, ffn_conv, l, seq, *t["ffn_up"])
        xf = _ffn_down(act, ffn_down_b, l, xf, *t["ffn_down"])
    return _rmsnorm(xf, final_g, t["final_norm"]).reshape(bsz, seq, d)
```

```python
import functools
import math

import numpy as np
import jax
import jax.numpy as jnp
from jax import lax
from jax.experimental import pallas as pl
from jax.experimental.pallas import tpu as pltpu

F32 = jnp.float32
BF16 = jnp.bfloat16
HIGHEST = lax.Precision.HIGHEST

LANES = 128
HEAD_DIM = 128
DN_HEADS = 6
DIL_HEADS = 4
NSA_GROUPS = 2
NSA_REP = 3
DN_CHUNK = 64
DN_SUB = 16
DN_HEADS_PER_STEP = 3
CMP_LEN = 32
CMP_STRIDE = 16
CMP_HIDDEN = 256
SLC_BLOCK = 64
SLC_SHIFT = 6
SLC_TOPK = 16
NSA_BLOCKS_PER_STEP = 4
WIN = 512
Q_BLOCK = 128
FFN_ROW_CHUNKS = 1
DIL_PATTERNS = ((128, 1), (512, 4), (2048, 16))
REL_BUCKETS = 32
REL_MAX_DIST = 2048
EPS = 1e-6
NEG = -1e30
VMEM_LIMIT = 56 * 1024 * 1024

CB_DN_Q, CB_DN_K, CB_DN_V, CB_DN_Z = 0, 6, 12, 18
CB_DIL_Q, CB_DIL_K, CB_DIL_V = 24, 28, 32
CB_NSA_Q = 36
CB_NSA_KV = 42
CB_SMALL = 54
PROJ_COLS = 7168
LANE_B, LANE_A, LANE_GATE = 0, 6, 12


def _bucket_thresholds():
    n = np.arange(0, REL_MAX_DIST + 1)
    max_exact = REL_BUCKETS // 2
    out = []
    for dt in (np.float32, np.float64):
        nf = np.maximum(n, 1).astype(dt)
        large = max_exact + (np.log(nf / dt(max_exact)) / dt(math.log(REL_MAX_DIST / max_exact))
                             * dt(REL_BUCKETS - max_exact)).astype(np.int32)
        large = np.minimum(large, REL_BUCKETS - 1)
        out.append(np.where(n < max_exact, n, large))
    assert np.array_equal(out[0], out[1])
    bucket = out[1]
    assert np.all(np.diff(bucket) >= 0)
    thr = [0]
    for b in range(1, REL_BUCKETS):
        idx = np.nonzero(bucket >= b)[0]
        thr.append(int(idx[0]) if idx.size else REL_MAX_DIST + 1)
    return tuple(thr)


_THR = _bucket_thresholds()


def _bias_of_dist(dist, tab_ref, head):
    val = jnp.full(dist.shape, tab_ref[head, 0], F32)
    for b in range(1, REL_BUCKETS):
        val = jnp.where(dist >= _THR[b], tab_ref[head, b], val)
    return val


def _sigmoid(x):
    return 1.0 / (1.0 + jnp.exp(-x))


def _bf(x):
    return x.astype(BF16)


def _dot(a, b, **kw):
    return jnp.dot(a, b, preferred_element_type=F32, **kw)


def _dot_nt(a, b, **kw):
    return lax.dot_general(a, b, (((1,), (1,)), ((), ())), preferred_element_type=F32, **kw)


def _bmm(a, b):
    return jnp.einsum('nij,njk->nik', _bf(a), _bf(b), preferred_element_type=F32)


def _bmm_nt(a, b):
    return jnp.einsum('nid,njd->nij', _bf(a), _bf(b), preferred_element_type=F32)


def _params(*sem):
    return pltpu.CompilerParams(dimension_semantics=sem, vmem_limit_bytes=VMEM_LIMIT)


def _norm_matmul_kernel(x_ref, g_ref, w_ref, o_ref, h_ref):
    @pl.when(pl.program_id(1) == 0)
    def _():
        x = x_ref[...]
        ms = jnp.mean(x * x, axis=-1, keepdims=True)
        h_ref[...] = (x * lax.rsqrt(ms + EPS) * g_ref[...]).astype(BF16)

    o_ref[...] = _dot_nt(h_ref[...], w_ref[...]).astype(o_ref.dtype)


def _norm_matmul(x, g, w_t, layer, tm, tn, out_dtype):
    m, d = x.shape
    n = w_t.shape[1]
    return pl.pallas_call(
        _norm_matmul_kernel,
        grid=(m // tm, n // tn),
        in_specs=[pl.BlockSpec((tm, d), lambda i, j: (i, 0)),
                  pl.BlockSpec((1, d), lambda i, j: (0, 0)),
                  pl.BlockSpec((None, tn, d), lambda i, j: (layer, j, 0))],
        out_specs=pl.BlockSpec((tm, tn), lambda i, j: (i, j)),
        out_shape=jax.ShapeDtypeStruct((m, n), out_dtype),
        scratch_shapes=[pltpu.VMEM((tm, d), BF16)],
        compiler_params=_params("parallel", "arbitrary"),
        name="in_proj",
    )(x, g.reshape(1, d), w_t)


def _out_proj_kernel(a_ref, b_ref, c_ref, wa_ref, wb_ref, wc_ref, x_ref, o_ref):
    acc = _dot(a_ref[...], wa_ref[...])
    acc = acc + _dot(b_ref[...], wb_ref[...])
    acc = acc + _dot(c_ref[...], wc_ref[...])
    o_ref[...] = x_ref[...] + acc


def _out_proj(o_dn, o_dil, o_nsa, w_out, x, tm):
    m, d = x.shape
    ka, kb, kc = o_dn.shape[1], o_dil.shape[1], o_nsa.shape[1]
    wa, wb, wc = w_out[:ka], w_out[ka:ka + kb], w_out[ka + kb:]
    row = lambda i: (i, 0)
    full = lambda i: (0, 0)
    return pl.pallas_call(
        _out_proj_kernel,
        grid=(m // tm,),
        in_specs=[pl.BlockSpec((tm, ka), row), pl.BlockSpec((tm, kb), row), pl.BlockSpec((tm, kc), row),
                  pl.BlockSpec((ka, d), full), pl.BlockSpec((kb, d), full), pl.BlockSpec((kc, d), full),
                  pl.BlockSpec((tm, d), row)],
        out_specs=pl.BlockSpec((tm, d), row),
        out_shape=jax.ShapeDtypeStruct((m, d), F32),
        compiler_params=_params("parallel"),
        name="out_proj",
    )(o_dn, o_dil, o_nsa, wa, wb, wc, x)


def _ffn_up_kernel(x_ref, xp_ref, g_ref, wg_ref, wv_ref, cg_ref, cv_ref, o_ref, h_ref, *, tiles_per_seq):
    tm = x_ref.shape[0]
    halo = xp_ref.shape[0]
    tn = o_ref.shape[1]

    @pl.when(pl.program_id(1) == 0)
    def _():
        def norm(x):
            ms = jnp.mean(x * x, axis=-1, keepdims=True)
            return (x * lax.rsqrt(ms + EPS) * g_ref[...]).astype(BF16)
        first = (pl.program_id(0) % tiles_per_seq) == 0
        hp = norm(xp_ref[...])
        h_ref[0:halo, :] = jnp.where(first, jnp.zeros_like(hp), hp)
        h_ref[halo:halo + tm, :] = norm(x_ref[...])

    def conv(h, w_ref, c_ref):
        u = _dot(h, w_ref[...])
        y = u * c_ref[2:3, :]
        y = y + pltpu.roll(u, 1, axis=0) * c_ref[1:2, :]
        y = y + pltpu.roll(u, 2, axis=0) * c_ref[0:1, :]
        return y[halo:, :]

    rows = tm // FFN_ROW_CHUNKS
    for s in range(FFN_ROW_CHUNKS):
        h = h_ref[s * rows:s * rows + halo + rows, :]
        gate = conv(h, wg_ref, cg_ref)
        val = conv(h, wv_ref, cv_ref)
        o_ref[s * rows:(s + 1) * rows, :] = (gate * _sigmoid(gate) * val).astype(o_ref.dtype)


def _ffn_up(x, g, w_up, conv_w, layer, seq, tm, tn):
    m, d = x.shape
    f = w_up.shape[2] // 2
    taps = conv_w.shape[1]
    halo = 8
    nf = f // tn
    kern = functools.partial(_ffn_up_kernel, tiles_per_seq=seq // tm)
    return pl.pallas_call(
        kern,
        grid=(m // tm, nf),
        in_specs=[pl.BlockSpec((tm, d), lambda i, j: (i, 0)),
                  pl.BlockSpec((halo, d), lambda i, j: (jnp.maximum(i * (tm // halo) - 1, 0), 0)),
                  pl.BlockSpec((1, d), lambda i, j: (0, 0)),
                  pl.BlockSpec((None, d, tn), lambda i, j: (layer, 0, j)),
                  pl.BlockSpec((None, d, tn), lambda i, j: (layer, 0, j + nf)),
                  pl.BlockSpec((None, taps, tn), lambda i, j: (layer, 0, j)),
                  pl.BlockSpec((None, taps, tn), lambda i, j: (layer, 0, j + nf))],
        out_specs=pl.BlockSpec((tm, tn), lambda i, j: (i, j)),
        out_shape=jax.ShapeDtypeStruct((m, f), BF16),
        scratch_shapes=[pltpu.VMEM((tm + halo, d), BF16)],
        compiler_params=_params("parallel", "arbitrary"),
        name="ffn_up",
    )(x, x, g.reshape(1, d), w_up, w_up, conv_w, conv_w)


def _ffn_down_kernel(a_ref, w_ref, x_ref, o_ref):
    o_ref[...] = x_ref[...] + _dot(a_ref[...], w_ref[...])


def _ffn_down(act, w_down, layer, x, tm, tn):
    m, f = act.shape
    d = w_down.shape[2]
    return pl.pallas_call(
        _ffn_down_kernel,
        grid=(d // tn, m // tm),
        in_specs=[pl.BlockSpec((tm, f), lambda j, i: (i, 0)),
                  pl.BlockSpec((None, f, tn), lambda j, i: (layer, 0, j)),
                  pl.BlockSpec((tm, tn), lambda j, i: (i, j))],
        out_specs=pl.BlockSpec((tm, tn), lambda j, i: (i, j)),
        out_shape=jax.ShapeDtypeStruct((m, d), F32),
        compiler_params=_params("parallel", "parallel"),
        name="ffn_down",
    )(act, w_down, x)


def _rmsnorm_kernel(x_ref, g_ref, o_ref):
    x = x_ref[...]
    ms = jnp.mean(x * x, axis=-1, keepdims=True)
    o_ref[...] = x * lax.rsqrt(ms + EPS) * g_ref[...]


def _rmsnorm(x, g, tm):
    m, d = x.shape
    return pl.pallas_call(
        _rmsnorm_kernel,
        grid=(m // tm,),
        in_specs=[pl.BlockSpec((tm, d), lambda i: (i, 0)), pl.BlockSpec((1, d), lambda i: (0, 0))],
        out_specs=pl.BlockSpec((tm, d), lambda i: (i, 0)),
        out_shape=jax.ShapeDtypeStruct((m, d), F32),
        compiler_params=_params("parallel"),
        name="final_norm",
    )(x, g.reshape(1, d))


def _dn_prepare(h, ls, alog_ref, dtb_ref, q_ref, k_ref, v_ref, s_ref, cq_ref, ck_ref, cv_ref,
                lhs_s, n_s, op_s, egl_s):
    seq = q_ref.shape[0]
    c = DN_CHUNK
    nc = seq // c

    def conv_silu(x_ref, c_ref):
        taps = c_ref.shape[0]

        def conv(x, mask_rows):
            acc = x * c_ref[taps - 1:taps, ls]
            for s in range(1, taps):
                xs = pltpu.roll(x, s, axis=0)
                if mask_rows is not None:
                    xs = jnp.where(mask_rows >= s, xs, 0.0)
                acc = acc + xs * c_ref[taps - 1 - s:taps - s, ls]
            return acc

        x = x_ref[:, ls]
        first = conv(x[0:8, :], lax.broadcasted_iota(jnp.int32, (8, 1), 0))
        acc = jnp.concatenate([first, conv(x, None)[8:, :]], axis=0)
        return acc * _sigmoid(acc)

    def l2norm(x):
        return x * lax.rsqrt(jnp.sum(x * x, axis=-1, keepdims=True) + EPS)

    q = l2norm(conv_silu(q_ref, cq_ref)) * (HEAD_DIM ** -0.5)
    k = l2norm(conv_silu(k_ref, ck_ref))
    v = conv_silu(v_ref, cv_ref)

    lane = lax.broadcasted_iota(jnp.int32, (1, LANES), 1)
    sm = s_ref[...]
    b_col = jnp.sum(jnp.where(lane == LANE_B + h, sm, 0.0), axis=-1, keepdims=True)
    a_col = jnp.sum(jnp.where(lane == LANE_A + h, sm, 0.0), axis=-1, keepdims=True)
    beta = _sigmoid(b_col)
    ap = a_col + dtb_ref[h]
    softplus = jnp.maximum(ap, 0.0) + jnp.log1p(jnp.exp(-jnp.abs(ap)))
    neg_rate = -jnp.exp(jnp.full((1, 1), alog_ref[h], F32))
    g = jnp.broadcast_to(neg_rate * softplus, (seq, LANES))

    ii = lax.broadcasted_iota(jnp.int32, (c, c), 0)
    jj = lax.broadcasted_iota(jnp.int32, (c, c), 1)
    tri = jnp.broadcast_to(jnp.where(ii >= jj, 1.0, 0.0).astype(F32), (nc, c, c))
    gc3 = jnp.einsum('nij,njk->nik', tri, g.reshape(nc, c, LANES),
                     preferred_element_type=F32, precision=HIGHEST)
    gc = gc3.reshape(seq, LANES)
    egc = jnp.exp(gc)
    glast3 = jnp.broadcast_to(gc3[:, c - 1:c, :], (nc, c, LANES))
    kb = k * beta
    k3 = k.reshape(nc, c, HEAD_DIM)
    kdt = jnp.swapaxes(k3 * jnp.exp(glast3 - gc3), 1, 2)
    egl_s[...] = jnp.exp(glast3[:, 0:8, :])

    gcol = gc3[:, :, 0:c]
    grow = jnp.swapaxes(gc3, 1, 2)[:, 0:c, :]
    decay = jnp.where(ii >= jj, jnp.exp(jnp.minimum(gcol - grow, 0.0)), 0.0)

    attn = _bmm_nt(q.reshape(nc, c, HEAD_DIM), k3) * decay
    a_mat = jnp.where(ii > jj, _bmm_nt(kb.reshape(nc, c, HEAD_DIM), k3) * decay, 0.0)

    same = (ii & -DN_SUB) == (jj & -DN_SUB)
    eye = jnp.where(ii == jj, 1.0, 0.0).astype(F32)
    x1 = jnp.where(same, -a_mat, 0.0)
    a_off = jnp.where(same, 0.0, a_mat)
    t_d = eye + x1
    xp = x1
    p = 2
    while p < DN_SUB:
        xp = _bmm(xp, xp)
        t_d = t_d + _bmm(t_d, xp)
        p *= 2
    y1 = -_bmm(t_d, a_off)
    t_o = eye + y1
    yp = y1
    p = 2
    while p < c // DN_SUB:
        yp = _bmm(yp, yp)
        t_o = t_o + _bmm(t_o, yp)
        p *= 2
    t_mat = _bmm(t_o, t_d)

    u = _bmm(t_mat, (v * beta).reshape(nc, c, HEAD_DIM))
    w = _bmm(t_mat, (kb * egc).reshape(nc, c, HEAD_DIM))

    lhs_s[:, 0:HEAD_DIM, :] = _bf(-_bmm(kdt, w))
    lhs_s[:, HEAD_DIM:HEAD_DIM + c, :] = _bf((q * egc).reshape(nc, c, HEAD_DIM) - _bmm(attn, w))
    n_s[...] = _bmm(kdt, u)
    op_s[...] = _bmm(attn, u)


def _dn_kernel(alog_ref, dtb_ref, q_ref, k_ref, v_ref, z_ref, s_ref, cq_ref, ck_ref, cv_ref, ng_ref,
               o_ref, lhs_s, n_s, op_s, egl_s, o_s):
    hs = DN_HEADS_PER_STEP
    seq = q_ref.shape[0]
    c = DN_CHUNK
    lanes = [slice(u * HEAD_DIM, (u + 1) * HEAD_DIM) for u in range(hs)]
    for u in range(hs):
        _dn_prepare(pl.program_id(1) * hs + u, lanes[u], alog_ref, dtb_ref, q_ref, k_ref, v_ref, s_ref,
                    cq_ref, ck_ref, cv_ref, lhs_s.at[u], n_s.at[u], op_s.at[u], egl_s.at[u])

    def chunk_step(n, states):
        rows = pl.ds(pl.multiple_of(n * c, c), c)
        out = []
        for u in range(hs):
            r = _dot(lhs_s[u, n], _bf(states[u]))
            o_s[u, rows, :] = r[HEAD_DIM:, :] + op_s[u, n]
            out.append(states[u] * egl_s[u, n][0:1, :] + r[0:HEAD_DIM, :] + n_s[u, n])
        return tuple(out)

    lax.fori_loop(0, seq // c, chunk_step, tuple(jnp.zeros((HEAD_DIM, HEAD_DIM), F32) for _ in range(hs)))

    for u in range(hs):
        o = o_s[u]
        y = o * lax.rsqrt(jnp.mean(o * o, axis=-1, keepdims=True) + EPS) * ng_ref[...]
        z = z_ref[:, lanes[u]]
        o_ref[:, lanes[u]] = (y * (z * _sigmoid(z))).astype(o_ref.dtype)


def _deltanet(proj, dn_conv, a_log, dt_bias, norm_g):
    bsz, seq, _ = proj.shape
    nc = seq // DN_CHUNK
    hs = DN_HEADS_PER_STEP
    width = hs * HEAD_DIM
    assert DN_HEADS % hs == 0 and all(cb % hs == 0 for cb in (CB_DN_Q, CB_DN_K, CB_DN_V, CB_DN_Z))
    col = lambda off: pl.BlockSpec((None, seq, width), lambda b, p: (b, 0, off // hs + p))
    cw = lambda off: pl.BlockSpec((dn_conv.shape[0], width), lambda b, p: (0, off // hs + p))
    smem = pl.BlockSpec(memory_space=pltpu.SMEM)
    return pl.pallas_call(
        _dn_kernel,
        grid=(bsz, DN_HEADS // hs),
        in_specs=[smem, smem, col(CB_DN_Q), col(CB_DN_K), col(CB_DN_V), col(CB_DN_Z),
                  pl.BlockSpec((None, seq, LANES), lambda b, p: (b, 0, CB_SMALL)),
                  cw(CB_DN_Q), cw(CB_DN_K), cw(CB_DN_V),
                  pl.BlockSpec((1, HEAD_DIM), lambda b, p: (0, 0))],
        out_specs=pl.BlockSpec((None, seq, width), lambda b, p: (b, 0, p)),
        out_shape=jax.ShapeDtypeStruct((bsz, seq, DN_HEADS * HEAD_DIM), BF16),
        scratch_shapes=[pltpu.VMEM((hs, nc, HEAD_DIM + DN_CHUNK, HEAD_DIM), BF16),
                        pltpu.VMEM((hs, nc, HEAD_DIM, HEAD_DIM), F32),
                        pltpu.VMEM((hs, nc, DN_CHUNK, HEAD_DIM), F32),
                        pltpu.VMEM((hs, nc, 8, LANES), F32),
                        pltpu.VMEM((hs, seq, HEAD_DIM), F32)],
        compiler_params=_params("parallel", "parallel"),
        name="deltanet",
    )(a_log, dt_bias, proj, proj, proj, proj, proj, dn_conv, dn_conv, dn_conv, norm_g.reshape(1, HEAD_DIM))


def _dil_kernel(tab_ref, q_ref, k_ref, v_ref, o_ref, tile_s, perm_s, back_s, nat_s):
    head = pl.program_id(0)
    seq = q_ref.shape[0]
    qb = Q_BLOCK
    scale = HEAD_DIM ** -0.5

    @pl.when(pl.program_id(1) == 0)
    def _():
        r = lax.broadcasted_iota(jnp.int32, (qb, 2 * qb), 0)
        cidx = lax.broadcasted_iota(jnp.int32, (qb, 2 * qb), 1)
        sub = qb + r - cidx
        for pat, (window, dil) in enumerate(DIL_PATTERNS):
            valid = (sub >= 0) & (sub <= window // dil)
            tile_s[pat] = jnp.where(valid, _bias_of_dist(jnp.maximum(sub, 0) * dil, tab_ref, head), NEG)

    st = DIL_PATTERNS[1][1]
    nb = seq // qb
    grp = seq // st
    per_class = grp // qb
    assert [d for _, d in DIL_PATTERNS] == [1, st, st * st] and all(w // d == qb for w, d in DIL_PATTERNS)
    assert grp // st == qb and per_class & (per_class - 1) == 0

    def level1_rows(rho):
        return pl.ds(rho, grp, stride=st), slice(rho * grp, (rho + 1) * grp)

    def level2_rows(c):
        rho, sigma = divmod(c, st)
        return pl.ds(rho * grp + sigma, qb, stride=st), slice(c * qb, (c + 1) * qb)

    for n, ref in enumerate((q_ref, k_ref, v_ref)):
        for rho in range(st):
            walk, dense = level1_rows(rho)
            perm_s[0, n, dense, :] = ref[walk, :]
        for c in range(st * st):
            walk, dense = level2_rows(c)
            perm_s[1, n, dense, :] = perm_s[0, n, walk, :]

    unit = lax.broadcasted_iota(jnp.int32, (nb, 1, 1), 0)

    def attend(pat, q, k, v, has_prev):
        q3 = _bf(q * scale).reshape(nb, qb, HEAD_DIM)
        k3 = _bf(k).reshape(nb, qb, HEAD_DIM)
        v3 = _bf(v).reshape(nb, qb, HEAD_DIM)
        tile = tile_s[pat]
        logits = [_bmm_nt(q3, k3) + tile[:, qb:]]
        values = [v3]
        if has_prev is not None:
            shift = lambda x: jnp.concatenate([jnp.zeros_like(x[:1]), x[:-1]], axis=0)
            logits.append(jnp.where(has_prev, _bmm_nt(q3, shift(k3)) + tile[:, :qb], NEG))
            values.append(shift(v3))
        m = functools.reduce(jnp.maximum, [jnp.max(s, axis=-1, keepdims=True) for s in logits])
        probs = [jnp.exp(s - m) for s in logits]
        l = sum(jnp.sum(p, axis=-1, keepdims=True) for p in probs)
        num = sum(_bmm(p, v_) for p, v_ in zip(probs, values))
        wide = lambda x: jnp.broadcast_to(x, (nb, qb, LANES)).reshape(seq, LANES)
        return num.reshape(seq, HEAD_DIM), wide(m), wide(l)

    res0 = attend(0, q_ref[...], k_ref[...], v_ref[...], unit >= 1)
    res1 = attend(1, perm_s[0, 0], perm_s[0, 1], perm_s[0, 2], (unit & (per_class - 1)) != 0)
    res2 = attend(2, perm_s[1, 0], perm_s[1, 1], perm_s[1, 2], None)

    for n in range(3):
        for c in range(st * st):
            walk, dense = level2_rows(c)
            back_s[n, walk, :] = res2[n][dense, :]
        for rho in range(st):
            walk, dense = level1_rows(rho)
            nat_s[1, n, walk, :] = back_s[n, dense, :]
            nat_s[0, n, walk, :] = res1[n][dense, :]

    parts = [res0, tuple(nat_s[0, n] for n in range(3)), tuple(nat_s[1, n] for n in range(3))]
    m_all = functools.reduce(jnp.maximum, [part[1] for part in parts])
    num = jnp.zeros((seq, HEAD_DIM), F32)
    den = jnp.zeros((seq, LANES), F32)
    for part_num, part_m, part_l in parts:
        wgt = jnp.exp(part_m - m_all)
        num = num + part_num * wgt
        den = den + part_l * wgt
    o_ref[...] = (num / den).astype(o_ref.dtype)


def _dilated(proj, dil_tab):
    bsz, seq, _ = proj.shape
    col = lambda off: pl.BlockSpec((None, seq, LANES), lambda h, b: (b, 0, off + h))
    return pl.pallas_call(
        _dil_kernel,
        grid=(DIL_HEADS, bsz),
        in_specs=[pl.BlockSpec(memory_space=pltpu.SMEM), col(CB_DIL_Q), col(CB_DIL_K), col(CB_DIL_V)],
        out_specs=pl.BlockSpec((None, seq, HEAD_DIM), lambda h, b: (b, 0, h)),
        out_shape=jax.ShapeDtypeStruct((bsz, seq, DIL_HEADS * HEAD_DIM), BF16),
        scratch_shapes=[pltpu.VMEM((len(DIL_PATTERNS), Q_BLOCK, 2 * Q_BLOCK), F32),
                        pltpu.VMEM((2, 3, seq, HEAD_DIM), F32),
                        pltpu.VMEM((3, seq, LANES), F32),
                        pltpu.VMEM((2, 3, seq, LANES), F32)],
        compiler_params=_params("arbitrary", "arbitrary"),
        name="dilated",
    )(dil_tab, proj, proj, proj)


def _cmp_kernel(x_ref, pos_ref, w1_ref, w2_ref, o_ref):
    nblk = o_ref.shape[0]
    half = CMP_LEN // 2
    ha = jnp.zeros((nblk, CMP_HIDDEN), F32)
    hb = jnp.zeros((nblk, CMP_HIDDEN), F32)
    for l in range(half):
        xl = x_ref[pl.ds(l, nblk, stride=CMP_STRIDE), :]
        wa = w1_ref[l * HEAD_DIM:(l + 1) * HEAD_DIM, :]
        wb = w1_ref[(half + l) * HEAD_DIM:(half + l + 1) * HEAD_DIM, :]
        ha = ha + _dot((xl + pos_ref[l:l + 1, :]).astype(BF16), wa)
        hb = hb + _dot((xl + pos_ref[half + l:half + l + 1, :]).astype(BF16), wb)
    hmid = ha + pltpu.roll(hb, nblk - 1, axis=0)
    hmid = 0.5 * hmid * (1.0 + jnp.tanh(math.sqrt(2.0 / math.pi) * (hmid + 0.044715 * hmid * hmid * hmid)))
    out = _dot(hmid.astype(BF16), w2_ref[...])
    rowi = lax.broadcasted_iota(jnp.int32, out.shape, 0)
    o_ref[...] = jnp.where(rowi < nblk - 1, out, 0.0)


def _compress(proj, cmp_pos, w1, w2):
    bsz, seq, _ = proj.shape
    nblk = seq // CMP_STRIDE
    return pl.pallas_call(
        _cmp_kernel,
        grid=(2, bsz, NSA_GROUPS),
        in_specs=[pl.BlockSpec((None, seq, LANES), lambda i, b, g: (b, 0, CB_NSA_KV + i * NSA_GROUPS + g)),
                  pl.BlockSpec((None, CMP_LEN, HEAD_DIM), lambda i, b, g: (i, 0, 0)),
                  pl.BlockSpec((None, CMP_LEN * HEAD_DIM, CMP_HIDDEN), lambda i, b, g: (i, 0, 0)),
                  pl.BlockSpec((None, CMP_HIDDEN, HEAD_DIM), lambda i, b, g: (i, 0, 0))],
        out_specs=pl.BlockSpec((None, None, None, nblk, HEAD_DIM), lambda i, b, g: (b, g, i, 0, 0)),
        out_shape=jax.ShapeDtypeStruct((bsz, NSA_GROUPS, 2, nblk, HEAD_DIM), F32),
        compiler_params=_params("parallel", "parallel", "parallel"),
        name="nsa_compress",
    )(proj, cmp_pos, w1, w2)


def _nsa_kernel(tab_ref, q0_ref, q1_ref, q2_ref, ks_ref, vs_ref, kw_ref, vw_ref, cmp_ref, s_ref,
                o_ref, strip_s, cbias_s, wtile_s, q_s, kv_s, oslc_s):
    grp = pl.program_id(0)
    step = pl.program_id(2)
    seq = q0_ref.shape[0]
    qb = Q_BLOCK
    nsub = o_ref.shape[0] // qb
    nqb = seq // qb
    n_slc = seq // SLC_BLOCK
    ncmp = cmp_ref.shape[1]
    nwin = WIN // qb
    win_keys = WIN + qb
    scale = HEAD_DIM ** -0.5
    q_refs = (q0_ref, q1_ref, q2_ref)
    head_rows = lambda x, rr: x[rr * qb:(rr + 1) * qb, :]

    @pl.when(step == 0)
    def _():
        for rr in range(NSA_REP):
            q_s[rr] = (q_refs[rr][...] * scale).astype(BF16)
        for n, ref in enumerate((ks_ref, vs_ref, kw_ref, vw_ref)):
            kv_s[n] = ref[...].astype(BF16)

    @pl.when(jnp.logical_and(pl.program_id(1) == 0, step == 0))
    def _():
        for rr in range(NSA_REP):
            head = DIL_HEADS + grp * NSA_REP + rr
            shape = strip_s.shape[1:]
            dist = (lax.broadcasted_iota(jnp.int32, shape, 0) - lax.broadcasted_iota(jnp.int32, shape, 1)
                    + (seq - qb))
            strip_s[rr] = jnp.where(dist >= 0, _bias_of_dist(jnp.maximum(dist, 0), tab_ref, head), NEG)
            shape = cbias_s.shape[1:]
            dist = (lax.broadcasted_iota(jnp.int32, shape, 0)
                    - (lax.broadcasted_iota(jnp.int32, shape, 1) * CMP_STRIDE + CMP_LEN - 1))
            cbias_s[rr] = jnp.where(dist >= 0, _bias_of_dist(jnp.maximum(dist, 0), tab_ref, head), NEG)
            shape = wtile_s.shape[1:]
            dist = lax.broadcasted_iota(jnp.int32, shape, 0) + WIN - lax.broadcasted_iota(jnp.int32, shape, 1)
            wtile_s[rr] = jnp.where((dist >= 0) & (dist < WIN),
                                    _bias_of_dist(jnp.clip(dist, 0, WIN), tab_ref, head), NEG)

    lane = lax.broadcasted_iota(jnp.int32, (1, LANES), 1)
    oj = lax.broadcasted_iota(jnp.int32, (n_slc, ncmp), 0) * SLC_BLOCK
    oc = lax.broadcasted_iota(jnp.int32, (n_slc, ncmp), 1) * CMP_STRIDE
    overlap_t = (jnp.maximum(jnp.minimum(oc + CMP_LEN, oj + SLC_BLOCK) - jnp.maximum(oc, oj), 0)
                 .astype(F32) / CMP_STRIDE)
    jrow = lax.broadcasted_iota(jnp.int32, (n_slc, qb), 0)
    tcol = lax.broadcasted_iota(jnp.int32, (n_slc, qb), 1)

    def softmax_rows(s):
        m = jnp.max(s, axis=-1, keepdims=True)
        p = jnp.exp(s - m)
        return p, jnp.sum(p, axis=-1, keepdims=True)

    def front(i):
        rows = pl.ds(pl.multiple_of(i * qb, qb), qb)
        q3 = jnp.concatenate([q_s[rr, rows, :] for rr in range(NSA_REP)], axis=0)
        sm = _sigmoid(s_ref[rows, :])

        def gate(rr, branch):
            ln = LANE_GATE + (grp * NSA_REP + rr) * 3 + branch
            return jnp.sum(jnp.where(lane == ln, sm, 0.0), axis=-1, keepdims=True)

        s_all = _dot_nt(q3, _bf(cmp_ref[0]))
        p_sum = jnp.zeros((qb, ncmp), F32)
        probs = []
        for rr in range(NSA_REP):
            cb = cbias_s[rr, rows, :]
            p, l = softmax_rows(head_rows(s_all, rr) + cb)
            p = jnp.where(cb > 0.5 * NEG, p, 0.0)
            p = p / jnp.maximum(jnp.sum(p, axis=-1, keepdims=True), 1e-30)
            p_sum = p_sum + p
            probs.append(_bf(p))
        o_cmp = _dot(jnp.concatenate(probs, axis=0), _bf(cmp_ref[1]))
        acc = [gate(rr, 0) * head_rows(o_cmp, rr) for rr in range(NSA_REP)]

        imp = _dot_nt(overlap_t, p_sum, precision=HIGHEST)
        cur = jnp.right_shift(i * qb + tcol, SLC_SHIFT)
        forced = (jrow == 0) | ((jrow <= cur) & (jrow > cur - 2))
        imp = jnp.where(forced, jnp.inf, jnp.where(jrow <= cur, imp, -jnp.inf))
        cnt = jnp.zeros((n_slc, qb), F32)
        for j2 in range(n_slc):
            other = imp[j2:j2 + 1, :]
            ge = jnp.where(other >= imp, 1.0, 0.0)
            gt = jnp.where(other > imp, 1.0, 0.0)
            cnt = cnt + jnp.where(jrow > j2, ge, gt)
        sel_t = jnp.where((cnt < SLC_TOPK) & (jrow <= cur), 1.0, 0.0)
        sel_t = jnp.concatenate([sel_t, jnp.zeros((LANES - n_slc, qb), F32)], axis=0)
        sel = _bf(sel_t.T)

        j0 = jnp.maximum(i - nwin, 0)
        krows = pl.ds(pl.multiple_of(j0 * qb, qb), win_keys)
        tile_start = pl.multiple_of((j0 - i + nwin) * qb, qb)
        s_all = _dot_nt(q3, kv_s[2, krows, :])
        probs, dens = [], []
        for rr in range(NSA_REP):
            p, l = softmax_rows(head_rows(s_all, rr) + wtile_s[rr, :, pl.ds(tile_start, win_keys)])
            probs.append(_bf(p))
            dens.append(l)
        pv = _dot(jnp.concatenate(probs, axis=0), kv_s[3, krows, :])
        acc = [acc[rr] + gate(rr, 2) * (head_rows(pv, rr) / dens[rr]) for rr in range(NSA_REP)]
        return q3, sel, acc, [gate(rr, 1) for rr in range(NSA_REP)]

    def selected(u, i, q3, sel):
        nk = (i + 1) * qb
        strip_start = (nqb - 1 - i) * qb
        s_all = _dot_nt(q3, kv_s[0, 0:nk, :])
        ej = lax.broadcasted_iota(jnp.int32, (LANES, nk), 0)
        ep = lax.broadcasted_iota(jnp.int32, (LANES, nk), 1)
        expand = jnp.where(ej == jnp.right_shift(ep, SLC_SHIFT), 1.0, 0.0).astype(BF16)
        keep = _dot(sel, expand) > 0.5
        probs, dens = [], []
        for rr in range(NSA_REP):
            s = head_rows(s_all, rr) + strip_s[rr, :, strip_start:strip_start + nk]
            p, l = softmax_rows(jnp.where(keep, s, NEG))
            probs.append(_bf(p))
            dens.append(l)
        pv = _dot(jnp.concatenate(probs, axis=0), kv_s[1, 0:nk, :])
        for rr in range(NSA_REP):
            oslc_s[u, rr] = head_rows(pv, rr) / dens[rr]

    fronts = [front(step * nsub + u) for u in range(nsub)]
    for s in range(nqb // nsub):
        @pl.when(step == s)
        def _(s=s):
            for u in range(nsub):
                selected(u, s * nsub + u, fronts[u][0], fronts[u][1])

    for u in range(nsub):
        _, _, acc, gate_slc = fronts[u]
        for rr in range(NSA_REP):
            out = acc[rr] + gate_slc[rr] * oslc_s[u, rr]
            o_ref[u * qb:(u + 1) * qb, rr * HEAD_DIM:(rr + 1) * HEAD_DIM] = out.astype(o_ref.dtype)


def _nsa(proj, cmp_kv, rel_bias):
    bsz, seq, _ = proj.shape
    ncmp = cmp_kv.shape[3]
    nqb = seq // Q_BLOCK
    strip_w = seq
    wtile_w = WIN + WIN + Q_BLOCK
    col = lambda fn: pl.BlockSpec((None, seq, LANES), lambda g, b, i: (b, 0, fn(g)))
    kv = lambda branch, which: col(lambda g: CB_NSA_KV + (branch * 2 + which) * NSA_GROUPS + g)
    return pl.pallas_call(
        _nsa_kernel,
        grid=(NSA_GROUPS, bsz, nqb // NSA_BLOCKS_PER_STEP),
        in_specs=[pl.BlockSpec(memory_space=pltpu.SMEM),
                  col(lambda g: CB_NSA_Q + g * NSA_REP), col(lambda g: CB_NSA_Q + g * NSA_REP + 1),
                  col(lambda g: CB_NSA_Q + g * NSA_REP + 2),
                  kv(1, 0), kv(1, 1), kv(2, 0), kv(2, 1),
                  pl.BlockSpec((None, None, 2, ncmp, HEAD_DIM), lambda g, b, i: (b, g, 0, 0, 0)),
                  pl.BlockSpec((None, seq, LANES), lambda g, b, i: (b, 0, CB_SMALL))],
        out_specs=pl.BlockSpec((None, NSA_BLOCKS_PER_STEP * Q_BLOCK, NSA_REP * HEAD_DIM),
                               lambda g, b, i: (b, i, g)),
        out_shape=jax.ShapeDtypeStruct((bsz, seq, NSA_GROUPS * NSA_REP * HEAD_DIM), BF16),
        scratch_shapes=[pltpu.VMEM((NSA_REP, Q_BLOCK, strip_w), F32),
                        pltpu.VMEM((NSA_REP, seq, ncmp), F32),
                        pltpu.VMEM((NSA_REP, Q_BLOCK, wtile_w), F32),
                        pltpu.VMEM((NSA_REP, seq, HEAD_DIM), BF16),
                        pltpu.VMEM((4, seq, HEAD_DIM), BF16),
                        pltpu.VMEM((NSA_BLOCKS_PER_STEP, NSA_REP, Q_BLOCK, HEAD_DIM), F32)],
        compiler_params=_params("arbitrary", "arbitrary", "arbitrary"),
        name="nsa_attention",
    )(rel_bias, proj, proj, proj, proj, proj, proj, proj, cmp_kv, proj)


IN_WIDE_A = 3072
IN_SMALL_A = 12
IN_WIDE_B = 3840
IN_SMALL_B = 18


def _w_in_prep_kernel(a_ref, b_ref, c_ref, o_ref):
    j = pl.program_id(0)
    depth = o_ref.shape[0]
    blk = o_ref.shape[1]
    n_a = IN_WIDE_A // blk
    n_b = IN_WIDE_B // blk
    s = IN_SMALL_A

    def put(lo, hi, src_ref, src_lo):
        for l in range(depth):
            o_ref[l, lo:hi, :] = src_ref[src_lo:src_lo + hi - lo, l, :].astype(o_ref.dtype)

    @pl.when(j < n_a)
    def _():
        put(0, blk, a_ref, 0)

    @pl.when(jnp.logical_and(j >= n_a, j < n_a + n_b))
    def _():
        put(0, blk - s, a_ref, s)
        put(blk - s, blk, b_ref, 0)

    @pl.when(j == n_a + n_b)
    def _():
        put(0, s, c_ref, 0)
        put(s, s + IN_SMALL_B, a_ref, s)
        o_ref[:, s + IN_SMALL_B:, :] = jnp.zeros((depth, blk - s - IN_SMALL_B, o_ref.shape[2]), o_ref.dtype)

    @pl.when(j > n_a + n_b)
    def _():
        o_ref[...] = jnp.zeros(o_ref.shape, o_ref.dtype)


def _prep_w_in(w):
    depth, d, cols = w.shape
    blk = LANES
    assert cols == IN_WIDE_A + IN_SMALL_A + IN_WIDE_B + IN_SMALL_B
    wt = jnp.transpose(w, (2, 0, 1))
    last = (cols - 1) // blk
    nxt = 16
    assert IN_SMALL_A <= nxt and blk % nxt == 0
    last_nxt = (cols - 1) // nxt
    return pl.pallas_call(
        _w_in_prep_kernel,
        grid=(PROJ_COLS // blk,),
        in_specs=[pl.BlockSpec((blk, depth, d), lambda j: (jnp.minimum(j, last), 0, 0)),
                  pl.BlockSpec((nxt, depth, d), lambda j: (jnp.minimum((j + 1) * (blk // nxt), last_nxt), 0, 0)),
                  pl.BlockSpec((nxt, depth, d), lambda j: (IN_WIDE_A // nxt, 0, 0))],
        out_specs=pl.BlockSpec((depth, blk, d), lambda j: (0, j, 0)),
        out_shape=jax.ShapeDtypeStruct((depth, PROJ_COLS, d), BF16),
        compiler_params=_params("parallel"),
        name="w_in_prep",
    )(wt, wt, wt)


_DENSE_TILES = {
    "in_proj": (1024, 1024),
    "out_proj": 512,
    "ffn_up": (1024, 512),
    "ffn_down": (1024, 512),
    "final_norm": 512,
}


def kernel(x, norm1_g, w_in, dn_conv, dn_a_log, dn_dt_bias, dn_norm_g, cmp_pos, cmp_w1, cmp_w2, w_out,
           norm2_g, ffn_up, ffn_conv, ffn_down, rel_bias, final_g):
    bsz, seq, d = x.shape
    depth = w_in.shape[0]
    w_in_b = _prep_w_in(w_in)
    cmp_w1_b, cmp_w2_b = cmp_w1.astype(BF16), cmp_w2.astype(BF16)
    w_out_b, ffn_up_b, ffn_down_b = w_out.astype(BF16), ffn_up.astype(BF16), ffn_down.astype(BF16)
    xf = x.reshape(bsz * seq, d)
    t = _DENSE_TILES
    for l in range(depth):
        proj = _norm_matmul(xf, norm1_g[l], w_in_b, l, *t["in_proj"], F32).reshape(bsz, seq, PROJ_COLS)
        o_dn = _deltanet(proj, dn_conv[l], dn_a_log[l], dn_dt_bias[l], dn_norm_g[l])
        o_dil = _dilated(proj, rel_bias)
        cmp_kv = _compress(proj, cmp_pos[l], cmp_w1_b[l], cmp_w2_b[l])
        o_nsa = _nsa(proj, cmp_kv, rel_bias)
        xf = _out_proj(o_dn.reshape(bsz * seq, -1), o_dil.reshape(bsz * seq, -1), o_nsa.reshape(bsz * seq, -1),
                       w_out_b[l], xf, t["out_proj"])
        act = _ffn_up(xf, norm2_g[l], ffn_up_b, ffn_conv, l, seq, *t["ffn_up"])
        xf = _ffn_down(act, ffn_down_b, l, xf, *t["ffn_down"])
    return _rmsnorm(xf, final_g, t["final_norm"]).reshape(bsz, seq, d)
```

```python
import functools
import math

import numpy as np
import jax
import jax.numpy as jnp
from jax import lax
from jax.experimental import pallas as pl
from jax.experimental.pallas import tpu as pltpu

F32 = jnp.float32
BF16 = jnp.bfloat16
HIGHEST = lax.Precision.HIGHEST

LANES = 128
HEAD_DIM = 128
DN_HEADS = 6
DIL_HEADS = 4
NSA_GROUPS = 2
NSA_REP = 3
DN_CHUNK = 128
DN_SUB = 16
DN_HEADS_PER_STEP = 3
CMP_LEN = 32
CMP_STRIDE = 16
CMP_HIDDEN = 256
SLC_BLOCK = 64
SLC_SHIFT = 6
SLC_TOPK = 16
NSA_BLOCKS_PER_STEP = 4
WIN = 512
Q_BLOCK = 128
FFN_ROW_CHUNKS = 1
DIL_PATTERNS = ((128, 1), (512, 4), (2048, 16))
REL_BUCKETS = 32
REL_MAX_DIST = 2048
EPS = 1e-6
NEG = -1e30
VMEM_LIMIT = 56 * 1024 * 1024

CB_DN_Q, CB_DN_K, CB_DN_V, CB_DN_Z = 0, 6, 12, 18
CB_DIL_Q, CB_DIL_K, CB_DIL_V = 24, 28, 32
CB_NSA_Q = 36
CB_NSA_KV = 42
CB_SMALL = 54
PROJ_COLS = 7168
LANE_B, LANE_A, LANE_GATE = 0, 6, 12


def _bucket_thresholds():
    n = np.arange(0, REL_MAX_DIST + 1)
    max_exact = REL_BUCKETS // 2
    out = []
    for dt in (np.float32, np.float64):
        nf = np.maximum(n, 1).astype(dt)
        large = max_exact + (np.log(nf / dt(max_exact)) / dt(math.log(REL_MAX_DIST / max_exact))
                             * dt(REL_BUCKETS - max_exact)).astype(np.int32)
        large = np.minimum(large, REL_BUCKETS - 1)
        out.append(np.where(n < max_exact, n, large))
    assert np.array_equal(out[0], out[1])
    bucket = out[1]
    assert np.all(np.diff(bucket) >= 0)
    thr = [0]
    for b in range(1, REL_BUCKETS):
        idx = np.nonzero(bucket >= b)[0]
        thr.append(int(idx[0]) if idx.size else REL_MAX_DIST + 1)
    return tuple(thr)


_THR = _bucket_thresholds()


def _bias_of_dist(dist, tab_ref, head):
    val = jnp.full(dist.shape, tab_ref[head, 0], F32)
    for b in range(1, REL_BUCKETS):
        val = jnp.where(dist >= _THR[b], tab_ref[head, b], val)
    return val


def _sigmoid(x):
    return 1.0 / (1.0 + jnp.exp(-x))


def _bf(x):
    return x.astype(BF16)


def _dot(a, b, **kw):
    return jnp.dot(a, b, preferred_element_type=F32, **kw)


def _dot_nt(a, b, **kw):
    return lax.dot_general(a, b, (((1,), (1,)), ((), ())), preferred_element_type=F32, **kw)


def _bmm(a, b):
    return jnp.einsum('nij,njk->nik', _bf(a), _bf(b), preferred_element_type=F32)


def _bmm_nt(a, b):
    return jnp.einsum('nid,njd->nij', _bf(a), _bf(b), preferred_element_type=F32)


def _params(*sem):
    return pltpu.CompilerParams(dimension_semantics=sem, vmem_limit_bytes=VMEM_LIMIT)


def _norm_matmul_kernel(x_ref, g_ref, w_ref, o_ref, h_ref):
    @pl.when(pl.program_id(1) == 0)
    def _():
        x = x_ref[...]
        ms = jnp.mean(x * x, axis=-1, keepdims=True)
        h_ref[...] = (x * lax.rsqrt(ms + EPS) * g_ref[...]).astype(BF16)

    o_ref[...] = _dot_nt(h_ref[...], w_ref[...]).astype(o_ref.dtype)


def _norm_matmul(x, g, w_t, layer, tm, tn, out_dtype):
    m, d = x.shape
    n = w_t.shape[1]
    return pl.pallas_call(
        _norm_matmul_kernel,
        grid=(m // tm, n // tn),
        in_specs=[pl.BlockSpec((tm, d), lambda i, j: (i, 0)),
                  pl.BlockSpec((1, d), lambda i, j: (0, 0)),
                  pl.BlockSpec((None, tn, d), lambda i, j: (layer, j, 0))],
        out_specs=pl.BlockSpec((tm, tn), lambda i, j: (i, j)),
        out_shape=jax.ShapeDtypeStruct((m, n), out_dtype),
        scratch_shapes=[pltpu.VMEM((tm, d), BF16)],
        compiler_params=_params("parallel", "arbitrary"),
        name="in_proj",
    )(x, g.reshape(1, d), w_t)


def _out_proj_kernel(a_ref, b_ref, c_ref, wa_ref, wb_ref, wc_ref, x_ref, o_ref):
    acc = _dot(a_ref[...], wa_ref[...])
    acc = acc + _dot(b_ref[...], wb_ref[...])
    acc = acc + _dot(c_ref[...], wc_ref[...])
    o_ref[...] = x_ref[...] + acc


def _out_proj(o_dn, o_dil, o_nsa, w_out, x, tm):
    m, d = x.shape
    ka, kb, kc = o_dn.shape[1], o_dil.shape[1], o_nsa.shape[1]
    wa, wb, wc = w_out[:ka], w_out[ka:ka + kb], w_out[ka + kb:]
    row = lambda i: (i, 0)
    full = lambda i: (0, 0)
    return pl.pallas_call(
        _out_proj_kernel,
        grid=(m // tm,),
        in_specs=[pl.BlockSpec((tm, ka), row), pl.BlockSpec((tm, kb), row), pl.BlockSpec((tm, kc), row),
                  pl.BlockSpec((ka, d), full), pl.BlockSpec((kb, d), full), pl.BlockSpec((kc, d), full),
                  pl.BlockSpec((tm, d), row)],
        out_specs=pl.BlockSpec((tm, d), row),
        out_shape=jax.ShapeDtypeStruct((m, d), F32),
        compiler_params=_params("parallel"),
        name="out_proj",
    )(o_dn, o_dil, o_nsa, wa, wb, wc, x)


def _ffn_up_kernel(x_ref, xp_ref, g_ref, wg_ref, wv_ref, cg_ref, cv_ref, o_ref, h_ref, *, tiles_per_seq):
    tm = x_ref.shape[0]
    halo = xp_ref.shape[0]
    tn = o_ref.shape[1]

    @pl.when(pl.program_id(1) == 0)
    def _():
        def norm(x):
            ms = jnp.mean(x * x, axis=-1, keepdims=True)
            return (x * lax.rsqrt(ms + EPS) * g_ref[...]).astype(BF16)
        first = (pl.program_id(0) % tiles_per_seq) == 0
        hp = norm(xp_ref[...])
        h_ref[0:halo, :] = jnp.where(first, jnp.zeros_like(hp), hp)
        h_ref[halo:halo + tm, :] = norm(x_ref[...])

    def conv(h, w_ref, c_ref):
        u = _dot(h, w_ref[...])
        y = u * c_ref[2:3, :]
        y = y + pltpu.roll(u, 1, axis=0) * c_ref[1:2, :]
        y = y + pltpu.roll(u, 2, axis=0) * c_ref[0:1, :]
        return y[halo:, :]

    rows = tm // FFN_ROW_CHUNKS
    for s in range(FFN_ROW_CHUNKS):
        h = h_ref[s * rows:s * rows + halo + rows, :]
        gate = conv(h, wg_ref, cg_ref)
        val = conv(h, wv_ref, cv_ref)
        o_ref[s * rows:(s + 1) * rows, :] = (gate * _sigmoid(gate) * val).astype(o_ref.dtype)


def _ffn_up(x, g, w_up, conv_w, layer, seq, tm, tn):
    m, d = x.shape
    f = w_up.shape[2] // 2
    taps = conv_w.shape[1]
    halo = 8
    nf = f // tn
    kern = functools.partial(_ffn_up_kernel, tiles_per_seq=seq // tm)
    return pl.pallas_call(
        kern,
        grid=(m // tm, nf),
        in_specs=[pl.BlockSpec((tm, d), lambda i, j: (i, 0)),
                  pl.BlockSpec((halo, d), lambda i, j: (jnp.maximum(i * (tm // halo) - 1, 0), 0)),
                  pl.BlockSpec((1, d), lambda i, j: (0, 0)),
                  pl.BlockSpec((None, d, tn), lambda i, j: (layer, 0, j)),
                  pl.BlockSpec((None, d, tn), lambda i, j: (layer, 0, j + nf)),
                  pl.BlockSpec((None, taps, tn), lambda i, j: (layer, 0, j)),
                  pl.BlockSpec((None, taps, tn), lambda i, j: (layer, 0, j + nf))],
        out_specs=pl.BlockSpec((tm, tn), lambda i, j: (i, j)),
        out_shape=jax.ShapeDtypeStruct((m, f), BF16),
        scratch_shapes=[pltpu.VMEM((tm + halo, d), BF16)],
        compiler_params=_params("parallel", "arbitrary"),
        name="ffn_up",
    )(x, x, g.reshape(1, d), w_up, w_up, conv_w, conv_w)


def _ffn_down_kernel(a_ref, w_ref, x_ref, o_ref):
    o_ref[...] = x_ref[...] + _dot(a_ref[...], w_ref[...])


def _ffn_down(act, w_down, layer, x, tm, tn):
    m, f = act.shape
    d = w_down.shape[2]
    return pl.pallas_call(
        _ffn_down_kernel,
        grid=(d // tn, m // tm),
        in_specs=[pl.BlockSpec((tm, f), lambda j, i: (i, 0)),
                  pl.BlockSpec((None, f, tn), lambda j, i: (layer, 0, j)),
                  pl.BlockSpec((tm, tn), lambda j, i: (i, j))],
        out_specs=pl.BlockSpec((tm, tn), lambda j, i: (i, j)),
        out_shape=jax.ShapeDtypeStruct((m, d), F32),
        compiler_params=_params("parallel", "parallel"),
        name="ffn_down",
    )(act, w_down, x)


def _rmsnorm_kernel(x_ref, g_ref, o_ref):
    x = x_ref[...]
    ms = jnp.mean(x * x, axis=-1, keepdims=True)
    o_ref[...] = x * lax.rsqrt(ms + EPS) * g_ref[...]


def _rmsnorm(x, g, tm):
    m, d = x.shape
    return pl.pallas_call(
        _rmsnorm_kernel,
        grid=(m // tm,),
        in_specs=[pl.BlockSpec((tm, d), lambda i: (i, 0)), pl.BlockSpec((1, d), lambda i: (0, 0))],
        out_specs=pl.BlockSpec((tm, d), lambda i: (i, 0)),
        out_shape=jax.ShapeDtypeStruct((m, d), F32),
        compiler_params=_params("parallel"),
        name="final_norm",
    )(x, g.reshape(1, d))


def _dn_prepare(h, ls, alog_ref, dtb_ref, q_ref, k_ref, v_ref, s_ref, cq_ref, ck_ref, cv_ref,
                lhs_s, n_s, op_s, egl_s):
    seq = q_ref.shape[0]
    c = DN_CHUNK
    nc = seq // c

    def conv_silu(x_ref, c_ref):
        taps = c_ref.shape[0]

        def conv(x, mask_rows):
            acc = x * c_ref[taps - 1:taps, ls]
            for s in range(1, taps):
                xs = pltpu.roll(x, s, axis=0)
                if mask_rows is not None:
                    xs = jnp.where(mask_rows >= s, xs, 0.0)
                acc = acc + xs * c_ref[taps - 1 - s:taps - s, ls]
            return acc

        x = x_ref[:, ls]
        first = conv(x[0:8, :], lax.broadcasted_iota(jnp.int32, (8, 1), 0))
        acc = jnp.concatenate([first, conv(x, None)[8:, :]], axis=0)
        return acc * _sigmoid(acc)

    def l2norm(x):
        return x * lax.rsqrt(jnp.sum(x * x, axis=-1, keepdims=True) + EPS)

    q = l2norm(conv_silu(q_ref, cq_ref)) * (HEAD_DIM ** -0.5)
    k = l2norm(conv_silu(k_ref, ck_ref))
    v = conv_silu(v_ref, cv_ref)

    lane = lax.broadcasted_iota(jnp.int32, (1, LANES), 1)
    sm = s_ref[...]
    b_col = jnp.sum(jnp.where(lane == LANE_B + h, sm, 0.0), axis=-1, keepdims=True)
    a_col = jnp.sum(jnp.where(lane == LANE_A + h, sm, 0.0), axis=-1, keepdims=True)
    beta = _sigmoid(b_col)
    ap = a_col + dtb_ref[h]
    softplus = jnp.maximum(ap, 0.0) + jnp.log1p(jnp.exp(-jnp.abs(ap)))
    neg_rate = -jnp.exp(jnp.full((1, 1), alog_ref[h], F32))
    g = jnp.broadcast_to(neg_rate * softplus, (seq, LANES))

    ii = lax.broadcasted_iota(jnp.int32, (c, c), 0)
    jj = lax.broadcasted_iota(jnp.int32, (c, c), 1)
    tri = jnp.broadcast_to(jnp.where(ii >= jj, 1.0, 0.0).astype(F32), (nc, c, c))
    gc3 = jnp.einsum('nij,njk->nik', tri, g.reshape(nc, c, LANES),
                     preferred_element_type=F32, precision=HIGHEST)
    gc = gc3.reshape(seq, LANES)
    egc = jnp.exp(gc)
    glast3 = jnp.broadcast_to(gc3[:, c - 1:c, :], (nc, c, LANES))
    kb = k * beta
    k3 = k.reshape(nc, c, HEAD_DIM)
    kdt = jnp.swapaxes(k3 * jnp.exp(glast3 - gc3), 1, 2)
    egl_s[...] = jnp.exp(glast3[:, 0:8, :])

    gcol = gc3[:, :, 0:c]
    grow = jnp.swapaxes(gc3, 1, 2)[:, 0:c, :]
    decay = jnp.where(ii >= jj, jnp.exp(jnp.minimum(gcol - grow, 0.0)), 0.0)

    attn = _bmm_nt(q.reshape(nc, c, HEAD_DIM), k3) * decay
    a_mat = jnp.where(ii > jj, _bmm_nt(kb.reshape(nc, c, HEAD_DIM), k3) * decay, 0.0)

    same = (ii & -DN_SUB) == (jj & -DN_SUB)
    eye = jnp.where(ii == jj, 1.0, 0.0).astype(F32)
    x1 = jnp.where(same, -a_mat, 0.0)
    a_off = jnp.where(same, 0.0, a_mat)
    t_d = eye + x1
    xp = x1
    p = 2
    while p < DN_SUB:
        xp = _bmm(xp, xp)
        t_d = t_d + _bmm(t_d, xp)
        p *= 2
    y1 = -_bmm(t_d, a_off)
    t_o = eye + y1
    yp = y1
    p = 2
    while p < c // DN_SUB:
        yp = _bmm(yp, yp)
        t_o = t_o + _bmm(t_o, yp)
        p *= 2
    t_mat = _bmm(t_o, t_d)

    u = _bmm(t_mat, (v * beta).reshape(nc, c, HEAD_DIM))
    w = _bmm(t_mat, (kb * egc).reshape(nc, c, HEAD_DIM))

    lhs_s[:, 0:HEAD_DIM, :] = _bf(-_bmm(kdt, w))
    lhs_s[:, HEAD_DIM:HEAD_DIM + c, :] = _bf((q * egc).reshape(nc, c, HEAD_DIM) - _bmm(attn, w))
    n_s[...] = _bmm(kdt, u)
    op_s[...] = _bmm(attn, u)


def _dn_kernel(alog_ref, dtb_ref, q_ref, k_ref, v_ref, z_ref, s_ref, cq_ref, ck_ref, cv_ref, ng_ref,
               o_ref, lhs_s, n_s, op_s, egl_s, o_s):
    hs = DN_HEADS_PER_STEP
    seq = q_ref.shape[0]
    c = DN_CHUNK
    lanes = [slice(u * HEAD_DIM, (u + 1) * HEAD_DIM) for u in range(hs)]
    for u in range(hs):
        _dn_prepare(pl.program_id(1) * hs + u, lanes[u], alog_ref, dtb_ref, q_ref, k_ref, v_ref, s_ref,
                    cq_ref, ck_ref, cv_ref, lhs_s.at[u], n_s.at[u], op_s.at[u], egl_s.at[u])

    def chunk_step(n, states):
        rows = pl.ds(pl.multiple_of(n * c, c), c)
        out = []
        for u in range(hs):
            r = _dot(lhs_s[u, n], _bf(states[u]))
            o_s[u, rows, :] = r[HEAD_DIM:, :] + op_s[u, n]
            out.append(states[u] * egl_s[u, n][0:1, :] + r[0:HEAD_DIM, :] + n_s[u, n])
        return tuple(out)

    lax.fori_loop(0, seq // c, chunk_step, tuple(jnp.zeros((HEAD_DIM, HEAD_DIM), F32) for _ in range(hs)))

    for u in range(hs):
        o = o_s[u]
        y = o * lax.rsqrt(jnp.mean(o * o, axis=-1, keepdims=True) + EPS) * ng_ref[...]
        z = z_ref[:, lanes[u]]
        o_ref[:, lanes[u]] = (y * (z * _sigmoid(z))).astype(o_ref.dtype)


def _deltanet(proj, dn_conv, a_log, dt_bias, norm_g):
    bsz, seq, _ = proj.shape
    nc = seq // DN_CHUNK
    hs = DN_HEADS_PER_STEP
    width = hs * HEAD_DIM
    assert DN_HEADS % hs == 0 and all(cb % hs == 0 for cb in (CB_DN_Q, CB_DN_K, CB_DN_V, CB_DN_Z))
    col = lambda off: pl.BlockSpec((None, seq, width), lambda b, p: (b, 0, off // hs + p))
    cw = lambda off: pl.BlockSpec((dn_conv.shape[0], width), lambda b, p: (0, off // hs + p))
    smem = pl.BlockSpec(memory_space=pltpu.SMEM)
    return pl.pallas_call(
        _dn_kernel,
        grid=(bsz, DN_HEADS // hs),
        in_specs=[smem, smem, col(CB_DN_Q), col(CB_DN_K), col(CB_DN_V), col(CB_DN_Z),
                  pl.BlockSpec((None, seq, LANES), lambda b, p: (b, 0, CB_SMALL)),
                  cw(CB_DN_Q), cw(CB_DN_K), cw(CB_DN_V),
                  pl.BlockSpec((1, HEAD_DIM), lambda b, p: (0, 0))],
        out_specs=pl.BlockSpec((None, seq, width), lambda b, p: (b, 0, p)),
        out_shape=jax.ShapeDtypeStruct((bsz, seq, DN_HEADS * HEAD_DIM), BF16),
        scratch_shapes=[pltpu.VMEM((hs, nc, HEAD_DIM + DN_CHUNK, HEAD_DIM), BF16),
                        pltpu.VMEM((hs, nc, HEAD_DIM, HEAD_DIM), F32),
                        pltpu.VMEM((hs, nc, DN_CHUNK, HEAD_DIM), F32),
                        pltpu.VMEM((hs, nc, 8, LANES), F32),
                        pltpu.VMEM((hs, seq, HEAD_DIM), F32)],
        compiler_params=_params("parallel", "parallel"),
        name="deltanet",
    )(a_log, dt_bias, proj, proj, proj, proj, proj, dn_conv, dn_conv, dn_conv, norm_g.reshape(1, HEAD_DIM))


def _dil_kernel(tab_ref, q_ref, k_ref, v_ref, o_ref, tile_s, perm_s, back_s, nat_s):
    head = pl.program_id(0)
    seq = q_ref.shape[0]
    qb = Q_BLOCK
    scale = HEAD_DIM ** -0.5

    @pl.when(pl.program_id(1) == 0)
    def _():
        r = lax.broadcasted_iota(jnp.int32, (qb, 2 * qb), 0)
        cidx = lax.broadcasted_iota(jnp.int32, (qb, 2 * qb), 1)
        sub = qb + r - cidx
        for pat, (window, dil) in enumerate(DIL_PATTERNS):
            valid = (sub >= 0) & (sub <= window // dil)
            tile_s[pat] = jnp.where(valid, _bias_of_dist(jnp.maximum(sub, 0) * dil, tab_ref, head), NEG)

    st = DIL_PATTERNS[1][1]
    nb = seq // qb
    grp = seq // st
    per_class = grp // qb
    assert [d for _, d in DIL_PATTERNS] == [1, st, st * st] and all(w // d == qb for w, d in DIL_PATTERNS)
    assert grp // st == qb and per_class & (per_class - 1) == 0

    def level1_rows(rho):
        return pl.ds(rho, grp, stride=st), slice(rho * grp, (rho + 1) * grp)

    def level2_rows(c):
        rho, sigma = divmod(c, st)
        return pl.ds(rho * grp + sigma, qb, stride=st), slice(c * qb, (c + 1) * qb)

    for n, ref in enumerate((q_ref, k_ref, v_ref)):
        for rho in range(st):
            walk, dense = level1_rows(rho)
            perm_s[0, n, dense, :] = ref[walk, :]
        for c in range(st * st):
            walk, dense = level2_rows(c)
            perm_s[1, n, dense, :] = perm_s[0, n, walk, :]

    unit = lax.broadcasted_iota(jnp.int32, (nb, 1, 1), 0)

    def attend(pat, q, k, v, has_prev):
        q3 = _bf(q * scale).reshape(nb, qb, HEAD_DIM)
        k3 = _bf(k).reshape(nb, qb, HEAD_DIM)
        v3 = _bf(v).reshape(nb, qb, HEAD_DIM)
        tile = tile_s[pat]
        logits = [_bmm_nt(q3, k3) + tile[:, qb:]]
        values = [v3]
        if has_prev is not None:
            shift = lambda x: jnp.concatenate([jnp.zeros_like(x[:1]), x[:-1]], axis=0)
            logits.append(jnp.where(has_prev, _bmm_nt(q3, shift(k3)) + tile[:, :qb], NEG))
            values.append(shift(v3))
        m = functools.reduce(jnp.maximum, [jnp.max(s, axis=-1, keepdims=True) for s in logits])
        probs = [jnp.exp(s - m) for s in logits]
        l = sum(jnp.sum(p, axis=-1, keepdims=True) for p in probs)
        num = sum(_bmm(p, v_) for p, v_ in zip(probs, values))
        wide = lambda x: jnp.broadcast_to(x, (nb, qb, LANES)).reshape(seq, LANES)
        return num.reshape(seq, HEAD_DIM), wide(m), wide(l)

    res0 = attend(0, q_ref[...], k_ref[...], v_ref[...], unit >= 1)
    res1 = attend(1, perm_s[0, 0], perm_s[0, 1], perm_s[0, 2], (unit & (per_class - 1)) != 0)
    res2 = attend(2, perm_s[1, 0], perm_s[1, 1], perm_s[1, 2], None)

    for n in range(3):
        for c in range(st * st):
            walk, dense = level2_rows(c)
            back_s[n, walk, :] = res2[n][dense, :]
        for rho in range(st):
            walk, dense = level1_rows(rho)
            nat_s[1, n, walk, :] = back_s[n, dense, :]
            nat_s[0, n, walk, :] = res1[n][dense, :]

    parts = [res0, tuple(nat_s[0, n] for n in range(3)), tuple(nat_s[1, n] for n in range(3))]
    m_all = functools.reduce(jnp.maximum, [part[1] for part in parts])
    num = jnp.zeros((seq, HEAD_DIM), F32)
    den = jnp.zeros((seq, LANES), F32)
    for part_num, part_m, part_l in parts:
        wgt = jnp.exp(part_m - m_all)
        num = num + part_num * wgt
        den = den + part_l * wgt
    o_ref[...] = (num / den).astype(o_ref.dtype)


def _dilated(proj, dil_tab):
    bsz, seq, _ = proj.shape
    col = lambda off: pl.BlockSpec((None, seq, LANES), lambda h, b: (b, 0, off + h))
    return pl.pallas_call(
        _dil_kernel,
        grid=(DIL_HEADS, bsz),
        in_specs=[pl.BlockSpec(memory_space=pltpu.SMEM), col(CB_DIL_Q), col(CB_DIL_K), col(CB_DIL_V)],
        out_specs=pl.BlockSpec((None, seq, HEAD_DIM), lambda h, b: (b, 0, h)),
        out_shape=jax.ShapeDtypeStruct((bsz, seq, DIL_HEADS * HEAD_DIM), BF16),
        scratch_shapes=[pltpu.VMEM((len(DIL_PATTERNS), Q_BLOCK, 2 * Q_BLOCK), F32),
                        pltpu.VMEM((2, 3, seq, HEAD_DIM), F32),
                        pltpu.VMEM((3, seq, LANES), F32),
                        pltpu.VMEM((2, 3, seq, LANES), F32)],
        compiler_params=_params("arbitrary", "arbitrary"),
        name="dilated",
    )(dil_tab, proj, proj, proj)


def _cmp_kernel(x_ref, pos_ref, w1_ref, w2_ref, o_ref):
    nblk = o_ref.shape[0]
    half = CMP_LEN // 2
    ha = jnp.zeros((nblk, CMP_HIDDEN), F32)
    hb = jnp.zeros((nblk, CMP_HIDDEN), F32)
    for l in range(half):
        xl = x_ref[pl.ds(l, nblk, stride=CMP_STRIDE), :]
        wa = w1_ref[l * HEAD_DIM:(l + 1) * HEAD_DIM, :]
        wb = w1_ref[(half + l) * HEAD_DIM:(half + l + 1) * HEAD_DIM, :]
        ha = ha + _dot((xl + pos_ref[l:l + 1, :]).astype(BF16), wa)
        hb = hb + _dot((xl + pos_ref[half + l:half + l + 1, :]).astype(BF16), wb)
    hmid = ha + pltpu.roll(hb, nblk - 1, axis=0)
    hmid = 0.5 * hmid * (1.0 + jnp.tanh(math.sqrt(2.0 / math.pi) * (hmid + 0.044715 * hmid * hmid * hmid)))
    out = _dot(hmid.astype(BF16), w2_ref[...])
    rowi = lax.broadcasted_iota(jnp.int32, out.shape, 0)
    o_ref[...] = jnp.where(rowi < nblk - 1, out, 0.0)


def _compress(proj, cmp_pos, w1, w2):
    bsz, seq, _ = proj.shape
    nblk = seq // CMP_STRIDE
    return pl.pallas_call(
        _cmp_kernel,
        grid=(2, bsz, NSA_GROUPS),
        in_specs=[pl.BlockSpec((None, seq, LANES), lambda i, b, g: (b, 0, CB_NSA_KV + i * NSA_GROUPS + g)),
                  pl.BlockSpec((None, CMP_LEN, HEAD_DIM), lambda i, b, g: (i, 0, 0)),
                  pl.BlockSpec((None, CMP_LEN * HEAD_DIM, CMP_HIDDEN), lambda i, b, g: (i, 0, 0)),
                  pl.BlockSpec((None, CMP_HIDDEN, HEAD_DIM), lambda i, b, g: (i, 0, 0))],
        out_specs=pl.BlockSpec((None, None, None, nblk, HEAD_DIM), lambda i, b, g: (b, g, i, 0, 0)),
        out_shape=jax.ShapeDtypeStruct((bsz, NSA_GROUPS, 2, nblk, HEAD_DIM), F32),
        compiler_params=_params("parallel", "parallel", "parallel"),
        name="nsa_compress",
    )(proj, cmp_pos, w1, w2)


def _nsa_kernel(tab_ref, q0_ref, q1_ref, q2_ref, ks_ref, vs_ref, kw_ref, vw_ref, cmp_ref, s_ref,
                o_ref, strip_s, cbias_s, wtile_s, q_s, kv_s, oslc_s):
    grp = pl.program_id(0)
    step = pl.program_id(2)
    seq = q0_ref.shape[0]
    qb = Q_BLOCK
    nsub = o_ref.shape[0] // qb
    nqb = seq // qb
    n_slc = seq // SLC_BLOCK
    ncmp = cmp_ref.shape[1]
    nwin = WIN // qb
    win_keys = WIN + qb
    scale = HEAD_DIM ** -0.5
    q_refs = (q0_ref, q1_ref, q2_ref)
    head_rows = lambda x, rr: x[rr * qb:(rr + 1) * qb, :]

    @pl.when(step == 0)
    def _():
        for rr in range(NSA_REP):
            q_s[rr] = (q_refs[rr][...] * scale).astype(BF16)
        for n, ref in enumerate((ks_ref, vs_ref, kw_ref, vw_ref)):
            kv_s[n] = ref[...].astype(BF16)

    @pl.when(jnp.logical_and(pl.program_id(1) == 0, step == 0))
    def _():
        for rr in range(NSA_REP):
            head = DIL_HEADS + grp * NSA_REP + rr
            shape = strip_s.shape[1:]
            dist = (lax.broadcasted_iota(jnp.int32, shape, 0) - lax.broadcasted_iota(jnp.int32, shape, 1)
                    + (seq - qb))
            strip_s[rr] = jnp.where(dist >= 0, _bias_of_dist(jnp.maximum(dist, 0), tab_ref, head), NEG)
            shape = cbias_s.shape[1:]
            dist = (lax.broadcasted_iota(jnp.int32, shape, 0)
                    - (lax.broadcasted_iota(jnp.int32, shape, 1) * CMP_STRIDE + CMP_LEN - 1))
            cbias_s[rr] = jnp.where(dist >= 0, _bias_of_dist(jnp.maximum(dist, 0), tab_ref, head), NEG)
            shape = wtile_s.shape[1:]
            dist = lax.broadcasted_iota(jnp.int32, shape, 0) + WIN - lax.broadcasted_iota(jnp.int32, shape, 1)
            wtile_s[rr] = jnp.where((dist >= 0) & (dist < WIN),
                                    _bias_of_dist(jnp.clip(dist, 0, WIN), tab_ref, head), NEG)

    lane = lax.broadcasted_iota(jnp.int32, (1, LANES), 1)
    oj = lax.broadcasted_iota(jnp.int32, (n_slc, ncmp), 0) * SLC_BLOCK
    oc = lax.broadcasted_iota(jnp.int32, (n_slc, ncmp), 1) * CMP_STRIDE
    overlap_t = (jnp.maximum(jnp.minimum(oc + CMP_LEN, oj + SLC_BLOCK) - jnp.maximum(oc, oj), 0)
                 .astype(F32) / CMP_STRIDE)
    jrow = lax.broadcasted_iota(jnp.int32, (n_slc, qb), 0)
    tcol = lax.broadcasted_iota(jnp.int32, (n_slc, qb), 1)

    def softmax_rows(s):
        m = jnp.max(s, axis=-1, keepdims=True)
        p = jnp.exp(s - m)
        return p, jnp.sum(p, axis=-1, keepdims=True)

    def front(i):
        rows = pl.ds(pl.multiple_of(i * qb, qb), qb)
        q3 = jnp.concatenate([q_s[rr, rows, :] for rr in range(NSA_REP)], axis=0)
        sm = _sigmoid(s_ref[rows, :])

        def gate(rr, branch):
            ln = LANE_GATE + (grp * NSA_REP + rr) * 3 + branch
            return jnp.sum(jnp.where(lane == ln, sm, 0.0), axis=-1, keepdims=True)

        s_all = _dot_nt(q3, _bf(cmp_ref[0]))
        p_sum = jnp.zeros((qb, ncmp), F32)
        probs = []
        for rr in range(NSA_REP):
            cb = cbias_s[rr, rows, :]
            p, l = softmax_rows(head_rows(s_all, rr) + cb)
            p = jnp.where(cb > 0.5 * NEG, p, 0.0)
            p = p / jnp.maximum(jnp.sum(p, axis=-1, keepdims=True), 1e-30)
            p_sum = p_sum + p
            probs.append(_bf(p))
        o_cmp = _dot(jnp.concatenate(probs, axis=0), _bf(cmp_ref[1]))
        acc = [gate(rr, 0) * head_rows(o_cmp, rr) for rr in range(NSA_REP)]

        imp = _dot_nt(overlap_t, p_sum, precision=HIGHEST)
        cur = jnp.right_shift(i * qb + tcol, SLC_SHIFT)
        forced = (jrow == 0) | ((jrow <= cur) & (jrow > cur - 2))
        imp = jnp.where(forced, jnp.inf, jnp.where(jrow <= cur, imp, -jnp.inf))
        cnt = jnp.zeros((n_slc, qb), F32)
        for j2 in range(n_slc):
            other = imp[j2:j2 + 1, :]
            ge = jnp.where(other >= imp, 1.0, 0.0)
            gt = jnp.where(other > imp, 1.0, 0.0)
            cnt = cnt + jnp.where(jrow > j2, ge, gt)
        sel_t = jnp.where((cnt < SLC_TOPK) & (jrow <= cur), 1.0, 0.0)
        sel_t = jnp.concatenate([sel_t, jnp.zeros((LANES - n_slc, qb), F32)], axis=0)
        sel = _bf(sel_t.T)

        j0 = jnp.maximum(i - nwin, 0)
        krows = pl.ds(pl.multiple_of(j0 * qb, qb), win_keys)
        tile_start = pl.multiple_of((j0 - i + nwin) * qb, qb)
        s_all = _dot_nt(q3, kv_s[2, krows, :])
        probs, dens = [], []
        for rr in range(NSA_REP):
            p, l = softmax_rows(head_rows(s_all, rr) + wtile_s[rr, :, pl.ds(tile_start, win_keys)])
            probs.append(_bf(p))
            dens.append(l)
        pv = _dot(jnp.concatenate(probs, axis=0), kv_s[3, krows, :])
        acc = [acc[rr] + gate(rr, 2) * (head_rows(pv, rr) / dens[rr]) for rr in range(NSA_REP)]
        return q3, sel, acc, [gate(rr, 1) for rr in range(NSA_REP)]

    def selected(u, i, q3, sel):
        nk = (i + 1) * qb
        strip_start = (nqb - 1 - i) * qb
        s_all = _dot_nt(q3, kv_s[0, 0:nk, :])
        ej = lax.broadcasted_iota(jnp.int32, (LANES, nk), 0)
        ep = lax.broadcasted_iota(jnp.int32, (LANES, nk), 1)
        expand = jnp.where(ej == jnp.right_shift(ep, SLC_SHIFT), 1.0, 0.0).astype(BF16)
        keep = _dot(sel, expand) > 0.5
        probs, dens = [], []
        for rr in range(NSA_REP):
            s = head_rows(s_all, rr) + strip_s[rr, :, strip_start:strip_start + nk]
            p, l = softmax_rows(jnp.where(keep, s, NEG))
            probs.append(_bf(p))
            dens.append(l)
        pv = _dot(jnp.concatenate(probs, axis=0), kv_s[1, 0:nk, :])
        for rr in range(NSA_REP):
            oslc_s[u, rr] = head_rows(pv, rr) / dens[rr]

    fronts = [front(step * nsub + u) for u in range(nsub)]
    for s in range(nqb // nsub):
        @pl.when(step == s)
        def _(s=s):
            for u in range(nsub):
                selected(u, s * nsub + u, fronts[u][0], fronts[u][1])

    for u in range(nsub):
        _, _, acc, gate_slc = fronts[u]
        for rr in range(NSA_REP):
            out = acc[rr] + gate_slc[rr] * oslc_s[u, rr]
            o_ref[u * qb:(u + 1) * qb, rr * HEAD_DIM:(rr + 1) * HEAD_DIM] = out.astype(o_ref.dtype)


def _nsa(proj, cmp_kv, rel_bias):
    bsz, seq, _ = proj.shape
    ncmp = cmp_kv.shape[3]
    nqb = seq // Q_BLOCK
    strip_w = seq
    wtile_w = WIN + WIN + Q_BLOCK
    col = lambda fn: pl.BlockSpec((None, seq, LANES), lambda g, b, i: (b, 0, fn(g)))
    kv = lambda branch, which: col(lambda g: CB_NSA_KV + (branch * 2 + which) * NSA_GROUPS + g)
    return pl.pallas_call(
        _nsa_kernel,
        grid=(NSA_GROUPS, bsz, nqb // NSA_BLOCKS_PER_STEP),
        in_specs=[pl.BlockSpec(memory_space=pltpu.SMEM),
                  col(lambda g: CB_NSA_Q + g * NSA_REP), col(lambda g: CB_NSA_Q + g * NSA_REP + 1),
                  col(lambda g: CB_NSA_Q + g * NSA_REP + 2),
                  kv(1, 0), kv(1, 1), kv(2, 0), kv(2, 1),
                  pl.BlockSpec((None, None, 2, ncmp, HEAD_DIM), lambda g, b, i: (b, g, 0, 0, 0)),
                  pl.BlockSpec((None, seq, LANES), lambda g, b, i: (b, 0, CB_SMALL))],
        out_specs=pl.BlockSpec((None, NSA_BLOCKS_PER_STEP * Q_BLOCK, NSA_REP * HEAD_DIM),
                               lambda g, b, i: (b, i, g)),
        out_shape=jax.ShapeDtypeStruct((bsz, seq, NSA_GROUPS * NSA_REP * HEAD_DIM), BF16),
        scratch_shapes=[pltpu.VMEM((NSA_REP, Q_BLOCK, strip_w), F32),
                        pltpu.VMEM((NSA_REP, seq, ncmp), F32),
                        pltpu.VMEM((NSA_REP, Q_BLOCK, wtile_w), F32),
                        pltpu.VMEM((NSA_REP, seq, HEAD_DIM), BF16),
                        pltpu.VMEM((4, seq, HEAD_DIM), BF16),
                        pltpu.VMEM((NSA_BLOCKS_PER_STEP, NSA_REP, Q_BLOCK, HEAD_DIM), F32)],
        compiler_params=_params("arbitrary", "arbitrary", "arbitrary"),
        name="nsa_attention",
    )(rel_bias, proj, proj, proj, proj, proj, proj, proj, cmp_kv, proj)


IN_WIDE_A = 3072
IN_SMALL_A = 12
IN_WIDE_B = 3840
IN_SMALL_B = 18


def _w_in_prep_kernel(a_ref, b_ref, c_ref, o_ref):
    j = pl.program_id(0)
    depth = o_ref.shape[0]
    blk = o_ref.shape[1]
    n_a = IN_WIDE_A // blk
    n_b = IN_WIDE_B // blk
    s = IN_SMALL_A

    def put(lo, hi, src_ref, src_lo):
        for l in range(depth):
            o_ref[l, lo:hi, :] = src_ref[src_lo:src_lo + hi - lo, l, :].astype(o_ref.dtype)

    @pl.when(j < n_a)
    def _():
        put(0, blk, a_ref, 0)

    @pl.when(jnp.logical_and(j >= n_a, j < n_a + n_b))
    def _():
        put(0, blk - s, a_ref, s)
        put(blk - s, blk, b_ref, 0)

    @pl.when(j == n_a + n_b)
    def _():
        put(0, s, c_ref, 0)
        put(s, s + IN_SMALL_B, a_ref, s)
        o_ref[:, s + IN_SMALL_B:, :] = jnp.zeros((depth, blk - s - IN_SMALL_B, o_ref.shape[2]), o_ref.dtype)

    @pl.when(j > n_a + n_b)
    def _():
        o_ref[...] = jnp.zeros(o_ref.shape, o_ref.dtype)


def _prep_w_in(w):
    depth, d, cols = w.shape
    blk = LANES
    assert cols == IN_WIDE_A + IN_SMALL_A + IN_WIDE_B + IN_SMALL_B
    wt = jnp.transpose(w, (2, 0, 1))
    last = (cols - 1) // blk
    nxt = 16
    assert IN_SMALL_A <= nxt and blk % nxt == 0
    last_nxt = (cols - 1) // nxt
    return pl.pallas_call(
        _w_in_prep_kernel,
        grid=(PROJ_COLS // blk,),
        in_specs=[pl.BlockSpec((blk, depth, d), lambda j: (jnp.minimum(j, last), 0, 0)),
                  pl.BlockSpec((nxt, depth, d), lambda j: (jnp.minimum((j + 1) * (blk // nxt), last_nxt), 0, 0)),
                  pl.BlockSpec((nxt, depth, d), lambda j: (IN_WIDE_A // nxt, 0, 0))],
        out_specs=pl.BlockSpec((depth, blk, d), lambda j: (0, j, 0)),
        out_shape=jax.ShapeDtypeStruct((depth, PROJ_COLS, d), BF16),
        compiler_params=_params("parallel"),
        name="w_in_prep",
    )(wt, wt, wt)


_DENSE_TILES = {
    "in_proj": (1024, 1024),
    "out_proj": 1024,
    "ffn_up": (1024, 512),
    "ffn_down": (1024, 512),
    "final_norm": 512,
}


def kernel(x, norm1_g, w_in, dn_conv, dn_a_log, dn_dt_bias, dn_norm_g, cmp_pos, cmp_w1, cmp_w2, w_out,
           norm2_g, ffn_up, ffn_conv, ffn_down, rel_bias, final_g):
    bsz, seq, d = x.shape
    depth = w_in.shape[0]
    w_in_b = _prep_w_in(w_in)
    cmp_w1_b, cmp_w2_b = cmp_w1.astype(BF16), cmp_w2.astype(BF16)
    w_out_b, ffn_up_b, ffn_down_b = w_out.astype(BF16), ffn_up.astype(BF16), ffn_down.astype(BF16)
    xf = x.reshape(bsz * seq, d)
    t = _DENSE_TILES
    for l in range(depth):
        proj = _norm_matmul(xf, norm1_g[l], w_in_b, l, *t["in_proj"], F32).reshape(bsz, seq, PROJ_COLS)
        o_dn = _deltanet(proj, dn_conv[l], dn_a_log[l], dn_dt_bias[l], dn_norm_g[l])
        o_dil = _dilated(proj, rel_bias)
        cmp_kv = _compress(proj, cmp_pos[l], cmp_w1_b[l], cmp_w2_b[l])
        o_nsa = _nsa(proj, cmp_kv, rel_bias)
        xf = _out_proj(o_dn.reshape(bsz * seq, -1), o_dil.reshape(bsz * seq, -1), o_nsa.reshape(bsz * seq, -1),
                       w_out_b[l], xf, t["out_proj"])
        act = _ffn_up(xf, norm2_g[l], ffn_up_b, ffn_conv, l, seq, *t["ffn_up"])
        xf = _ffn_down(act, ffn_down_b, l, xf, *t["ffn_down"])
    return _rmsnorm(xf, final_g, t["final_norm"]).reshape(bsz, seq, d)
```

```python
import functools
import math

import numpy as np
import jax
import jax.numpy as jnp
from jax import lax
from jax.experimental import pallas as pl
from jax.experimental.pallas import tpu as pltpu

F32 = jnp.float32
BF16 = jnp.bfloat16
HIGHEST = lax.Precision.HIGHEST

LANES = 128
SUBLANES = 8
HEAD_DIM = 128
DN_HEADS = 6
DIL_HEADS = 4
NSA_GROUPS = 2
NSA_REP = 3
DN_CHUNK = 128
DN_SUB = 16
DN_HEADS_PER_STEP = 3
CMP_LEN = 32
CMP_STRIDE = 16
CMP_HIDDEN = 256
SLC_BLOCK = 64
SLC_SHIFT = 6
SLC_TOPK = 16
NSA_BLOCKS_PER_STEP = 4
WIN = 512
Q_BLOCK = 128
DIL_PATTERNS = ((128, 1), (512, 4), (2048, 16))
REL_BUCKETS = 32
REL_MAX_DIST = 2048
EPS = 1e-6
NEG = -1e30
VMEM_LIMIT = 56 * 1024 * 1024

CB_DN_Q, CB_DN_K, CB_DN_V, CB_DN_Z = 0, 6, 12, 18
CB_DIL_Q, CB_DIL_K, CB_DIL_V = 24, 28, 32
CB_NSA_Q = 36
CB_NSA_KV = 42
CB_SMALL = 54
PROJ_COLS = 7168
LANE_B, LANE_A, LANE_GATE = 0, 6, 12


def _bucket_thresholds():
    n = np.arange(0, REL_MAX_DIST + 1)
    max_exact = REL_BUCKETS // 2
    out = []
    for dt in (np.float32, np.float64):
        nf = np.maximum(n, 1).astype(dt)
        large = max_exact + (np.log(nf / dt(max_exact)) / dt(math.log(REL_MAX_DIST / max_exact))
                             * dt(REL_BUCKETS - max_exact)).astype(np.int32)
        large = np.minimum(large, REL_BUCKETS - 1)
        out.append(np.where(n < max_exact, n, large))
    assert np.array_equal(out[0], out[1])
    bucket = out[1]
    assert np.all(np.diff(bucket) >= 0)
    thr = [0]
    for b in range(1, REL_BUCKETS):
        idx = np.nonzero(bucket >= b)[0]
        thr.append(int(idx[0]) if idx.size else REL_MAX_DIST + 1)
    return tuple(thr)


_THR = _bucket_thresholds()


def _bias_of_dist(dist, tab_ref, head):
    val = jnp.full(dist.shape, tab_ref[head, 0], F32)
    for b in range(1, REL_BUCKETS):
        val = jnp.where(dist >= _THR[b], tab_ref[head, b], val)
    return val


def _sigmoid(x):
    return 1.0 / (1.0 + jnp.exp(-x))


def _bf(x):
    return x.astype(BF16)


def _dot(a, b, **kw):
    return jnp.dot(a, b, preferred_element_type=F32, **kw)


def _dot_nt(a, b, **kw):
    return lax.dot_general(a, b, (((1,), (1,)), ((), ())), preferred_element_type=F32, **kw)


def _bmm(a, b):
    return jnp.einsum('nij,njk->nik', _bf(a), _bf(b), preferred_element_type=F32)


def _bmm_nt(a, b):
    return jnp.einsum('nid,njd->nij', _bf(a), _bf(b), preferred_element_type=F32)


def _params(*sem):
    return pltpu.CompilerParams(dimension_semantics=sem, vmem_limit_bytes=VMEM_LIMIT)


def _norm_matmul_kernel(x_ref, g_ref, w_ref, o_ref, h_ref):
    @pl.when(pl.program_id(1) == 0)
    def _():
        x = x_ref[...]
        ms = jnp.mean(x * x, axis=-1, keepdims=True)
        h_ref[...] = (x * lax.rsqrt(ms + EPS) * g_ref[...]).astype(BF16)

    o_ref[...] = _dot_nt(h_ref[...], w_ref[...]).astype(o_ref.dtype)


def _norm_matmul(x, g, w_t, layer, tm, tn, out_dtype):
    m, d = x.shape
    n = w_t.shape[1]
    return pl.pallas_call(
        _norm_matmul_kernel,
        grid=(m // tm, n // tn),
        in_specs=[pl.BlockSpec((tm, d), lambda i, j: (i, 0)),
                  pl.BlockSpec((1, d), lambda i, j: (0, 0)),
                  pl.BlockSpec((None, tn, d), lambda i, j: (layer, j, 0))],
        out_specs=pl.BlockSpec((tm, tn), lambda i, j: (i, j)),
        out_shape=jax.ShapeDtypeStruct((m, n), out_dtype),
        scratch_shapes=[pltpu.VMEM((tm, d), BF16)],
        compiler_params=_params("parallel", "arbitrary"),
        name="in_proj",
    )(x, g.reshape(1, d), w_t)


def _out_proj_kernel(a_ref, b_ref, c_ref, wa_ref, wb_ref, wc_ref, x_ref, o_ref):
    acc = _dot(a_ref[...], wa_ref[...])
    acc = acc + _dot(b_ref[...], wb_ref[...])
    acc = acc + _dot(c_ref[...], wc_ref[...])
    o_ref[...] = x_ref[...] + acc


def _out_proj(o_dn, o_dil, o_nsa, w_out, x, tm):
    m, d = x.shape
    ka, kb, kc = o_dn.shape[1], o_dil.shape[1], o_nsa.shape[1]
    wa, wb, wc = w_out[:ka], w_out[ka:ka + kb], w_out[ka + kb:]
    row = lambda i: (i, 0)
    full = lambda i: (0, 0)
    return pl.pallas_call(
        _out_proj_kernel,
        grid=(m // tm,),
        in_specs=[pl.BlockSpec((tm, ka), row), pl.BlockSpec((tm, kb), row), pl.BlockSpec((tm, kc), row),
                  pl.BlockSpec((ka, d), full), pl.BlockSpec((kb, d), full), pl.BlockSpec((kc, d), full),
                  pl.BlockSpec((tm, d), row)],
        out_specs=pl.BlockSpec((tm, d), row),
        out_shape=jax.ShapeDtypeStruct((m, d), F32),
        compiler_params=_params("parallel"),
        name="out_proj",
    )(o_dn, o_dil, o_nsa, wa, wb, wc, x)


def _ffn_up_kernel(x_ref, xp_ref, g_ref, wg_ref, wv_ref, cg_ref, cv_ref, o_ref, h_ref, *, tiles_per_seq):
    tm = x_ref.shape[0]
    halo = xp_ref.shape[0]

    @pl.when(pl.program_id(1) == 0)
    def _():
        def norm(x):
            ms = jnp.mean(x * x, axis=-1, keepdims=True)
            return (x * lax.rsqrt(ms + EPS) * g_ref[...]).astype(BF16)
        first = (pl.program_id(0) % tiles_per_seq) == 0
        hp = norm(xp_ref[...])
        h_ref[0:halo, :] = jnp.where(first, jnp.zeros_like(hp), hp)
        h_ref[halo:halo + tm, :] = norm(x_ref[...])

    h = h_ref[...]

    def conv(w_ref, c_ref):
        u = _dot(h, w_ref[...])
        y = u * c_ref[2:3, :]
        y = y + pltpu.roll(u, 1, axis=0) * c_ref[1:2, :]
        y = y + pltpu.roll(u, 2, axis=0) * c_ref[0:1, :]
        return y[halo:, :]

    gate = conv(wg_ref, cg_ref)
    val = conv(wv_ref, cv_ref)
    o_ref[...] = (gate * _sigmoid(gate) * val).astype(o_ref.dtype)


def _ffn_up(x, g, w_up, conv_w, layer, seq, tm, tn):
    m, d = x.shape
    f = w_up.shape[2] // 2
    taps = conv_w.shape[1]
    halo = SUBLANES
    assert taps - 1 <= halo
    nf = f // tn
    kern = functools.partial(_ffn_up_kernel, tiles_per_seq=seq // tm)
    return pl.pallas_call(
        kern,
        grid=(m // tm, nf),
        in_specs=[pl.BlockSpec((tm, d), lambda i, j: (i, 0)),
                  pl.BlockSpec((halo, d), lambda i, j: (jnp.maximum(i * (tm // halo) - 1, 0), 0)),
                  pl.BlockSpec((1, d), lambda i, j: (0, 0)),
                  pl.BlockSpec((None, d, tn), lambda i, j: (layer, 0, j)),
                  pl.BlockSpec((None, d, tn), lambda i, j: (layer, 0, j + nf)),
                  pl.BlockSpec((None, taps, tn), lambda i, j: (layer, 0, j)),
                  pl.BlockSpec((None, taps, tn), lambda i, j: (layer, 0, j + nf))],
        out_specs=pl.BlockSpec((tm, tn), lambda i, j: (i, j)),
        out_shape=jax.ShapeDtypeStruct((m, f), BF16),
        scratch_shapes=[pltpu.VMEM((tm + halo, d), BF16)],
        compiler_params=_params("parallel", "arbitrary"),
        name="ffn_up",
    )(x, x, g.reshape(1, d), w_up, w_up, conv_w, conv_w)


def _ffn_down_kernel(a_ref, w_ref, x_ref, o_ref):
    o_ref[...] = x_ref[...] + _dot(a_ref[...], w_ref[...])


def _ffn_down(act, w_down, layer, x, tm, tn):
    m, f = act.shape
    d = w_down.shape[2]
    return pl.pallas_call(
        _ffn_down_kernel,
        grid=(d // tn, m // tm),
        in_specs=[pl.BlockSpec((tm, f), lambda j, i: (i, 0)),
                  pl.BlockSpec((None, f, tn), lambda j, i: (layer, 0, j)),
                  pl.BlockSpec((tm, tn), lambda j, i: (i, j))],
        out_specs=pl.BlockSpec((tm, tn), lambda j, i: (i, j)),
        out_shape=jax.ShapeDtypeStruct((m, d), F32),
        compiler_params=_params("parallel", "parallel"),
        name="ffn_down",
    )(act, w_down, x)


def _rmsnorm_kernel(x_ref, g_ref, o_ref):
    x = x_ref[...]
    ms = jnp.mean(x * x, axis=-1, keepdims=True)
    o_ref[...] = x * lax.rsqrt(ms + EPS) * g_ref[...]


def _rmsnorm(x, g, tm):
    m, d = x.shape
    return pl.pallas_call(
        _rmsnorm_kernel,
        grid=(m // tm,),
        in_specs=[pl.BlockSpec((tm, d), lambda i: (i, 0)), pl.BlockSpec((1, d), lambda i: (0, 0))],
        out_specs=pl.BlockSpec((tm, d), lambda i: (i, 0)),
        out_shape=jax.ShapeDtypeStruct((m, d), F32),
        compiler_params=_params("parallel"),
        name="final_norm",
    )(x, g.reshape(1, d))


def _dn_prepare(h, ls, alog_ref, dtb_ref, q_ref, k_ref, v_ref, s_ref, cq_ref, ck_ref, cv_ref,
                lhs_s, n_s, op_s, egl_s):
    seq = q_ref.shape[0]
    c = DN_CHUNK
    nc = seq // c

    def conv_silu(x_ref, c_ref):
        taps = c_ref.shape[0]

        def conv(x, mask_rows):
            acc = x * c_ref[taps - 1:taps, ls]
            for s in range(1, taps):
                xs = pltpu.roll(x, s, axis=0)
                if mask_rows is not None:
                    xs = jnp.where(mask_rows >= s, xs, 0.0)
                acc = acc + xs * c_ref[taps - 1 - s:taps - s, ls]
            return acc

        x = x_ref[:, ls]
        first = conv(x[0:SUBLANES, :], lax.broadcasted_iota(jnp.int32, (SUBLANES, 1), 0))
        acc = jnp.concatenate([first, conv(x, None)[SUBLANES:, :]], axis=0)
        return acc * _sigmoid(acc)

    def l2norm(x):
        return x * lax.rsqrt(jnp.sum(x * x, axis=-1, keepdims=True) + EPS)

    q = l2norm(conv_silu(q_ref, cq_ref)) * (HEAD_DIM ** -0.5)
    k = l2norm(conv_silu(k_ref, ck_ref))
    v = conv_silu(v_ref, cv_ref)

    lane = lax.broadcasted_iota(jnp.int32, (1, LANES), 1)
    sm = s_ref[...]
    b_col = jnp.sum(jnp.where(lane == LANE_B + h, sm, 0.0), axis=-1, keepdims=True)
    a_col = jnp.sum(jnp.where(lane == LANE_A + h, sm, 0.0), axis=-1, keepdims=True)
    beta = _sigmoid(b_col)
    ap = a_col + dtb_ref[h]
    softplus = jnp.maximum(ap, 0.0) + jnp.log1p(jnp.exp(-jnp.abs(ap)))
    neg_rate = -jnp.exp(jnp.full((1, 1), alog_ref[h], F32))
    g = jnp.broadcast_to(neg_rate * softplus, (seq, LANES))

    ii = lax.broadcasted_iota(jnp.int32, (c, c), 0)
    jj = lax.broadcasted_iota(jnp.int32, (c, c), 1)
    tri = jnp.broadcast_to(jnp.where(ii >= jj, 1.0, 0.0).astype(F32), (nc, c, c))
    gc3 = jnp.einsum('nij,njk->nik', tri, g.reshape(nc, c, LANES),
                     preferred_element_type=F32, precision=HIGHEST)
    gc = gc3.reshape(seq, LANES)
    egc = jnp.exp(gc)
    glast3 = jnp.broadcast_to(gc3[:, c - 1:c, :], (nc, c, LANES))
    kb = k * beta
    k3 = k.reshape(nc, c, HEAD_DIM)
    kdt = jnp.swapaxes(k3 * jnp.exp(glast3 - gc3), 1, 2)
    egl_s[...] = jnp.exp(glast3[:, 0:SUBLANES, :])

    gcol = gc3[:, :, 0:c]
    grow = jnp.swapaxes(gc3, 1, 2)[:, 0:c, :]
    decay = jnp.where(ii >= jj, jnp.exp(jnp.minimum(gcol - grow, 0.0)), 0.0)

    attn = _bmm_nt(q.reshape(nc, c, HEAD_DIM), k3) * decay
    a_mat = jnp.where(ii > jj, _bmm_nt(kb.reshape(nc, c, HEAD_DIM), k3) * decay, 0.0)

    same = (ii & -DN_SUB) == (jj & -DN_SUB)
    eye = jnp.where(ii == jj, 1.0, 0.0).astype(F32)
    x1 = jnp.where(same, -a_mat, 0.0)
    a_off = jnp.where(same, 0.0, a_mat)
    t_d = eye + x1
    xp = x1
    p = 2
    while p < DN_SUB:
        xp = _bmm(xp, xp)
        t_d = t_d + _bmm(t_d, xp)
        p *= 2
    y1 = -_bmm(t_d, a_off)
    t_o = eye + y1
    yp = y1
    p = 2
    while p < c // DN_SUB:
        yp = _bmm(yp, yp)
        t_o = t_o + _bmm(t_o, yp)
        p *= 2
    t_mat = _bmm(t_o, t_d)

    u = _bmm(t_mat, (v * beta).reshape(nc, c, HEAD_DIM))
    w = _bmm(t_mat, (kb * egc).reshape(nc, c, HEAD_DIM))

    lhs_s[:, 0:HEAD_DIM, :] = _bf(-_bmm(kdt, w))
    lhs_s[:, HEAD_DIM:HEAD_DIM + c, :] = _bf((q * egc).reshape(nc, c, HEAD_DIM) - _bmm(attn, w))
    n_s[...] = _bmm(kdt, u)
    op_s[...] = _bmm(attn, u)


def _dn_kernel(alog_ref, dtb_ref, q_ref, k_ref, v_ref, z_ref, s_ref, cq_ref, ck_ref, cv_ref, ng_ref,
               o_ref, lhs_s, n_s, op_s, egl_s, o_s):
    hs = DN_HEADS_PER_STEP
    seq = q_ref.shape[0]
    c = DN_CHUNK
    lanes = [slice(u * HEAD_DIM, (u + 1) * HEAD_DIM) for u in range(hs)]
    for u in range(hs):
        _dn_prepare(pl.program_id(1) * hs + u, lanes[u], alog_ref, dtb_ref, q_ref, k_ref, v_ref, s_ref,
                    cq_ref, ck_ref, cv_ref, lhs_s.at[u], n_s.at[u], op_s.at[u], egl_s.at[u])

    def chunk_step(n, states):
        rows = pl.ds(pl.multiple_of(n * c, c), c)
        out = []
        for u in range(hs):
            r = _dot(lhs_s[u, n], _bf(states[u]))
            o_s[u, rows, :] = r[HEAD_DIM:, :] + op_s[u, n]
            out.append(states[u] * egl_s[u, n][0:1, :] + r[0:HEAD_DIM, :] + n_s[u, n])
        return tuple(out)

    lax.fori_loop(0, seq // c, chunk_step, tuple(jnp.zeros((HEAD_DIM, HEAD_DIM), F32) for _ in range(hs)))

    for u in range(hs):
        o = o_s[u]
        y = o * lax.rsqrt(jnp.mean(o * o, axis=-1, keepdims=True) + EPS) * ng_ref[...]
        z = z_ref[:, lanes[u]]
        o_ref[:, lanes[u]] = (y * (z * _sigmoid(z))).astype(o_ref.dtype)


def _deltanet(proj, dn_conv, a_log, dt_bias, norm_g):
    bsz, seq, _ = proj.shape
    nc = seq // DN_CHUNK
    hs = DN_HEADS_PER_STEP
    width = hs * HEAD_DIM
    assert DN_HEADS % hs == 0 and all(cb % hs == 0 for cb in (CB_DN_Q, CB_DN_K, CB_DN_V, CB_DN_Z))
    col = lambda off: pl.BlockSpec((None, seq, width), lambda b, p: (b, 0, off // hs + p))
    cw = lambda off: pl.BlockSpec((dn_conv.shape[0], width), lambda b, p: (0, off // hs + p))
    smem = pl.BlockSpec(memory_space=pltpu.SMEM)
    return pl.pallas_call(
        _dn_kernel,
        grid=(bsz, DN_HEADS // hs),
        in_specs=[smem, smem, col(CB_DN_Q), col(CB_DN_K), col(CB_DN_V), col(CB_DN_Z),
                  pl.BlockSpec((None, seq, LANES), lambda b, p: (b, 0, CB_SMALL)),
                  cw(CB_DN_Q), cw(CB_DN_K), cw(CB_DN_V),
                  pl.BlockSpec((1, HEAD_DIM), lambda b, p: (0, 0))],
        out_specs=pl.BlockSpec((None, seq, width), lambda b, p: (b, 0, p)),
        out_shape=jax.ShapeDtypeStruct((bsz, seq, DN_HEADS * HEAD_DIM), BF16),
        scratch_shapes=[pltpu.VMEM((hs, nc, HEAD_DIM + DN_CHUNK, HEAD_DIM), BF16),
                        pltpu.VMEM((hs, nc, HEAD_DIM, HEAD_DIM), F32),
                        pltpu.VMEM((hs, nc, DN_CHUNK, HEAD_DIM), F32),
                        pltpu.VMEM((hs, nc, SUBLANES, LANES), F32),
                        pltpu.VMEM((hs, seq, HEAD_DIM), F32)],
        compiler_params=_params("parallel", "parallel"),
        name="deltanet",
    )(a_log, dt_bias, proj, proj, proj, proj, proj, dn_conv, dn_conv, dn_conv, norm_g.reshape(1, HEAD_DIM))


def _dil_kernel(tab_ref, q_ref, k_ref, v_ref, o_ref, tile_s, perm_s, back_s, nat_s):
    head = pl.program_id(0)
    seq = q_ref.shape[0]
    qb = Q_BLOCK
    scale = HEAD_DIM ** -0.5

    @pl.when(pl.program_id(1) == 0)
    def _():
        r = lax.broadcasted_iota(jnp.int32, (qb, 2 * qb), 0)
        cidx = lax.broadcasted_iota(jnp.int32, (qb, 2 * qb), 1)
        sub = qb + r - cidx
        for pat, (window, dil) in enumerate(DIL_PATTERNS):
            valid = (sub >= 0) & (sub <= window // dil)
            tile_s[pat] = jnp.where(valid, _bias_of_dist(jnp.maximum(sub, 0) * dil, tab_ref, head), NEG)

    st = DIL_PATTERNS[1][1]
    nb = seq // qb
    grp = seq // st
    per_class = grp // qb
    assert [d for _, d in DIL_PATTERNS] == [1, st, st * st] and all(w // d == qb for w, d in DIL_PATTERNS)
    assert grp // st == qb and per_class & (per_class - 1) == 0

    def level1_rows(rho):
        return pl.ds(rho, grp, stride=st), slice(rho * grp, (rho + 1) * grp)

    def level2_rows(c):
        rho, sigma = divmod(c, st)
        return pl.ds(rho * grp + sigma, qb, stride=st), slice(c * qb, (c + 1) * qb)

    for n, ref in enumerate((q_ref, k_ref, v_ref)):
        for rho in range(st):
            walk, dense = level1_rows(rho)
            perm_s[0, n, dense, :] = ref[walk, :]
        for c in range(st * st):
            walk, dense = level2_rows(c)
            perm_s[1, n, dense, :] = perm_s[0, n, walk, :]

    unit = lax.broadcasted_iota(jnp.int32, (nb, 1, 1), 0)

    def attend(pat, q, k, v, has_prev):
        q3 = _bf(q * scale).reshape(nb, qb, HEAD_DIM)
        k3 = _bf(k).reshape(nb, qb, HEAD_DIM)
        v3 = _bf(v).reshape(nb, qb, HEAD_DIM)
        tile = tile_s[pat]
        logits = [_bmm_nt(q3, k3) + tile[:, qb:]]
        values = [v3]
        if has_prev is not None:
            shift = lambda x: jnp.concatenate([jnp.zeros_like(x[:1]), x[:-1]], axis=0)
            logits.append(jnp.where(has_prev, _bmm_nt(q3, shift(k3)) + tile[:, :qb], NEG))
            values.append(shift(v3))
        m = functools.reduce(jnp.maximum, [jnp.max(s, axis=-1, keepdims=True) for s in logits])
        probs = [jnp.exp(s - m) for s in logits]
        l = sum(jnp.sum(p, axis=-1, keepdims=True) for p in probs)
        num = sum(_bmm(p, v_) for p, v_ in zip(probs, values))
        wide = lambda x: jnp.broadcast_to(x, (nb, qb, LANES)).reshape(seq, LANES)
        return num.reshape(seq, HEAD_DIM), wide(m), wide(l)

    res0 = attend(0, q_ref[...], k_ref[...], v_ref[...], unit >= 1)
    res1 = attend(1, perm_s[0, 0], perm_s[0, 1], perm_s[0, 2], (unit & (per_class - 1)) != 0)
    res2 = attend(2, perm_s[1, 0], perm_s[1, 1], perm_s[1, 2], None)

    for n in range(3):
        for c in range(st * st):
            walk, dense = level2_rows(c)
            back_s[n, walk, :] = res2[n][dense, :]
        for rho in range(st):
            walk, dense = level1_rows(rho)
            nat_s[1, n, walk, :] = back_s[n, dense, :]
            nat_s[0, n, walk, :] = res1[n][dense, :]

    parts = [res0, tuple(nat_s[0, n] for n in range(3)), tuple(nat_s[1, n] for n in range(3))]
    m_all = functools.reduce(jnp.maximum, [part[1] for part in parts])
    num = jnp.zeros((seq, HEAD_DIM), F32)
    den = jnp.zeros((seq, LANES), F32)
    for part_num, part_m, part_l in parts:
        wgt = jnp.exp(part_m - m_all)
        num = num + part_num * wgt
        den = den + part_l * wgt
    o_ref[...] = (num / den).astype(o_ref.dtype)


def _dilated(proj, dil_tab):
    bsz, seq, _ = proj.shape
    col = lambda off: pl.BlockSpec((None, seq, LANES), lambda h, b: (b, 0, off + h))
    return pl.pallas_call(
        _dil_kernel,
        grid=(DIL_HEADS, bsz),
        in_specs=[pl.BlockSpec(memory_space=pltpu.SMEM), col(CB_DIL_Q), col(CB_DIL_K), col(CB_DIL_V)],
        out_specs=pl.BlockSpec((None, seq, HEAD_DIM), lambda h, b: (b, 0, h)),
        out_shape=jax.ShapeDtypeStruct((bsz, seq, DIL_HEADS * HEAD_DIM), BF16),
        scratch_shapes=[pltpu.VMEM((len(DIL_PATTERNS), Q_BLOCK, 2 * Q_BLOCK), F32),
                        pltpu.VMEM((2, 3, seq, HEAD_DIM), F32),
                        pltpu.VMEM((3, seq, LANES), F32),
                        pltpu.VMEM((2, 3, seq, LANES), F32)],
        compiler_params=_params("arbitrary", "arbitrary"),
        name="dilated",
    )(dil_tab, proj, proj, proj)


def _cmp_kernel(x_ref, pos_ref, w1_ref, w2_ref, o_ref):
    nblk = o_ref.shape[0]
    half = CMP_LEN // 2
    ha = jnp.zeros((nblk, CMP_HIDDEN), F32)
    hb = jnp.zeros((nblk, CMP_HIDDEN), F32)
    for l in range(half):
        xl = x_ref[pl.ds(l, nblk, stride=CMP_STRIDE), :]
        wa = w1_ref[l * HEAD_DIM:(l + 1) * HEAD_DIM, :]
        wb = w1_ref[(half + l) * HEAD_DIM:(half + l + 1) * HEAD_DIM, :]
        ha = ha + _dot((xl + pos_ref[l:l + 1, :]).astype(BF16), wa)
        hb = hb + _dot((xl + pos_ref[half + l:half + l + 1, :]).astype(BF16), wb)
    hmid = ha + pltpu.roll(hb, nblk - 1, axis=0)
    hmid = 0.5 * hmid * (1.0 + jnp.tanh(math.sqrt(2.0 / math.pi) * (hmid + 0.044715 * hmid * hmid * hmid)))
    out = _dot(hmid.astype(BF16), w2_ref[...])
    rowi = lax.broadcasted_iota(jnp.int32, out.shape, 0)
    o_ref[...] = jnp.where(rowi < nblk - 1, out, 0.0)


def _compress(proj, cmp_pos, w1, w2):
    bsz, seq, _ = proj.shape
    nblk = seq // CMP_STRIDE
    return pl.pallas_call(
        _cmp_kernel,
        grid=(2, bsz, NSA_GROUPS),
        in_specs=[pl.BlockSpec((None, seq, LANES), lambda i, b, g: (b, 0, CB_NSA_KV + i * NSA_GROUPS + g)),
                  pl.BlockSpec((None, CMP_LEN, HEAD_DIM), lambda i, b, g: (i, 0, 0)),
                  pl.BlockSpec((None, CMP_LEN * HEAD_DIM, CMP_HIDDEN), lambda i, b, g: (i, 0, 0)),
                  pl.BlockSpec((None, CMP_HIDDEN, HEAD_DIM), lambda i, b, g: (i, 0, 0))],
        out_specs=pl.BlockSpec((None, None, None, nblk, HEAD_DIM), lambda i, b, g: (b, g, i, 0, 0)),
        out_shape=jax.ShapeDtypeStruct((bsz, NSA_GROUPS, 2, nblk, HEAD_DIM), F32),
        compiler_params=_params("parallel", "parallel", "parallel"),
        name="nsa_compress",
    )(proj, cmp_pos, w1, w2)


def _nsa_kernel(tab_ref, q0_ref, q1_ref, q2_ref, ks_ref, vs_ref, kw_ref, vw_ref, cmp_ref, s_ref,
                o_ref, strip_s, cbias_s, wtile_s, q_s, kv_s, oslc_s):
    grp = pl.program_id(0)
    step = pl.program_id(2)
    seq = q0_ref.shape[0]
    qb = Q_BLOCK
    nsub = o_ref.shape[0] // qb
    nqb = seq // qb
    n_slc = seq // SLC_BLOCK
    ncmp = cmp_ref.shape[1]
    nwin = WIN // qb
    win_keys = WIN + qb
    scale = HEAD_DIM ** -0.5
    q_refs = (q0_ref, q1_ref, q2_ref)
    head_rows = lambda x, rr: x[rr * qb:(rr + 1) * qb, :]

    @pl.when(step == 0)
    def _():
        for rr in range(NSA_REP):
            q_s[rr] = (q_refs[rr][...] * scale).astype(BF16)
        for n, ref in enumerate((ks_ref, vs_ref, kw_ref, vw_ref)):
            kv_s[n] = ref[...].astype(BF16)

    @pl.when(jnp.logical_and(pl.program_id(1) == 0, step == 0))
    def _():
        for rr in range(NSA_REP):
            head = DIL_HEADS + grp * NSA_REP + rr
            shape = strip_s.shape[1:]
            dist = (lax.broadcasted_iota(jnp.int32, shape, 0) - lax.broadcasted_iota(jnp.int32, shape, 1)
                    + (seq - qb))
            strip_s[rr] = jnp.where(dist >= 0, _bias_of_dist(jnp.maximum(dist, 0), tab_ref, head), NEG)
            shape = cbias_s.shape[1:]
            dist = (lax.broadcasted_iota(jnp.int32, shape, 0)
                    - (lax.broadcasted_iota(jnp.int32, shape, 1) * CMP_STRIDE + CMP_LEN - 1))
            cbias_s[rr] = jnp.where(dist >= 0, _bias_of_dist(jnp.maximum(dist, 0), tab_ref, head), NEG)
            shape = wtile_s.shape[1:]
            dist = lax.broadcasted_iota(jnp.int32, shape, 0) + WIN - lax.broadcasted_iota(jnp.int32, shape, 1)
            wtile_s[rr] = jnp.where((dist >= 0) & (dist < WIN),
                                    _bias_of_dist(jnp.clip(dist, 0, WIN), tab_ref, head), NEG)

    lane = lax.broadcasted_iota(jnp.int32, (1, LANES), 1)
    oj = lax.broadcasted_iota(jnp.int32, (n_slc, ncmp), 0) * SLC_BLOCK
    oc = lax.broadcasted_iota(jnp.int32, (n_slc, ncmp), 1) * CMP_STRIDE
    overlap_t = (jnp.maximum(jnp.minimum(oc + CMP_LEN, oj + SLC_BLOCK) - jnp.maximum(oc, oj), 0)
                 .astype(F32) / CMP_STRIDE)
    jrow = lax.broadcasted_iota(jnp.int32, (n_slc, qb), 0)
    tcol = lax.broadcasted_iota(jnp.int32, (n_slc, qb), 1)

    def softmax_rows(s):
        m = jnp.max(s, axis=-1, keepdims=True)
        p = jnp.exp(s - m)
        return p, jnp.sum(p, axis=-1, keepdims=True)

    def front(i):
        rows = pl.ds(pl.multiple_of(i * qb, qb), qb)
        q3 = jnp.concatenate([q_s[rr, rows, :] for rr in range(NSA_REP)], axis=0)
        sm = _sigmoid(s_ref[rows, :])

        def gate(rr, branch):
            ln = LANE_GATE + (grp * NSA_REP + rr) * 3 + branch
            return jnp.sum(jnp.where(lane == ln, sm, 0.0), axis=-1, keepdims=True)

        s_all = _dot_nt(q3, _bf(cmp_ref[0]))
        p_sum = jnp.zeros((qb, ncmp), F32)
        probs = []
        for rr in range(NSA_REP):
            cb = cbias_s[rr, rows, :]
            p, l = softmax_rows(head_rows(s_all, rr) + cb)
            p = jnp.where(cb > 0.5 * NEG, p, 0.0)
            p = p / jnp.maximum(jnp.sum(p, axis=-1, keepdims=True), 1e-30)
            p_sum = p_sum + p
            probs.append(_bf(p))
        o_cmp = _dot(jnp.concatenate(probs, axis=0), _bf(cmp_ref[1]))
        acc = [gate(rr, 0) * head_rows(o_cmp, rr) for rr in range(NSA_REP)]

        imp = _dot_nt(overlap_t, p_sum, precision=HIGHEST)
        cur = jnp.right_shift(i * qb + tcol, SLC_SHIFT)
        forced = (jrow == 0) | ((jrow <= cur) & (jrow > cur - 2))
        imp = jnp.where(forced, jnp.inf, jnp.where(jrow <= cur, imp, -jnp.inf))
        cnt = jnp.zeros((n_slc, qb), F32)
        for j2 in range(n_slc):
            other = imp[j2:j2 + 1, :]
            ge = jnp.where(other >= imp, 1.0, 0.0)
            gt = jnp.where(other > imp, 1.0, 0.0)
            cnt = cnt + jnp.where(jrow > j2, ge, gt)
        sel_t = jnp.where((cnt < SLC_TOPK) & (jrow <= cur), 1.0, 0.0)
        sel_t = jnp.concatenate([sel_t, jnp.zeros((LANES - n_slc, qb), F32)], axis=0)
        sel = _bf(sel_t.T)

        j0 = jnp.maximum(i - nwin, 0)
        krows = pl.ds(pl.multiple_of(j0 * qb, qb), win_keys)
        tile_start = pl.multiple_of((j0 - i + nwin) * qb, qb)
        s_all = _dot_nt(q3, kv_s[2, krows, :])
        probs, dens = [], []
        for rr in range(NSA_REP):
            p, l = softmax_rows(head_rows(s_all, rr) + wtile_s[rr, :, pl.ds(tile_start, win_keys)])
            probs.append(_bf(p))
            dens.append(l)
        pv = _dot(jnp.concatenate(probs, axis=0), kv_s[3, krows, :])
        acc = [acc[rr] + gate(rr, 2) * (head_rows(pv, rr) / dens[rr]) for rr in range(NSA_REP)]
        return q3, sel, acc, [gate(rr, 1) for rr in range(NSA_REP)]

    def selected(u, i, q3, sel):
        nk = (i + 1) * qb
        strip_start = (nqb - 1 - i) * qb
        s_all = _dot_nt(q3, kv_s[0, 0:nk, :])
        ej = lax.broadcasted_iota(jnp.int32, (LANES, nk), 0)
        ep = lax.broadcasted_iota(jnp.int32, (LANES, nk), 1)
        expand = jnp.where(ej == jnp.right_shift(ep, SLC_SHIFT), 1.0, 0.0).astype(BF16)
        keep = _dot(sel, expand) > 0.5
        probs, dens = [], []
        for rr in range(NSA_REP):
            s = head_rows(s_all, rr) + strip_s[rr, :, strip_start:strip_start + nk]
            p, l = softmax_rows(jnp.where(keep, s, NEG))
            probs.append(_bf(p))
            dens.append(l)
        pv = _dot(jnp.concatenate(probs, axis=0), kv_s[1, 0:nk, :])
        for rr in range(NSA_REP):
            oslc_s[u, rr] = head_rows(pv, rr) / dens[rr]

    fronts = [front(step * nsub + u) for u in range(nsub)]
    for s in range(nqb // nsub):
        @pl.when(step == s)
        def _(s=s):
            for u in range(nsub):
                selected(u, s * nsub + u, fronts[u][0], fronts[u][1])

    for u in range(nsub):
        _, _, acc, gate_slc = fronts[u]
        for rr in range(NSA_REP):
            out = acc[rr] + gate_slc[rr] * oslc_s[u, rr]
            o_ref[u * qb:(u + 1) * qb, rr * HEAD_DIM:(rr + 1) * HEAD_DIM] = out.astype(o_ref.dtype)


def _nsa(proj, cmp_kv, rel_bias):
    bsz, seq, _ = proj.shape
    ncmp = cmp_kv.shape[3]
    nqb = seq // Q_BLOCK
    strip_w = seq
    wtile_w = WIN + WIN + Q_BLOCK
    col = lambda fn: pl.BlockSpec((None, seq, LANES), lambda g, b, i: (b, 0, fn(g)))
    kv = lambda branch, which: col(lambda g: CB_NSA_KV + (branch * 2 + which) * NSA_GROUPS + g)
    return pl.pallas_call(
        _nsa_kernel,
        grid=(NSA_GROUPS, bsz, nqb // NSA_BLOCKS_PER_STEP),
        in_specs=[pl.BlockSpec(memory_space=pltpu.SMEM),
                  col(lambda g: CB_NSA_Q + g * NSA_REP), col(lambda g: CB_NSA_Q + g * NSA_REP + 1),
                  col(lambda g: CB_NSA_Q + g * NSA_REP + 2),
                  kv(1, 0), kv(1, 1), kv(2, 0), kv(2, 1),
                  pl.BlockSpec((None, None, 2, ncmp, HEAD_DIM), lambda g, b, i: (b, g, 0, 0, 0)),
                  pl.BlockSpec((None, seq, LANES), lambda g, b, i: (b, 0, CB_SMALL))],
        out_specs=pl.BlockSpec((None, NSA_BLOCKS_PER_STEP * Q_BLOCK, NSA_REP * HEAD_DIM),
                               lambda g, b, i: (b, i, g)),
        out_shape=jax.ShapeDtypeStruct((bsz, seq, NSA_GROUPS * NSA_REP * HEAD_DIM), BF16),
        scratch_shapes=[pltpu.VMEM((NSA_REP, Q_BLOCK, strip_w), F32),
                        pltpu.VMEM((NSA_REP, seq, ncmp), F32),
                        pltpu.VMEM((NSA_REP, Q_BLOCK, wtile_w), F32),
                        pltpu.VMEM((NSA_REP, seq, HEAD_DIM), BF16),
                        pltpu.VMEM((4, seq, HEAD_DIM), BF16),
                        pltpu.VMEM((NSA_BLOCKS_PER_STEP, NSA_REP, Q_BLOCK, HEAD_DIM), F32)],
        compiler_params=_params("arbitrary", "arbitrary", "arbitrary"),
        name="nsa_attention",
    )(rel_bias, proj, proj, proj, proj, proj, proj, proj, cmp_kv, proj)


IN_WIDE_A = 3072
IN_SMALL_A = 12
IN_WIDE_B = 3840
IN_SMALL_B = 18


def _w_in_prep_kernel(a_ref, b_ref, c_ref, o_ref):
    j = pl.program_id(0)
    depth = o_ref.shape[0]
    blk = o_ref.shape[1]
    n_a = IN_WIDE_A // blk
    n_b = IN_WIDE_B // blk
    s = IN_SMALL_A

    def put(lo, hi, src_ref, src_lo):
        for l in range(depth):
            o_ref[l, lo:hi, :] = src_ref[src_lo:src_lo + hi - lo, l, :].astype(o_ref.dtype)

    @pl.when(j < n_a)
    def _():
        put(0, blk, a_ref, 0)

    @pl.when(jnp.logical_and(j >= n_a, j < n_a + n_b))
    def _():
        put(0, blk - s, a_ref, s)
        put(blk - s, blk, b_ref, 0)

    @pl.when(j == n_a + n_b)
    def _():
        put(0, s, c_ref, 0)
        put(s, s + IN_SMALL_B, a_ref, s)
        o_ref[:, s + IN_SMALL_B:, :] = jnp.zeros((depth, blk - s - IN_SMALL_B, o_ref.shape[2]), o_ref.dtype)

    @pl.when(j > n_a + n_b)
    def _():
        o_ref[...] = jnp.zeros(o_ref.shape, o_ref.dtype)


def _prep_w_in(w):
    depth, d, cols = w.shape
    blk = LANES
    assert cols == IN_WIDE_A + IN_SMALL_A + IN_WIDE_B + IN_SMALL_B
    wt = jnp.transpose(w, (2, 0, 1))
    last = (cols - 1) // blk
    nxt = 2 * SUBLANES
    assert IN_SMALL_A <= nxt and blk % nxt == 0 and IN_WIDE_A % nxt == 0
    last_nxt = (cols - 1) // nxt
    return pl.pallas_call(
        _w_in_prep_kernel,
        grid=(PROJ_COLS // blk,),
        in_specs=[pl.BlockSpec((blk, depth, d), lambda j: (jnp.minimum(j, last), 0, 0)),
                  pl.BlockSpec((nxt, depth, d), lambda j: (jnp.minimum((j + 1) * (blk // nxt), last_nxt), 0, 0)),
                  pl.BlockSpec((nxt, depth, d), lambda j: (IN_WIDE_A // nxt, 0, 0))],
        out_specs=pl.BlockSpec((depth, blk, d), lambda j: (0, j, 0)),
        out_shape=jax.ShapeDtypeStruct((depth, PROJ_COLS, d), BF16),
        compiler_params=_params("parallel"),
        name="w_in_prep",
    )(wt, wt, wt)


_DENSE_TILES = {
    "in_proj": (1024, 1792),
    "out_proj": 1024,
    "ffn_up": (1024, 512),
    "ffn_down": (1024, 512),
    "final_norm": 512,
}


def kernel(x, norm1_g, w_in, dn_conv, dn_a_log, dn_dt_bias, dn_norm_g, cmp_pos, cmp_w1, cmp_w2, w_out,
           norm2_g, ffn_up, ffn_conv, ffn_down, rel_bias, final_g):
    bsz, seq, d = x.shape
    depth = w_in.shape[0]
    w_in_b = _prep_w_in(w_in)
    cmp_w1_b, cmp_w2_b = cmp_w1.astype(BF16), cmp_w2.astype(BF16)
    w_out_b, ffn_up_b, ffn_down_b = w_out.astype(BF16), ffn_up.astype(BF16), ffn_down.astype(BF16)
    xf = x.reshape(bsz * seq, d)
    t = _DENSE_TILES
    for l in range(depth):
        proj = _norm_matmul(xf, norm1_g[l], w_in_b, l, *t["in_proj"], F32).reshape(bsz, seq, PROJ_COLS)
        o_dn = _deltanet(proj, dn_conv[l], dn_a_log[l], dn_dt_bias[l], dn_norm_g[l])
        o_dil = _dilated(proj, rel_bias)
        cmp_kv = _compress(proj, cmp_pos[l], cmp_w1_b[l], cmp_w2_b[l])
        o_nsa = _nsa(proj, cmp_kv, rel_bias)
        xf = _out_proj(o_dn.reshape(bsz * seq, -1), o_dil.reshape(bsz * seq, -1), o_nsa.reshape(bsz * seq, -1),
                       w_out_b[l], xf, t["out_proj"])
        act = _ffn_up(xf, norm2_g[l], ffn_up_b, ffn_conv, l, seq, *t["ffn_up"])
        xf = _ffn_down(act, ffn_down_b, l, xf, *t["ffn_down"])
    return _rmsnorm(xf, final_g, t["final_norm"]).reshape(bsz, seq, d)
```

```python
import functools
import math

import numpy as np
import jax
import jax.numpy as jnp
from jax import lax
from jax.experimental import pallas as pl
from jax.experimental.pallas import tpu as pltpu

F32 = jnp.float32
BF16 = jnp.bfloat16
HIGHEST = lax.Precision.HIGHEST

LANES = 128
SUBLANES = 8
HEAD_DIM = 128
DN_HEADS = 6
DIL_HEADS = 4
NSA_GROUPS = 2
NSA_REP = 3
DN_CHUNK = 128
DN_SUB = 16
DN_HEADS_PER_STEP = 3
CMP_LEN = 32
CMP_STRIDE = 16
CMP_HIDDEN = 256
SLC_BLOCK = 64
SLC_SHIFT = 6
SLC_TOPK = 16
NSA_BLOCKS_PER_STEP = 4
WIN = 512
Q_BLOCK = 128
DIL_PATTERNS = ((128, 1), (512, 4), (2048, 16))
REL_BUCKETS = 32
REL_MAX_DIST = 2048
EPS = 1e-6
NEG = -1e30
VMEM_LIMIT = 56 * 1024 * 1024

CB_DN_Q, CB_DN_K, CB_DN_V, CB_DN_Z = 0, 6, 12, 18
CB_DIL_Q, CB_DIL_K, CB_DIL_V = 24, 28, 32
CB_NSA_Q = 36
CB_NSA_KV = 42
CB_SMALL = 54
PROJ_COLS = 7168
LANE_B, LANE_A, LANE_GATE = 0, 6, 12


def _bucket_thresholds():
    n = np.arange(0, REL_MAX_DIST + 1)
    max_exact = REL_BUCKETS // 2
    out = []
    for dt in (np.float32, np.float64):
        nf = np.maximum(n, 1).astype(dt)
        large = max_exact + (np.log(nf / dt(max_exact)) / dt(math.log(REL_MAX_DIST / max_exact))
                             * dt(REL_BUCKETS - max_exact)).astype(np.int32)
        large = np.minimum(large, REL_BUCKETS - 1)
        out.append(np.where(n < max_exact, n, large))
    assert np.array_equal(out[0], out[1])
    bucket = out[1]
    assert np.all(np.diff(bucket) >= 0)
    thr = [0]
    for b in range(1, REL_BUCKETS):
        idx = np.nonzero(bucket >= b)[0]
        thr.append(int(idx[0]) if idx.size else REL_MAX_DIST + 1)
    return tuple(thr)


_THR = _bucket_thresholds()


def _bias_of_dist(dist, tab_ref, head):
    val = jnp.full(dist.shape, tab_ref[head, 0], F32)
    for b in range(1, REL_BUCKETS):
        val = jnp.where(dist >= _THR[b], tab_ref[head, b], val)
    return val


def _sigmoid(x):
    return 1.0 / (1.0 + jnp.exp(-x))


def _bf(x):
    return x.astype(BF16)


def _dot(a, b, **kw):
    return jnp.dot(a, b, preferred_element_type=F32, **kw)


def _dot_nt(a, b, **kw):
    return lax.dot_general(a, b, (((1,), (1,)), ((), ())), preferred_element_type=F32, **kw)


def _bmm(a, b):
    return jnp.einsum('nij,njk->nik', _bf(a), _bf(b), preferred_element_type=F32)


def _bmm_nt(a, b):
    return jnp.einsum('nid,njd->nij', _bf(a), _bf(b), preferred_element_type=F32)


def _params(*sem):
    return pltpu.CompilerParams(dimension_semantics=sem, vmem_limit_bytes=VMEM_LIMIT)


def _norm_matmul_kernel(x_ref, g_ref, w_ref, o_ref, h_ref):
    @pl.when(pl.program_id(1) == 0)
    def _():
        x = x_ref[...]
        ms = jnp.mean(x * x, axis=-1, keepdims=True)
        h_ref[...] = (x * lax.rsqrt(ms + EPS) * g_ref[...]).astype(BF16)

    o_ref[...] = _dot_nt(h_ref[...], w_ref[...]).astype(o_ref.dtype)


def _norm_matmul(x, g, w_t, layer, tm, tn, out_dtype):
    m, d = x.shape
    n = w_t.shape[1]
    return pl.pallas_call(
        _norm_matmul_kernel,
        grid=(m // tm, n // tn),
        in_specs=[pl.BlockSpec((tm, d), lambda i, j: (i, 0)),
                  pl.BlockSpec((1, d), lambda i, j: (0, 0)),
                  pl.BlockSpec((None, tn, d), lambda i, j: (layer, j, 0))],
        out_specs=pl.BlockSpec((tm, tn), lambda i, j: (i, j)),
        out_shape=jax.ShapeDtypeStruct((m, n), out_dtype),
        scratch_shapes=[pltpu.VMEM((tm, d), BF16)],
        compiler_params=_params("parallel", "arbitrary"),
        name="in_proj",
    )(x, g.reshape(1, d), w_t)


def _out_proj_kernel(a_ref, b_ref, c_ref, wa_ref, wb_ref, wc_ref, x_ref, g_ref, o_ref, h_ref):
    acc = _dot(a_ref[...], wa_ref[...])
    acc = acc + _dot(b_ref[...], wb_ref[...])
    acc = acc + _dot(c_ref[...], wc_ref[...])
    x = x_ref[...] + acc
    o_ref[...] = x
    ms = jnp.mean(x * x, axis=-1, keepdims=True)
    h_ref[...] = (x * lax.rsqrt(ms + EPS) * g_ref[...]).astype(h_ref.dtype)


def _out_proj(o_dn, o_dil, o_nsa, w_out, x, g_next, tm):
    m, d = x.shape
    ka, kb, kc = o_dn.shape[1], o_dil.shape[1], o_nsa.shape[1]
    wa, wb, wc = w_out[:ka], w_out[ka:ka + kb], w_out[ka + kb:]
    row = lambda i: (i, 0)
    full = lambda i: (0, 0)
    return pl.pallas_call(
        _out_proj_kernel,
        grid=(m // tm,),
        in_specs=[pl.BlockSpec((tm, ka), row), pl.BlockSpec((tm, kb), row), pl.BlockSpec((tm, kc), row),
                  pl.BlockSpec((ka, d), full), pl.BlockSpec((kb, d), full), pl.BlockSpec((kc, d), full),
                  pl.BlockSpec((tm, d), row), pl.BlockSpec((1, d), full)],
        out_specs=[pl.BlockSpec((tm, d), row), pl.BlockSpec((tm, d), row)],
        out_shape=[jax.ShapeDtypeStruct((m, d), F32), jax.ShapeDtypeStruct((m, d), BF16)],
        compiler_params=_params("parallel"),
        name="out_proj",
    )(o_dn, o_dil, o_nsa, wa, wb, wc, x, g_next.reshape(1, d))


def _ffn_up_kernel(h_ref, hp_ref, wg_ref, wv_ref, cg_ref, cv_ref, o_ref, *, tiles_per_seq):
    grp = SUBLANES
    first = (pl.program_id(0) % tiles_per_seq) == 0
    h = h_ref[...]
    h_before = hp_ref[hp_ref.shape[0] - grp:, :]

    def conv(w_ref, c_ref):
        def taps(u):
            y = u * c_ref[2:3, :]
            y = y + pltpu.roll(u, 1, axis=0) * c_ref[1:2, :]
            return y + pltpu.roll(u, 2, axis=0) * c_ref[0:1, :]
        u = _dot(h, w_ref[...])
        u_before = jnp.where(first, 0.0, _dot(h_before, w_ref[...]))
        head = taps(jnp.concatenate([u_before, u[0:grp, :]], axis=0))[grp:, :]
        return jnp.concatenate([head, taps(u)[grp:, :]], axis=0)

    gate = conv(wg_ref, cg_ref)
    val = conv(wv_ref, cv_ref)
    o_ref[...] = (gate * _sigmoid(gate) * val).astype(o_ref.dtype)


def _ffn_up(h, w_up, conv_w, layer, seq, tm, tn):
    m, d = h.shape
    f = w_up.shape[2] // 2
    taps = conv_w.shape[1]
    halo = 2 * SUBLANES
    assert taps - 1 <= SUBLANES and taps == 3
    nf = f // tn
    kern = functools.partial(_ffn_up_kernel, tiles_per_seq=seq // tm)
    return pl.pallas_call(
        kern,
        grid=(m // tm, nf),
        in_specs=[pl.BlockSpec((tm, d), lambda i, j: (i, 0)),
                  pl.BlockSpec((halo, d), lambda i, j: (jnp.maximum(i * (tm // halo) - 1, 0), 0)),
                  pl.BlockSpec((None, d, tn), lambda i, j: (layer, 0, j)),
                  pl.BlockSpec((None, d, tn), lambda i, j: (layer, 0, j + nf)),
                  pl.BlockSpec((None, taps, tn), lambda i, j: (layer, 0, j)),
                  pl.BlockSpec((None, taps, tn), lambda i, j: (layer, 0, j + nf))],
        out_specs=pl.BlockSpec((tm, tn), lambda i, j: (i, j)),
        out_shape=jax.ShapeDtypeStruct((m, f), BF16),
        compiler_params=_params("parallel", "arbitrary"),
        name="ffn_up",
    )(h, h, w_up, w_up, conv_w, conv_w)


def _ffn_down_kernel(a_ref, w_ref, x_ref, o_ref):
    o_ref[...] = x_ref[...] + _dot(a_ref[...], w_ref[...])


def _ffn_down(act, w_down, layer, x, tm, tn):
    m, f = act.shape
    d = w_down.shape[2]
    return pl.pallas_call(
        _ffn_down_kernel,
        grid=(d // tn, m // tm),
        in_specs=[pl.BlockSpec((tm, f), lambda j, i: (i, 0)),
                  pl.BlockSpec((None, f, tn), lambda j, i: (layer, 0, j)),
                  pl.BlockSpec((tm, tn), lambda j, i: (i, j))],
        out_specs=pl.BlockSpec((tm, tn), lambda j, i: (i, j)),
        out_shape=jax.ShapeDtypeStruct((m, d), F32),
        compiler_params=_params("parallel", "parallel"),
        name="ffn_down",
    )(act, w_down, x)


def _rmsnorm_kernel(x_ref, g_ref, o_ref):
    x = x_ref[...]
    ms = jnp.mean(x * x, axis=-1, keepdims=True)
    o_ref[...] = x * lax.rsqrt(ms + EPS) * g_ref[...]


def _rmsnorm(x, g, tm):
    m, d = x.shape
    return pl.pallas_call(
        _rmsnorm_kernel,
        grid=(m // tm,),
        in_specs=[pl.BlockSpec((tm, d), lambda i: (i, 0)), pl.BlockSpec((1, d), lambda i: (0, 0))],
        out_specs=pl.BlockSpec((tm, d), lambda i: (i, 0)),
        out_shape=jax.ShapeDtypeStruct((m, d), F32),
        compiler_params=_params("parallel"),
        name="final_norm",
    )(x, g.reshape(1, d))


def _dn_prepare(h, ls, alog_ref, dtb_ref, q_ref, k_ref, v_ref, s_ref, cq_ref, ck_ref, cv_ref,
                lhs_s, n_s, op_s, egl_s):
    seq = q_ref.shape[0]
    c = DN_CHUNK
    nc = seq // c

    def conv_silu(x_ref, c_ref):
        taps = c_ref.shape[0]

        def conv(x, mask_rows):
            acc = x * c_ref[taps - 1:taps, ls]
            for s in range(1, taps):
                xs = pltpu.roll(x, s, axis=0)
                if mask_rows is not None:
                    xs = jnp.where(mask_rows >= s, xs, 0.0)
                acc = acc + xs * c_ref[taps - 1 - s:taps - s, ls]
            return acc

        x = x_ref[:, ls]
        first = conv(x[0:SUBLANES, :], lax.broadcasted_iota(jnp.int32, (SUBLANES, 1), 0))
        acc = jnp.concatenate([first, conv(x, None)[SUBLANES:, :]], axis=0)
        return acc * _sigmoid(acc)

    def l2norm(x):
        return x * lax.rsqrt(jnp.sum(x * x, axis=-1, keepdims=True) + EPS)

    q = l2norm(conv_silu(q_ref, cq_ref)) * (HEAD_DIM ** -0.5)
    k = l2norm(conv_silu(k_ref, ck_ref))
    v = conv_silu(v_ref, cv_ref)

    lane = lax.broadcasted_iota(jnp.int32, (1, LANES), 1)
    sm = s_ref[...]
    b_col = jnp.sum(jnp.where(lane == LANE_B + h, sm, 0.0), axis=-1, keepdims=True)
    a_col = jnp.sum(jnp.where(lane == LANE_A + h, sm, 0.0), axis=-1, keepdims=True)
    beta = _sigmoid(b_col)
    ap = a_col + dtb_ref[h]
    softplus = jnp.maximum(ap, 0.0) + jnp.log1p(jnp.exp(-jnp.abs(ap)))
    neg_rate = -jnp.exp(jnp.full((1, 1), alog_ref[h], F32))
    g = jnp.broadcast_to(neg_rate * softplus, (seq, LANES))

    ii = lax.broadcasted_iota(jnp.int32, (c, c), 0)
    jj = lax.broadcasted_iota(jnp.int32, (c, c), 1)
    tri = jnp.broadcast_to(jnp.where(ii >= jj, 1.0, 0.0).astype(F32), (nc, c, c))
    gc3 = jnp.einsum('nij,njk->nik', tri, g.reshape(nc, c, LANES),
                     preferred_element_type=F32, precision=HIGHEST)
    gc = gc3.reshape(seq, LANES)
    egc = jnp.exp(gc)
    glast3 = jnp.broadcast_to(gc3[:, c - 1:c, :], (nc, c, LANES))
    kb = k * beta
    k3 = k.reshape(nc, c, HEAD_DIM)
    kdt = jnp.swapaxes(k3 * jnp.exp(glast3 - gc3), 1, 2)
    egl_s[...] = jnp.exp(glast3[:, 0:SUBLANES, :])

    gcol = gc3[:, :, 0:c]
    grow = jnp.swapaxes(gc3, 1, 2)[:, 0:c, :]
    decay = jnp.where(ii >= jj, jnp.exp(jnp.minimum(gcol - grow, 0.0)), 0.0)

    attn = _bmm_nt(q.reshape(nc, c, HEAD_DIM), k3) * decay
    a_mat = jnp.where(ii > jj, _bmm_nt(kb.reshape(nc, c, HEAD_DIM), k3) * decay, 0.0)

    same = (ii & -DN_SUB) == (jj & -DN_SUB)
    eye = jnp.where(ii == jj, 1.0, 0.0).astype(F32)
    x1 = jnp.where(same, -a_mat, 0.0)
    a_off = jnp.where(same, 0.0, a_mat)
    t_d = eye + x1
    xp = x1
    p = 2
    while p < DN_SUB:
        xp = _bmm(xp, xp)
        t_d = t_d + _bmm(t_d, xp)
        p *= 2
    y1 = -_bmm(t_d, a_off)
    t_o = eye + y1
    yp = y1
    p = 2
    while p < c // DN_SUB:
        yp = _bmm(yp, yp)
        t_o = t_o + _bmm(t_o, yp)
        p *= 2
    t_mat = _bmm(t_o, t_d)

    u = _bmm(t_mat, (v * beta).reshape(nc, c, HEAD_DIM))
    w = _bmm(t_mat, (kb * egc).reshape(nc, c, HEAD_DIM))

    lhs_s[:, 0:HEAD_DIM, :] = _bf(-_bmm(kdt, w))
    lhs_s[:, HEAD_DIM:HEAD_DIM + c, :] = _bf((q * egc).reshape(nc, c, HEAD_DIM) - _bmm(attn, w))
    n_s[...] = _bmm(kdt, u)
    op_s[...] = _bmm(attn, u)


def _dn_kernel(alog_ref, dtb_ref, q_ref, k_ref, v_ref, z_ref, s_ref, cq_ref, ck_ref, cv_ref, ng_ref,
               o_ref, lhs_s, n_s, op_s, egl_s, o_s):
    hs = DN_HEADS_PER_STEP
    seq = q_ref.shape[0]
    c = DN_CHUNK
    lanes = [slice(u * HEAD_DIM, (u + 1) * HEAD_DIM) for u in range(hs)]
    for u in range(hs):
        _dn_prepare(pl.program_id(1) * hs + u, lanes[u], alog_ref, dtb_ref, q_ref, k_ref, v_ref, s_ref,
                    cq_ref, ck_ref, cv_ref, lhs_s.at[u], n_s.at[u], op_s.at[u], egl_s.at[u])

    def chunk_step(n, states):
        rows = pl.ds(pl.multiple_of(n * c, c), c)
        out = []
        for u in range(hs):
            r = _dot(lhs_s[u, n], _bf(states[u]))
            o_s[u, rows, :] = r[HEAD_DIM:, :] + op_s[u, n]
            out.append(states[u] * egl_s[u, n][0:1, :] + r[0:HEAD_DIM, :] + n_s[u, n])
        return tuple(out)

    lax.fori_loop(0, seq // c, chunk_step, tuple(jnp.zeros((HEAD_DIM, HEAD_DIM), F32) for _ in range(hs)))

    for u in range(hs):
        o = o_s[u]
        y = o * lax.rsqrt(jnp.mean(o * o, axis=-1, keepdims=True) + EPS) * ng_ref[...]
        z = z_ref[:, lanes[u]]
        o_ref[:, lanes[u]] = (y * (z * _sigmoid(z))).astype(o_ref.dtype)


def _deltanet(proj, dn_conv, a_log, dt_bias, norm_g):
    bsz, seq, _ = proj.shape
    nc = seq // DN_CHUNK
    hs = DN_HEADS_PER_STEP
    width = hs * HEAD_DIM
    assert DN_HEADS % hs == 0 and all(cb % hs == 0 for cb in (CB_DN_Q, CB_DN_K, CB_DN_V, CB_DN_Z))
    col = lambda off: pl.BlockSpec((None, seq, width), lambda b, p: (b, 0, off // hs + p))
    cw = lambda off: pl.BlockSpec((dn_conv.shape[0], width), lambda b, p: (0, off // hs + p))
    smem = pl.BlockSpec(memory_space=pltpu.SMEM)
    return pl.pallas_call(
        _dn_kernel,
        grid=(bsz, DN_HEADS // hs),
        in_specs=[smem, smem, col(CB_DN_Q), col(CB_DN_K), col(CB_DN_V), col(CB_DN_Z),
                  pl.BlockSpec((None, seq, LANES), lambda b, p: (b, 0, CB_SMALL)),
                  cw(CB_DN_Q), cw(CB_DN_K), cw(CB_DN_V),
                  pl.BlockSpec((1, HEAD_DIM), lambda b, p: (0, 0))],
        out_specs=pl.BlockSpec((None, seq, width), lambda b, p: (b, 0, p)),
        out_shape=jax.ShapeDtypeStruct((bsz, seq, DN_HEADS * HEAD_DIM), BF16),
        scratch_shapes=[pltpu.VMEM((hs, nc, HEAD_DIM + DN_CHUNK, HEAD_DIM), BF16),
                        pltpu.VMEM((hs, nc, HEAD_DIM, HEAD_DIM), F32),
                        pltpu.VMEM((hs, nc, DN_CHUNK, HEAD_DIM), F32),
                        pltpu.VMEM((hs, nc, SUBLANES, LANES), F32),
                        pltpu.VMEM((hs, seq, HEAD_DIM), F32)],
        compiler_params=_params("parallel", "parallel"),
        name="deltanet",
    )(a_log, dt_bias, proj, proj, proj, proj, proj, dn_conv, dn_conv, dn_conv, norm_g.reshape(1, HEAD_DIM))


def _dil_kernel(tab_ref, q_ref, k_ref, v_ref, o_ref, tile_s, perm_s, back_s, nat_s):
    head = pl.program_id(0)
    seq = q_ref.shape[0]
    qb = Q_BLOCK
    scale = HEAD_DIM ** -0.5

    @pl.when(pl.program_id(1) == 0)
    def _():
        r = lax.broadcasted_iota(jnp.int32, (qb, 2 * qb), 0)
        cidx = lax.broadcasted_iota(jnp.int32, (qb, 2 * qb), 1)
        sub = qb + r - cidx
        for pat, (window, dil) in enumerate(DIL_PATTERNS):
            valid = (sub >= 0) & (sub <= window // dil)
            tile_s[pat] = jnp.where(valid, _bias_of_dist(jnp.maximum(sub, 0) * dil, tab_ref, head), NEG)

    st = DIL_PATTERNS[1][1]
    nb = seq // qb
    grp = seq // st
    per_class = grp // qb
    assert [d for _, d in DIL_PATTERNS] == [1, st, st * st] and all(w // d == qb for w, d in DIL_PATTERNS)
    assert grp // st == qb and per_class & (per_class - 1) == 0

    def level1_rows(rho):
        return pl.ds(rho, grp, stride=st), slice(rho * grp, (rho + 1) * grp)

    def level2_rows(c):
        rho, sigma = divmod(c, st)
        return pl.ds(rho * grp + sigma, qb, stride=st), slice(c * qb, (c + 1) * qb)

    for n, ref in enumerate((q_ref, k_ref, v_ref)):
        for rho in range(st):
            walk, dense = level1_rows(rho)
            perm_s[0, n, dense, :] = ref[walk, :]
        for c in range(st * st):
            walk, dense = level2_rows(c)
            perm_s[1, n, dense, :] = perm_s[0, n, walk, :]

    unit = lax.broadcasted_iota(jnp.int32, (nb, 1, 1), 0)

    def attend(pat, q, k, v, has_prev):
        q3 = _bf(q * scale).reshape(nb, qb, HEAD_DIM)
        k3 = _bf(k).reshape(nb, qb, HEAD_DIM)
        v3 = _bf(v).reshape(nb, qb, HEAD_DIM)
        tile = tile_s[pat]
        logits = [_bmm_nt(q3, k3) + tile[:, qb:]]
        values = [v3]
        if has_prev is not None:
            shift = lambda x: jnp.concatenate([jnp.zeros_like(x[:1]), x[:-1]], axis=0)
            logits.append(jnp.where(has_prev, _bmm_nt(q3, shift(k3)) + tile[:, :qb], NEG))
            values.append(shift(v3))
        m = functools.reduce(jnp.maximum, [jnp.max(s, axis=-1, keepdims=True) for s in logits])
        probs = [jnp.exp(s - m) for s in logits]
        l = sum(jnp.sum(p, axis=-1, keepdims=True) for p in probs)
        num = sum(_bmm(p, v_) for p, v_ in zip(probs, values))
        wide = lambda x: jnp.broadcast_to(x, (nb, qb, LANES)).reshape(seq, LANES)
        return num.reshape(seq, HEAD_DIM), wide(m), wide(l)

    res0 = attend(0, q_ref[...], k_ref[...], v_ref[...], unit >= 1)
    res1 = attend(1, perm_s[0, 0], perm_s[0, 1], perm_s[0, 2], (unit & (per_class - 1)) != 0)
    res2 = attend(2, perm_s[1, 0], perm_s[1, 1], perm_s[1, 2], None)

    for n in range(3):
        for c in range(st * st):
            walk, dense = level2_rows(c)
            back_s[n, walk, :] = res2[n][dense, :]
        for rho in range(st):
            walk, dense = level1_rows(rho)
            nat_s[1, n, walk, :] = back_s[n, dense, :]
            nat_s[0, n, walk, :] = res1[n][dense, :]

    parts = [res0, tuple(nat_s[0, n] for n in range(3)), tuple(nat_s[1, n] for n in range(3))]
    m_all = functools.reduce(jnp.maximum, [part[1] for part in parts])
    num = jnp.zeros((seq, HEAD_DIM), F32)
    den = jnp.zeros((seq, LANES), F32)
    for part_num, part_m, part_l in parts:
        wgt = jnp.exp(part_m - m_all)
        num = num + part_num * wgt
        den = den + part_l * wgt
    o_ref[...] = (num / den).astype(o_ref.dtype)


def _dilated(proj, dil_tab):
    bsz, seq, _ = proj.shape
    col = lambda off: pl.BlockSpec((None, seq, LANES), lambda h, b: (b, 0, off + h))
    return pl.pallas_call(
        _dil_kernel,
        grid=(DIL_HEADS, bsz),
        in_specs=[pl.BlockSpec(memory_space=pltpu.SMEM), col(CB_DIL_Q), col(CB_DIL_K), col(CB_DIL_V)],
        out_specs=pl.BlockSpec((None, seq, HEAD_DIM), lambda h, b: (b, 0, h)),
        out_shape=jax.ShapeDtypeStruct((bsz, seq, DIL_HEADS * HEAD_DIM), BF16),
        scratch_shapes=[pltpu.VMEM((len(DIL_PATTERNS), Q_BLOCK, 2 * Q_BLOCK), F32),
                        pltpu.VMEM((2, 3, seq, HEAD_DIM), F32),
                        pltpu.VMEM((3, seq, LANES), F32),
                        pltpu.VMEM((2, 3, seq, LANES), F32)],
        compiler_params=_params("arbitrary", "arbitrary"),
        name="dilated",
    )(dil_tab, proj, proj, proj)


def _cmp_kernel(x_ref, pos_ref, w1_ref, w2_ref, o_ref):
    nblk = o_ref.shape[0]
    half = CMP_LEN // 2
    ha = jnp.zeros((nblk, CMP_HIDDEN), F32)
    hb = jnp.zeros((nblk, CMP_HIDDEN), F32)
    for l in range(half):
        xl = x_ref[pl.ds(l, nblk, stride=CMP_STRIDE), :]
        wa = w1_ref[l * HEAD_DIM:(l + 1) * HEAD_DIM, :]
        wb = w1_ref[(half + l) * HEAD_DIM:(half + l + 1) * HEAD_DIM, :]
        ha = ha + _dot((xl + pos_ref[l:l + 1, :]).astype(BF16), wa)
        hb = hb + _dot((xl + pos_ref[half + l:half + l + 1, :]).astype(BF16), wb)
    hmid = ha + pltpu.roll(hb, nblk - 1, axis=0)
    hmid = 0.5 * hmid * (1.0 + jnp.tanh(math.sqrt(2.0 / math.pi) * (hmid + 0.044715 * hmid * hmid * hmid)))
    out = _dot(hmid.astype(BF16), w2_ref[...])
    rowi = lax.broadcasted_iota(jnp.int32, out.shape, 0)
    o_ref[...] = jnp.where(rowi < nblk - 1, out, 0.0)


def _compress(proj, cmp_pos, w1, w2):
    bsz, seq, _ = proj.shape
    nblk = seq // CMP_STRIDE
    return pl.pallas_call(
        _cmp_kernel,
        grid=(2, bsz, NSA_GROUPS),
        in_specs=[pl.BlockSpec((None, seq, LANES), lambda i, b, g: (b, 0, CB_NSA_KV + i * NSA_GROUPS + g)),
                  pl.BlockSpec((None, CMP_LEN, HEAD_DIM), lambda i, b, g: (i, 0, 0)),
                  pl.BlockSpec((None, CMP_LEN * HEAD_DIM, CMP_HIDDEN), lambda i, b, g: (i, 0, 0)),
                  pl.BlockSpec((None, CMP_HIDDEN, HEAD_DIM), lambda i, b, g: (i, 0, 0))],
        out_specs=pl.BlockSpec((None, None, None, nblk, HEAD_DIM), lambda i, b, g: (b, g, i, 0, 0)),
        out_shape=jax.ShapeDtypeStruct((bsz, NSA_GROUPS, 2, nblk, HEAD_DIM), F32),
        compiler_params=_params("parallel", "parallel", "parallel"),
        name="nsa_compress",
    )(proj, cmp_pos, w1, w2)


def _nsa_kernel(tab_ref, q0_ref, q1_ref, q2_ref, ks_ref, vs_ref, kw_ref, vw_ref, cmp_ref, s_ref,
                o_ref, strip_s, cbias_s, wtile_s, q_s, kv_s, oslc_s):
    grp = pl.program_id(0)
    step = pl.program_id(2)
    seq = q0_ref.shape[0]
    qb = Q_BLOCK
    nsub = o_ref.shape[0] // qb
    nqb = seq // qb
    n_slc = seq // SLC_BLOCK
    ncmp = cmp_ref.shape[1]
    nwin = WIN // qb
    win_keys = WIN + qb
    scale = HEAD_DIM ** -0.5
    q_refs = (q0_ref, q1_ref, q2_ref)
    head_rows = lambda x, rr: x[rr * qb:(rr + 1) * qb, :]

    @pl.when(step == 0)
    def _():
        for rr in range(NSA_REP):
            q_s[rr] = (q_refs[rr][...] * scale).astype(BF16)
        for n, ref in enumerate((ks_ref, vs_ref, kw_ref, vw_ref)):
            kv_s[n] = ref[...].astype(BF16)

    @pl.when(jnp.logical_and(pl.program_id(1) == 0, step == 0))
    def _():
        for rr in range(NSA_REP):
            head = DIL_HEADS + grp * NSA_REP + rr
            shape = strip_s.shape[1:]
            dist = (lax.broadcasted_iota(jnp.int32, shape, 0) - lax.broadcasted_iota(jnp.int32, shape, 1)
                    + (seq - qb))
            strip_s[rr] = jnp.where(dist >= 0, _bias_of_dist(jnp.maximum(dist, 0), tab_ref, head), NEG)
            shape = cbias_s.shape[1:]
            dist = (lax.broadcasted_iota(jnp.int32, shape, 0)
                    - (lax.broadcasted_iota(jnp.int32, shape, 1) * CMP_STRIDE + CMP_LEN - 1))
            cbias_s[rr] = jnp.where(dist >= 0, _bias_of_dist(jnp.maximum(dist, 0), tab_ref, head), NEG)
            shape = wtile_s.shape[1:]
            dist = lax.broadcasted_iota(jnp.int32, shape, 0) + WIN - lax.broadcasted_iota(jnp.int32, shape, 1)
            wtile_s[rr] = jnp.where((dist >= 0) & (dist < WIN),
                                    _bias_of_dist(jnp.clip(dist, 0, WIN), tab_ref, head), NEG)

    lane = lax.broadcasted_iota(jnp.int32, (1, LANES), 1)
    oj = lax.broadcasted_iota(jnp.int32, (n_slc, ncmp), 0) * SLC_BLOCK
    oc = lax.broadcasted_iota(jnp.int32, (n_slc, ncmp), 1) * CMP_STRIDE
    overlap_t = (jnp.maximum(jnp.minimum(oc + CMP_LEN, oj + SLC_BLOCK) - jnp.maximum(oc, oj), 0)
                 .astype(F32) / CMP_STRIDE)
    jrow = lax.broadcasted_iota(jnp.int32, (n_slc, qb), 0)
    tcol = lax.broadcasted_iota(jnp.int32, (n_slc, qb), 1)

    def softmax_rows(s):
        m = jnp.max(s, axis=-1, keepdims=True)
        p = jnp.exp(s - m)
        return p, jnp.sum(p, axis=-1, keepdims=True)

    def front(i):
        rows = pl.ds(pl.multiple_of(i * qb, qb), qb)
        q3 = jnp.concatenate([q_s[rr, rows, :] for rr in range(NSA_REP)], axis=0)
        sm = _sigmoid(s_ref[rows, :])

        def gate(rr, branch):
            ln = LANE_GATE + (grp * NSA_REP + rr) * 3 + branch
            return jnp.sum(jnp.where(lane == ln, sm, 0.0), axis=-1, keepdims=True)

        s_all = _dot_nt(q3, _bf(cmp_ref[0]))
        p_sum = jnp.zeros((qb, ncmp), F32)
        probs = []
        for rr in range(NSA_REP):
            cb = cbias_s[rr, rows, :]
            p, l = softmax_rows(head_rows(s_all, rr) + cb)
            p = jnp.where(cb > 0.5 * NEG, p, 0.0)
            p = p / jnp.maximum(jnp.sum(p, axis=-1, keepdims=True), 1e-30)
            p_sum = p_sum + p
            probs.append(_bf(p))
        o_cmp = _dot(jnp.concatenate(probs, axis=0), _bf(cmp_ref[1]))
        acc = [gate(rr, 0) * head_rows(o_cmp, rr) for rr in range(NSA_REP)]

        imp = _dot_nt(overlap_t, p_sum, precision=HIGHEST)
        cur = jnp.right_shift(i * qb + tcol, SLC_SHIFT)
        forced = (jrow == 0) | ((jrow <= cur) & (jrow > cur - 2))
        imp = jnp.where(forced, jnp.inf, jnp.where(jrow <= cur, imp, -jnp.inf))
        cnt = jnp.zeros((n_slc, qb), F32)
        for j2 in range(n_slc):
            other = imp[j2:j2 + 1, :]
            ge = jnp.where(other >= imp, 1.0, 0.0)
            gt = jnp.where(other > imp, 1.0, 0.0)
            cnt = cnt + jnp.where(jrow > j2, ge, gt)
        sel_t = jnp.where((cnt < SLC_TOPK) & (jrow <= cur), 1.0, 0.0)
        sel_t = jnp.concatenate([sel_t, jnp.zeros((LANES - n_slc, qb), F32)], axis=0)
        sel = _bf(sel_t.T)

        j0 = jnp.maximum(i - nwin, 0)
        krows = pl.ds(pl.multiple_of(j0 * qb, qb), win_keys)
        tile_start = pl.multiple_of((j0 - i + nwin) * qb, qb)
        s_all = _dot_nt(q3, kv_s[2, krows, :])
        probs, dens = [], []
        for rr in range(NSA_REP):
            p, l = softmax_rows(head_rows(s_all, rr) + wtile_s[rr, :, pl.ds(tile_start, win_keys)])
            probs.append(_bf(p))
            dens.append(l)
        pv = _dot(jnp.concatenate(probs, axis=0), kv_s[3, krows, :])
        acc = [acc[rr] + gate(rr, 2) * (head_rows(pv, rr) / dens[rr]) for rr in range(NSA_REP)]
        return q3, sel, acc, [gate(rr, 1) for rr in range(NSA_REP)]

    def selected(u, i, q3, sel):
        nk = (i + 1) * qb
        strip_start = (nqb - 1 - i) * qb
        s_all = _dot_nt(q3, kv_s[0, 0:nk, :])
        ej = lax.broadcasted_iota(jnp.int32, (LANES, nk), 0)
        ep = lax.broadcasted_iota(jnp.int32, (LANES, nk), 1)
        expand = jnp.where(ej == jnp.right_shift(ep, SLC_SHIFT), 1.0, 0.0).astype(BF16)
        keep = _dot(sel, expand) > 0.5
        probs, dens = [], []
        for rr in range(NSA_REP):
            s = head_rows(s_all, rr) + strip_s[rr, :, strip_start:strip_start + nk]
            p, l = softmax_rows(jnp.where(keep, s, NEG))
            probs.append(_bf(p))
            dens.append(l)
        pv = _dot(jnp.concatenate(probs, axis=0), kv_s[1, 0:nk, :])
        for rr in range(NSA_REP):
            oslc_s[u, rr] = head_rows(pv, rr) / dens[rr]

    fronts = [front(step * nsub + u) for u in range(nsub)]
    for s in range(nqb // nsub):
        @pl.when(step == s)
        def _(s=s):
            for u in range(nsub):
                selected(u, s * nsub + u, fronts[u][0], fronts[u][1])

    for u in range(nsub):
        _, _, acc, gate_slc = fronts[u]
        for rr in range(NSA_REP):
            out = acc[rr] + gate_slc[rr] * oslc_s[u, rr]
            o_ref[u * qb:(u + 1) * qb, rr * HEAD_DIM:(rr + 1) * HEAD_DIM] = out.astype(o_ref.dtype)


def _nsa(proj, cmp_kv, rel_bias):
    bsz, seq, _ = proj.shape
    ncmp = cmp_kv.shape[3]
    nqb = seq // Q_BLOCK
    strip_w = seq
    wtile_w = WIN + WIN + Q_BLOCK
    col = lambda fn: pl.BlockSpec((None, seq, LANES), lambda g, b, i: (b, 0, fn(g)))
    kv = lambda branch, which: col(lambda g: CB_NSA_KV + (branch * 2 + which) * NSA_GROUPS + g)
    return pl.pallas_call(
        _nsa_kernel,
        grid=(NSA_GROUPS, bsz, nqb // NSA_BLOCKS_PER_STEP),
        in_specs=[pl.BlockSpec(memory_space=pltpu.SMEM),
                  col(lambda g: CB_NSA_Q + g * NSA_REP), col(lambda g: CB_NSA_Q + g * NSA_REP + 1),
                  col(lambda g: CB_NSA_Q + g * NSA_REP + 2),
                  kv(1, 0), kv(1, 1), kv(2, 0), kv(2, 1),
                  pl.BlockSpec((None, None, 2, ncmp, HEAD_DIM), lambda g, b, i: (b, g, 0, 0, 0)),
                  pl.BlockSpec((None, seq, LANES), lambda g, b, i: (b, 0, CB_SMALL))],
        out_specs=pl.BlockSpec((None, NSA_BLOCKS_PER_STEP * Q_BLOCK, NSA_REP * HEAD_DIM),
                               lambda g, b, i: (b, i, g)),
        out_shape=jax.ShapeDtypeStruct((bsz, seq, NSA_GROUPS * NSA_REP * HEAD_DIM), BF16),
        scratch_shapes=[pltpu.VMEM((NSA_REP, Q_BLOCK, strip_w), F32),
                        pltpu.VMEM((NSA_REP, seq, ncmp), F32),
                        pltpu.VMEM((NSA_REP, Q_BLOCK, wtile_w), F32),
                        pltpu.VMEM((NSA_REP, seq, HEAD_DIM), BF16),
                        pltpu.VMEM((4, seq, HEAD_DIM), BF16),
                        pltpu.VMEM((NSA_BLOCKS_PER_STEP, NSA_REP, Q_BLOCK, HEAD_DIM), F32)],
        compiler_params=_params("arbitrary", "arbitrary", "arbitrary"),
        name="nsa_attention",
    )(rel_bias, proj, proj, proj, proj, proj, proj, proj, cmp_kv, proj)


IN_WIDE_A = 3072
IN_SMALL_A = 12
IN_WIDE_B = 3840
IN_SMALL_B = 18


def _w_in_prep_kernel(a_ref, b_ref, c_ref, o_ref):
    j = pl.program_id(0)
    depth = o_ref.shape[0]
    blk = o_ref.shape[1]
    n_a = IN_WIDE_A // blk
    n_b = IN_WIDE_B // blk
    s = IN_SMALL_A

    def put(lo, hi, src_ref, src_lo):
        for l in range(depth):
            o_ref[l, lo:hi, :] = src_ref[src_lo:src_lo + hi - lo, l, :].astype(o_ref.dtype)

    @pl.when(j < n_a)
    def _():
        put(0, blk, a_ref, 0)

    @pl.when(jnp.logical_and(j >= n_a, j < n_a + n_b))
    def _():
        put(0, blk - s, a_ref, s)
        put(blk - s, blk, b_ref, 0)

    @pl.when(j == n_a + n_b)
    def _():
        put(0, s, c_ref, 0)
        put(s, s + IN_SMALL_B, a_ref, s)
        o_ref[:, s + IN_SMALL_B:, :] = jnp.zeros((depth, blk - s - IN_SMALL_B, o_ref.shape[2]), o_ref.dtype)

    @pl.when(j > n_a + n_b)
    def _():
        o_ref[...] = jnp.zeros(o_ref.shape, o_ref.dtype)


def _prep_w_in(w):
    depth, d, cols = w.shape
    blk = LANES
    assert cols == IN_WIDE_A + IN_SMALL_A + IN_WIDE_B + IN_SMALL_B
    wt = jnp.transpose(w, (2, 0, 1))
    last = (cols - 1) // blk
    nxt = 2 * SUBLANES
    assert IN_SMALL_A <= nxt and blk % nxt == 0 and IN_WIDE_A % nxt == 0
    last_nxt = (cols - 1) // nxt
    return pl.pallas_call(
        _w_in_prep_kernel,
        grid=(PROJ_COLS // blk,),
        in_specs=[pl.BlockSpec((blk, depth, d), lambda j: (jnp.minimum(j, last), 0, 0)),
                  pl.BlockSpec((nxt, depth, d), lambda j: (jnp.minimum((j + 1) * (blk // nxt), last_nxt), 0, 0)),
                  pl.BlockSpec((nxt, depth, d), lambda j: (IN_WIDE_A // nxt, 0, 0))],
        out_specs=pl.BlockSpec((depth, blk, d), lambda j: (0, j, 0)),
        out_shape=jax.ShapeDtypeStruct((depth, PROJ_COLS, d), BF16),
        compiler_params=_params("parallel"),
        name="w_in_prep",
    )(wt, wt, wt)


_DENSE_TILES = {
    "in_proj": (1024, 1792),
    "out_proj": 512,
    "ffn_up": (2048, 512),
    "ffn_down": (1024, 512),
    "final_norm": 512,
}


def kernel(x, norm1_g, w_in, dn_conv, dn_a_log, dn_dt_bias, dn_norm_g, cmp_pos, cmp_w1, cmp_w2, w_out,
           norm2_g, ffn_up, ffn_conv, ffn_down, rel_bias, final_g):
    bsz, seq, d = x.shape
    depth = w_in.shape[0]
    w_in_b = _prep_w_in(w_in)
    cmp_w1_b, cmp_w2_b = cmp_w1.astype(BF16), cmp_w2.astype(BF16)
    w_out_b, ffn_up_b, ffn_down_b = w_out.astype(BF16), ffn_up.astype(BF16), ffn_down.astype(BF16)
    xf = x.reshape(bsz * seq, d)
    t = _DENSE_TILES
    for l in range(depth):
        proj = _norm_matmul(xf, norm1_g[l], w_in_b, l, *t["in_proj"], F32).reshape(bsz, seq, PROJ_COLS)
        o_dn = _deltanet(proj, dn_conv[l], dn_a_log[l], dn_dt_bias[l], dn_norm_g[l])
        o_dil = _dilated(proj, rel_bias)
        cmp_kv = _compress(proj, cmp_pos[l], cmp_w1_b[l], cmp_w2_b[l])
        o_nsa = _nsa(proj, cmp_kv, rel_bias)
        xf, hf = _out_proj(o_dn.reshape(bsz * seq, -1), o_dil.reshape(bsz * seq, -1),
                           o_nsa.reshape(bsz * seq, -1), w_out_b[l], xf, norm2_g[l], t["out_proj"])
        act = _ffn_up(hf, ffn_up_b, ffn_conv, l, seq, *t["ffn_up"])
        xf = _ffn_down(act, ffn_down_b, l, xf, *t["ffn_down"])
    return _rmsnorm(xf, final_g, t["final_norm"]).reshape(bsz, seq, d)
```

```python
import functools
import math

import numpy as np
import jax
import jax.numpy as jnp
from jax import lax
from jax.experimental import pallas as pl
from jax.experimental.pallas import tpu as pltpu

F32 = jnp.float32
BF16 = jnp.bfloat16
HIGHEST = lax.Precision.HIGHEST

LANES = 128
SUBLANES = 8
HEAD_DIM = 128
DN_HEADS = 6
DIL_HEADS = 4
NSA_GROUPS = 2
NSA_REP = 3
DN_CHUNK = 128
DN_SUB = 16
DN_HEADS_PER_STEP = 3
CMP_LEN = 32
CMP_STRIDE = 16
CMP_HIDDEN = 256
SLC_BLOCK = 64
SLC_SHIFT = 6
SLC_TOPK = 16
NSA_BLOCKS_PER_STEP = 8
WIN = 512
Q_BLOCK = 128
DIL_PATTERNS = ((128, 1), (512, 4), (2048, 16))
REL_BUCKETS = 32
REL_MAX_DIST = 2048
EPS = 1e-6
NEG = -1e30
VMEM_LIMIT = 56 * 1024 * 1024

CB_DN_Q, CB_DN_K, CB_DN_V, CB_DN_Z = 0, 6, 12, 18
CB_DIL_Q, CB_DIL_K, CB_DIL_V = 24, 28, 32
CB_NSA_Q = 36
CB_NSA_KV = 42
CB_SMALL = 54
PROJ_COLS = 7168
LANE_B, LANE_A, LANE_GATE = 0, 6, 12


def _bucket_thresholds():
    n = np.arange(0, REL_MAX_DIST + 1)
    max_exact = REL_BUCKETS // 2
    out = []
    for dt in (np.float32, np.float64):
        nf = np.maximum(n, 1).astype(dt)
        large = max_exact + (np.log(nf / dt(max_exact)) / dt(math.log(REL_MAX_DIST / max_exact))
                             * dt(REL_BUCKETS - max_exact)).astype(np.int32)
        large = np.minimum(large, REL_BUCKETS - 1)
        out.append(np.where(n < max_exact, n, large))
    assert np.array_equal(out[0], out[1])
    bucket = out[1]
    assert np.all(np.diff(bucket) >= 0)
    thr = [0]
    for b in range(1, REL_BUCKETS):
        idx = np.nonzero(bucket >= b)[0]
        thr.append(int(idx[0]) if idx.size else REL_MAX_DIST + 1)
    return tuple(thr)


_THR = _bucket_thresholds()


def _bias_of_dist(dist, tab_ref, head):
    val = jnp.full(dist.shape, tab_ref[head, 0], F32)
    for b in range(1, REL_BUCKETS):
        val = jnp.where(dist >= _THR[b], tab_ref[head, b], val)
    return val


def _sigmoid(x):
    return 1.0 / (1.0 + jnp.exp(-x))


def _bf(x):
    return x.astype(BF16)


def _dot(a, b, **kw):
    return jnp.dot(a, b, preferred_element_type=F32, **kw)


def _dot_nt(a, b, **kw):
    return lax.dot_general(a, b, (((1,), (1,)), ((), ())), preferred_element_type=F32, **kw)


def _bmm(a, b):
    return jnp.einsum('nij,njk->nik', _bf(a), _bf(b), preferred_element_type=F32)


def _bmm_nt(a, b):
    return jnp.einsum('nid,njd->nij', _bf(a), _bf(b), preferred_element_type=F32)


def _params(*sem):
    return pltpu.CompilerParams(dimension_semantics=sem, vmem_limit_bytes=VMEM_LIMIT)


def _norm_matmul_kernel(x_ref, g_ref, w_ref, o_ref, h_ref):
    @pl.when(pl.program_id(1) == 0)
    def _():
        x = x_ref[...]
        ms = jnp.mean(x * x, axis=-1, keepdims=True)
        h_ref[...] = (x * lax.rsqrt(ms + EPS) * g_ref[...]).astype(BF16)

    o_ref[...] = _dot_nt(h_ref[...], w_ref[...]).astype(o_ref.dtype)


def _norm_matmul(x, g, w_t, layer, tm, tn, out_dtype):
    m, d = x.shape
    n = w_t.shape[1]
    return pl.pallas_call(
        _norm_matmul_kernel,
        grid=(m // tm, n // tn),
        in_specs=[pl.BlockSpec((tm, d), lambda i, j: (i, 0)),
                  pl.BlockSpec((1, d), lambda i, j: (0, 0)),
                  pl.BlockSpec((None, tn, d), lambda i, j: (layer, j, 0))],
        out_specs=pl.BlockSpec((tm, tn), lambda i, j: (i, j)),
        out_shape=jax.ShapeDtypeStruct((m, n), out_dtype),
        scratch_shapes=[pltpu.VMEM((tm, d), BF16)],
        compiler_params=_params("parallel", "arbitrary"),
        name="in_proj",
    )(x, g.reshape(1, d), w_t)


def _out_proj_kernel(a_ref, b_ref, c_ref, wa_ref, wb_ref, wc_ref, x_ref, o_ref):
    acc = _dot(a_ref[...], wa_ref[...])
    acc = acc + _dot(b_ref[...], wb_ref[...])
    acc = acc + _dot(c_ref[...], wc_ref[...])
    o_ref[...] = x_ref[...] + acc


def _out_proj(o_dn, o_dil, o_nsa, w_out, x, tm):
    m, d = x.shape
    ka, kb, kc = o_dn.shape[1], o_dil.shape[1], o_nsa.shape[1]
    wa, wb, wc = w_out[:ka], w_out[ka:ka + kb], w_out[ka + kb:]
    row = lambda i: (i, 0)
    full = lambda i: (0, 0)
    return pl.pallas_call(
        _out_proj_kernel,
        grid=(m // tm,),
        in_specs=[pl.BlockSpec((tm, ka), row), pl.BlockSpec((tm, kb), row), pl.BlockSpec((tm, kc), row),
                  pl.BlockSpec((ka, d), full), pl.BlockSpec((kb, d), full), pl.BlockSpec((kc, d), full),
                  pl.BlockSpec((tm, d), row)],
        out_specs=pl.BlockSpec((tm, d), row),
        out_shape=jax.ShapeDtypeStruct((m, d), F32),
        compiler_params=_params("parallel"),
        name="out_proj",
    )(o_dn, o_dil, o_nsa, wa, wb, wc, x)


def _ffn_up_kernel(x_ref, xp_ref, g_ref, wg_ref, wv_ref, cg_ref, cv_ref, o_ref, h_ref, *, tiles_per_seq):
    tm = x_ref.shape[0]
    halo = xp_ref.shape[0]

    @pl.when(pl.program_id(1) == 0)
    def _():
        def norm(x):
            ms = jnp.mean(x * x, axis=-1, keepdims=True)
            return (x * lax.rsqrt(ms + EPS) * g_ref[...]).astype(BF16)
        first = (pl.program_id(0) % tiles_per_seq) == 0
        hp = norm(xp_ref[...])
        h_ref[0:halo, :] = jnp.where(first, jnp.zeros_like(hp), hp)
        h_ref[halo:halo + tm, :] = norm(x_ref[...])

    h = h_ref[...]

    def conv(w_ref, c_ref):
        u = _dot(h, w_ref[...])
        y = u * c_ref[2:3, :]
        y = y + pltpu.roll(u, 1, axis=0) * c_ref[1:2, :]
        y = y + pltpu.roll(u, 2, axis=0) * c_ref[0:1, :]
        return y[halo:, :]

    gate = conv(wg_ref, cg_ref)
    val = conv(wv_ref, cv_ref)
    o_ref[...] = (gate * _sigmoid(gate) * val).astype(o_ref.dtype)


def _ffn_up(x, g, w_up, conv_w, layer, seq, tm, tn):
    m, d = x.shape
    f = w_up.shape[2] // 2
    taps = conv_w.shape[1]
    halo = SUBLANES
    assert taps - 1 <= halo
    nf = f // tn
    kern = functools.partial(_ffn_up_kernel, tiles_per_seq=seq // tm)
    return pl.pallas_call(
        kern,
        grid=(m // tm, nf),
        in_specs=[pl.BlockSpec((tm, d), lambda i, j: (i, 0)),
                  pl.BlockSpec((halo, d), lambda i, j: (jnp.maximum(i * (tm // halo) - 1, 0), 0)),
                  pl.BlockSpec((1, d), lambda i, j: (0, 0)),
                  pl.BlockSpec((None, d, tn), lambda i, j: (layer, 0, j)),
                  pl.BlockSpec((None, d, tn), lambda i, j: (layer, 0, j + nf)),
                  pl.BlockSpec((None, taps, tn), lambda i, j: (layer, 0, j)),
                  pl.BlockSpec((None, taps, tn), lambda i, j: (layer, 0, j + nf))],
        out_specs=pl.BlockSpec((tm, tn), lambda i, j: (i, j)),
        out_shape=jax.ShapeDtypeStruct((m, f), BF16),
        scratch_shapes=[pltpu.VMEM((tm + halo, d), BF16)],
        compiler_params=_params("parallel", "arbitrary"),
        name="ffn_up",
    )(x, x, g.reshape(1, d), w_up, w_up, conv_w, conv_w)


def _ffn_down_kernel(a_ref, w_ref, x_ref, o_ref):
    o_ref[...] = x_ref[...] + _dot(a_ref[...], w_ref[...])


def _ffn_down(act, w_down, layer, x, tm, tn):
    m, f = act.shape
    d = w_down.shape[2]
    return pl.pallas_call(
        _ffn_down_kernel,
        grid=(d // tn, m // tm),
        in_specs=[pl.BlockSpec((tm, f), lambda j, i: (i, 0)),
                  pl.BlockSpec((None, f, tn), lambda j, i: (layer, 0, j)),
                  pl.BlockSpec((tm, tn), lambda j, i: (i, j))],
        out_specs=pl.BlockSpec((tm, tn), lambda j, i: (i, j)),
        out_shape=jax.ShapeDtypeStruct((m, d), F32),
        compiler_params=_params("parallel", "parallel"),
        name="ffn_down",
    )(act, w_down, x)


def _rmsnorm_kernel(x_ref, g_ref, o_ref):
    x = x_ref[...]
    ms = jnp.mean(x * x, axis=-1, keepdims=True)
    o_ref[...] = x * lax.rsqrt(ms + EPS) * g_ref[...]


def _rmsnorm(x, g, tm):
    m, d = x.shape
    return pl.pallas_call(
        _rmsnorm_kernel,
        grid=(m // tm,),
        in_specs=[pl.BlockSpec((tm, d), lambda i: (i, 0)), pl.BlockSpec((1, d), lambda i: (0, 0))],
        out_specs=pl.BlockSpec((tm, d), lambda i: (i, 0)),
        out_shape=jax.ShapeDtypeStruct((m, d), F32),
        compiler_params=_params("parallel"),
        name="final_norm",
    )(x, g.reshape(1, d))


def _dn_prepare(h, ls, alog_ref, dtb_ref, q_ref, k_ref, v_ref, s_ref, cq_ref, ck_ref, cv_ref,
                lhs_s, n_s, op_s, egl_s):
    seq = q_ref.shape[0]
    c = DN_CHUNK
    nc = seq // c

    def conv_silu(x_ref, c_ref):
        taps = c_ref.shape[0]

        def conv(x, mask_rows):
            acc = x * c_ref[taps - 1:taps, ls]
            for s in range(1, taps):
                xs = pltpu.roll(x, s, axis=0)
                if mask_rows is not None:
                    xs = jnp.where(mask_rows >= s, xs, 0.0)
                acc = acc + xs * c_ref[taps - 1 - s:taps - s, ls]
            return acc

        x = x_ref[:, ls]
        first = conv(x[0:SUBLANES, :], lax.broadcasted_iota(jnp.int32, (SUBLANES, 1), 0))
        acc = jnp.concatenate([first, conv(x, None)[SUBLANES:, :]], axis=0)
        return acc * _sigmoid(acc)

    def l2norm(x):
        return x * lax.rsqrt(jnp.sum(x * x, axis=-1, keepdims=True) + EPS)

    q = l2norm(conv_silu(q_ref, cq_ref)) * (HEAD_DIM ** -0.5)
    k = l2norm(conv_silu(k_ref, ck_ref))
    v = conv_silu(v_ref, cv_ref)

    lane = lax.broadcasted_iota(jnp.int32, (1, LANES), 1)
    sm = s_ref[...]
    b_col = jnp.sum(jnp.where(lane == LANE_B + h, sm, 0.0), axis=-1, keepdims=True)
    a_col = jnp.sum(jnp.where(lane == LANE_A + h, sm, 0.0), axis=-1, keepdims=True)
    beta = _sigmoid(b_col)
    ap = a_col + dtb_ref[h]
    softplus = jnp.maximum(ap, 0.0) + jnp.log1p(jnp.exp(-jnp.abs(ap)))
    neg_rate = -jnp.exp(jnp.full((1, 1), alog_ref[h], F32))
    g = jnp.broadcast_to(neg_rate * softplus, (seq, LANES))

    ii = lax.broadcasted_iota(jnp.int32, (c, c), 0)
    jj = lax.broadcasted_iota(jnp.int32, (c, c), 1)
    tri = jnp.broadcast_to(jnp.where(ii >= jj, 1.0, 0.0).astype(F32), (nc, c, c))
    gc3 = jnp.einsum('nij,njk->nik', tri, g.reshape(nc, c, LANES),
                     preferred_element_type=F32, precision=HIGHEST)
    gc = gc3.reshape(seq, LANES)
    egc = jnp.exp(gc)
    glast3 = jnp.broadcast_to(gc3[:, c - 1:c, :], (nc, c, LANES))
    kb = k * beta
    k3 = k.reshape(nc, c, HEAD_DIM)
    kdt = jnp.swapaxes(k3 * jnp.exp(glast3 - gc3), 1, 2)
    egl_s[...] = jnp.exp(glast3[:, 0:SUBLANES, :])

    gcol = gc3[:, :, 0:c]
    grow = jnp.swapaxes(gc3, 1, 2)[:, 0:c, :]
    decay = jnp.where(ii >= jj, jnp.exp(jnp.minimum(gcol - grow, 0.0)), 0.0)

    attn = _bmm_nt(q.reshape(nc, c, HEAD_DIM), k3) * decay
    a_mat = jnp.where(ii > jj, _bmm_nt(kb.reshape(nc, c, HEAD_DIM), k3) * decay, 0.0)

    same = (ii & -DN_SUB) == (jj & -DN_SUB)
    eye = jnp.where(ii == jj, 1.0, 0.0).astype(F32)
    x1 = jnp.where(same, -a_mat, 0.0)
    a_off = jnp.where(same, 0.0, a_mat)
    t_d = eye + x1
    xp = x1
    p = 2
    while p < DN_SUB:
        xp = _bmm(xp, xp)
        t_d = t_d + _bmm(t_d, xp)
        p *= 2
    y1 = -_bmm(t_d, a_off)
    t_o = eye + y1
    yp = y1
    p = 2
    while p < c // DN_SUB:
        yp = _bmm(yp, yp)
        t_o = t_o + _bmm(t_o, yp)
        p *= 2
    t_mat = _bmm(t_o, t_d)

    u = _bmm(t_mat, (v * beta).reshape(nc, c, HEAD_DIM))
    w = _bmm(t_mat, (kb * egc).reshape(nc, c, HEAD_DIM))

    lhs_s[:, 0:HEAD_DIM, :] = _bf(-_bmm(kdt, w))
    lhs_s[:, HEAD_DIM:HEAD_DIM + c, :] = _bf((q * egc).reshape(nc, c, HEAD_DIM) - _bmm(attn, w))
    n_s[...] = _bmm(kdt, u)
    op_s[...] = _bmm(attn, u)


def _dn_kernel(alog_ref, dtb_ref, q_ref, k_ref, v_ref, z_ref, s_ref, cq_ref, ck_ref, cv_ref, ng_ref,
               o_ref, lhs_s, n_s, op_s, egl_s, o_s):
    hs = DN_HEADS_PER_STEP
    seq = q_ref.shape[0]
    c = DN_CHUNK
    lanes = [slice(u * HEAD_DIM, (u + 1) * HEAD_DIM) for u in range(hs)]
    for u in range(hs):
        _dn_prepare(pl.program_id(1) * hs + u, lanes[u], alog_ref, dtb_ref, q_ref, k_ref, v_ref, s_ref,
                    cq_ref, ck_ref, cv_ref, lhs_s.at[u], n_s.at[u], op_s.at[u], egl_s.at[u])

    def chunk_step(n, states):
        rows = pl.ds(pl.multiple_of(n * c, c), c)
        out = []
        for u in range(hs):
            r = _dot(lhs_s[u, n], _bf(states[u]))
            o_s[u, rows, :] = r[HEAD_DIM:, :] + op_s[u, n]
            out.append(states[u] * egl_s[u, n][0:1, :] + r[0:HEAD_DIM, :] + n_s[u, n])
        return tuple(out)

    lax.fori_loop(0, seq // c, chunk_step, tuple(jnp.zeros((HEAD_DIM, HEAD_DIM), F32) for _ in range(hs)))

    for u in range(hs):
        o = o_s[u]
        y = o * lax.rsqrt(jnp.mean(o * o, axis=-1, keepdims=True) + EPS) * ng_ref[...]
        z = z_ref[:, lanes[u]]
        o_ref[:, lanes[u]] = (y * (z * _sigmoid(z))).astype(o_ref.dtype)


def _deltanet(proj, dn_conv, a_log, dt_bias, norm_g):
    bsz, seq, _ = proj.shape
    nc = seq // DN_CHUNK
    hs = DN_HEADS_PER_STEP
    width = hs * HEAD_DIM
    assert DN_HEADS % hs == 0 and all(cb % hs == 0 for cb in (CB_DN_Q, CB_DN_K, CB_DN_V, CB_DN_Z))
    col = lambda off: pl.BlockSpec((None, seq, width), lambda b, p: (b, 0, off // hs + p))
    cw = lambda off: pl.BlockSpec((dn_conv.shape[0], width), lambda b, p: (0, off // hs + p))
    smem = pl.BlockSpec(memory_space=pltpu.SMEM)
    return pl.pallas_call(
        _dn_kernel,
        grid=(bsz, DN_HEADS // hs),
        in_specs=[smem, smem, col(CB_DN_Q), col(CB_DN_K), col(CB_DN_V), col(CB_DN_Z),
                  pl.BlockSpec((None, seq, LANES), lambda b, p: (b, 0, CB_SMALL)),
                  cw(CB_DN_Q), cw(CB_DN_K), cw(CB_DN_V),
                  pl.BlockSpec((1, HEAD_DIM), lambda b, p: (0, 0))],
        out_specs=pl.BlockSpec((None, seq, width), lambda b, p: (b, 0, p)),
        out_shape=jax.ShapeDtypeStruct((bsz, seq, DN_HEADS * HEAD_DIM), BF16),
        scratch_shapes=[pltpu.VMEM((hs, nc, HEAD_DIM + DN_CHUNK, HEAD_DIM), BF16),
                        pltpu.VMEM((hs, nc, HEAD_DIM, HEAD_DIM), F32),
                        pltpu.VMEM((hs, nc, DN_CHUNK, HEAD_DIM), F32),
                        pltpu.VMEM((hs, nc, SUBLANES, LANES), F32),
                        pltpu.VMEM((hs, seq, HEAD_DIM), F32)],
        compiler_params=_params("parallel", "parallel"),
        name="deltanet",
    )(a_log, dt_bias, proj, proj, proj, proj, proj, dn_conv, dn_conv, dn_conv, norm_g.reshape(1, HEAD_DIM))


def _dil_kernel(tab_ref, q_ref, k_ref, v_ref, o_ref, tile_s, perm_s, back_s, nat_s):
    head = pl.program_id(0)
    seq = q_ref.shape[0]
    qb = Q_BLOCK
    scale = HEAD_DIM ** -0.5

    @pl.when(pl.program_id(1) == 0)
    def _():
        r = lax.broadcasted_iota(jnp.int32, (qb, 2 * qb), 0)
        cidx = lax.broadcasted_iota(jnp.int32, (qb, 2 * qb), 1)
        sub = qb + r - cidx
        for pat, (window, dil) in enumerate(DIL_PATTERNS):
            valid = (sub >= 0) & (sub <= window // dil)
            tile_s[pat] = jnp.where(valid, _bias_of_dist(jnp.maximum(sub, 0) * dil, tab_ref, head), NEG)

    st = DIL_PATTERNS[1][1]
    nb = seq // qb
    grp = seq // st
    per_class = grp // qb
    assert [d for _, d in DIL_PATTERNS] == [1, st, st * st] and all(w // d == qb for w, d in DIL_PATTERNS)
    assert grp // st == qb and per_class & (per_class - 1) == 0

    def level1_rows(rho):
        return pl.ds(rho, grp, stride=st), slice(rho * grp, (rho + 1) * grp)

    def level2_rows(c):
        rho, sigma = divmod(c, st)
        return pl.ds(rho * grp + sigma, qb, stride=st), slice(c * qb, (c + 1) * qb)

    for n, ref in enumerate((q_ref, k_ref, v_ref)):
        for rho in range(st):
            walk, dense = level1_rows(rho)
            perm_s[0, n, dense, :] = ref[walk, :]
        for c in range(st * st):
            walk, dense = level2_rows(c)
            perm_s[1, n, dense, :] = perm_s[0, n, walk, :]

    unit = lax.broadcasted_iota(jnp.int32, (nb, 1, 1), 0)

    def attend(pat, q, k, v, has_prev):
        q3 = _bf(q * scale).reshape(nb, qb, HEAD_DIM)
        k3 = _bf(k).reshape(nb, qb, HEAD_DIM)
        v3 = _bf(v).reshape(nb, qb, HEAD_DIM)
        tile = tile_s[pat]
        logits = [_bmm_nt(q3, k3) + tile[:, qb:]]
        values = [v3]
        if has_prev is not None:
            shift = lambda x: jnp.concatenate([jnp.zeros_like(x[:1]), x[:-1]], axis=0)
            logits.append(jnp.where(has_prev, _bmm_nt(q3, shift(k3)) + tile[:, :qb], NEG))
            values.append(shift(v3))
        m = functools.reduce(jnp.maximum, [jnp.max(s, axis=-1, keepdims=True) for s in logits])
        probs = [jnp.exp(s - m) for s in logits]
        l = sum(jnp.sum(p, axis=-1, keepdims=True) for p in probs)
        num = sum(_bmm(p, v_) for p, v_ in zip(probs, values))
        wide = lambda x: jnp.broadcast_to(x, (nb, qb, LANES)).reshape(seq, LANES)
        return num.reshape(seq, HEAD_DIM), wide(m), wide(l)

    res0 = attend(0, q_ref[...], k_ref[...], v_ref[...], unit >= 1)
    res1 = attend(1, perm_s[0, 0], perm_s[0, 1], perm_s[0, 2], (unit & (per_class - 1)) != 0)
    res2 = attend(2, perm_s[1, 0], perm_s[1, 1], perm_s[1, 2], None)

    for n in range(3):
        for c in range(st * st):
            walk, dense = level2_rows(c)
            back_s[n, walk, :] = res2[n][dense, :]
        for rho in range(st):
            walk, dense = level1_rows(rho)
            nat_s[1, n, walk, :] = back_s[n, dense, :]
            nat_s[0, n, walk, :] = res1[n][dense, :]

    parts = [res0, tuple(nat_s[0, n] for n in range(3)), tuple(nat_s[1, n] for n in range(3))]
    m_all = functools.reduce(jnp.maximum, [part[1] for part in parts])
    num = jnp.zeros((seq, HEAD_DIM), F32)
    den = jnp.zeros((seq, LANES), F32)
    for part_num, part_m, part_l in parts:
        wgt = jnp.exp(part_m - m_all)
        num = num + part_num * wgt
        den = den + part_l * wgt
    o_ref[...] = (num / den).astype(o_ref.dtype)


def _dilated(proj, dil_tab):
    bsz, seq, _ = proj.shape
    col = lambda off: pl.BlockSpec((None, seq, LANES), lambda h, b: (b, 0, off + h))
    return pl.pallas_call(
        _dil_kernel,
        grid=(DIL_HEADS, bsz),
        in_specs=[pl.BlockSpec(memory_space=pltpu.SMEM), col(CB_DIL_Q), col(CB_DIL_K), col(CB_DIL_V)],
        out_specs=pl.BlockSpec((None, seq, HEAD_DIM), lambda h, b: (b, 0, h)),
        out_shape=jax.ShapeDtypeStruct((bsz, seq, DIL_HEADS * HEAD_DIM), BF16),
        scratch_shapes=[pltpu.VMEM((len(DIL_PATTERNS), Q_BLOCK, 2 * Q_BLOCK), F32),
                        pltpu.VMEM((2, 3, seq, HEAD_DIM), F32),
                        pltpu.VMEM((3, seq, LANES), F32),
                        pltpu.VMEM((2, 3, seq, LANES), F32)],
        compiler_params=_params("arbitrary", "arbitrary"),
        name="dilated",
    )(dil_tab, proj, proj, proj)


def _cmp_kernel(x_ref, pos_ref, w1_ref, w2_ref, o_ref):
    nblk = o_ref.shape[0]
    half = CMP_LEN // 2
    ha = jnp.zeros((nblk, CMP_HIDDEN), F32)
    hb = jnp.zeros((nblk, CMP_HIDDEN), F32)
    for l in range(half):
        xl = x_ref[pl.ds(l, nblk, stride=CMP_STRIDE), :]
        wa = w1_ref[l * HEAD_DIM:(l + 1) * HEAD_DIM, :]
        wb = w1_ref[(half + l) * HEAD_DIM:(half + l + 1) * HEAD_DIM, :]
        ha = ha + _dot((xl + pos_ref[l:l + 1, :]).astype(BF16), wa)
        hb = hb + _dot((xl + pos_ref[half + l:half + l + 1, :]).astype(BF16), wb)
    hmid = ha + pltpu.roll(hb, nblk - 1, axis=0)
    hmid = 0.5 * hmid * (1.0 + jnp.tanh(math.sqrt(2.0 / math.pi) * (hmid + 0.044715 * hmid * hmid * hmid)))
    out = _dot(hmid.astype(BF16), w2_ref[...])
    rowi = lax.broadcasted_iota(jnp.int32, out.shape, 0)
    o_ref[...] = jnp.where(rowi < nblk - 1, out, 0.0)


def _compress(proj, cmp_pos, w1, w2):
    bsz, seq, _ = proj.shape
    nblk = seq // CMP_STRIDE
    return pl.pallas_call(
        _cmp_kernel,
        grid=(2, bsz, NSA_GROUPS),
        in_specs=[pl.BlockSpec((None, seq, LANES), lambda i, b, g: (b, 0, CB_NSA_KV + i * NSA_GROUPS + g)),
                  pl.BlockSpec((None, CMP_LEN, HEAD_DIM), lambda i, b, g: (i, 0, 0)),
                  pl.BlockSpec((None, CMP_LEN * HEAD_DIM, CMP_HIDDEN), lambda i, b, g: (i, 0, 0)),
                  pl.BlockSpec((None, CMP_HIDDEN, HEAD_DIM), lambda i, b, g: (i, 0, 0))],
        out_specs=pl.BlockSpec((None, None, None, nblk, HEAD_DIM), lambda i, b, g: (b, g, i, 0, 0)),
        out_shape=jax.ShapeDtypeStruct((bsz, NSA_GROUPS, 2, nblk, HEAD_DIM), F32),
        compiler_params=_params("parallel", "parallel", "parallel"),
        name="nsa_compress",
    )(proj, cmp_pos, w1, w2)


def _nsa_kernel(tab_ref, q0_ref, q1_ref, q2_ref, ks_ref, vs_ref, kw_ref, vw_ref, cmp_ref, s_ref,
                o_ref, strip_s, cbias_s, wtile_s, q_s, kv_s, oslc_s):
    grp = pl.program_id(0)
    step = pl.program_id(2)
    seq = q0_ref.shape[0]
    qb = Q_BLOCK
    nsub = o_ref.shape[0] // qb
    nqb = seq // qb
    n_slc = seq // SLC_BLOCK
    ncmp = cmp_ref.shape[1]
    nwin = WIN // qb
    win_keys = WIN + qb
    scale = HEAD_DIM ** -0.5
    q_refs = (q0_ref, q1_ref, q2_ref)
    head_rows = lambda x, rr: x[rr * qb:(rr + 1) * qb, :]

    @pl.when(step == 0)
    def _():
        for rr in range(NSA_REP):
            q_s[rr] = (q_refs[rr][...] * scale).astype(BF16)
        for n, ref in enumerate((ks_ref, vs_ref, kw_ref, vw_ref)):
            kv_s[n] = ref[...].astype(BF16)

    @pl.when(jnp.logical_and(pl.program_id(1) == 0, step == 0))
    def _():
        for rr in range(NSA_REP):
            head = DIL_HEADS + grp * NSA_REP + rr
            shape = strip_s.shape[1:]
            dist = (lax.broadcasted_iota(jnp.int32, shape, 0) - lax.broadcasted_iota(jnp.int32, shape, 1)
                    + (seq - qb))
            strip_s[rr] = jnp.where(dist >= 0, _bias_of_dist(jnp.maximum(dist, 0), tab_ref, head), NEG)
            shape = cbias_s.shape[1:]
            dist = (lax.broadcasted_iota(jnp.int32, shape, 0)
                    - (lax.broadcasted_iota(jnp.int32, shape, 1) * CMP_STRIDE + CMP_LEN - 1))
            cbias_s[rr] = jnp.where(dist >= 0, _bias_of_dist(jnp.maximum(dist, 0), tab_ref, head), NEG)
            shape = wtile_s.shape[1:]
            dist = lax.broadcasted_iota(jnp.int32, shape, 0) + WIN - lax.broadcasted_iota(jnp.int32, shape, 1)
            wtile_s[rr] = jnp.where((dist >= 0) & (dist < WIN),
                                    _bias_of_dist(jnp.clip(dist, 0, WIN), tab_ref, head), NEG)

    lane = lax.broadcasted_iota(jnp.int32, (1, LANES), 1)
    oj = lax.broadcasted_iota(jnp.int32, (n_slc, ncmp), 0) * SLC_BLOCK
    oc = lax.broadcasted_iota(jnp.int32, (n_slc, ncmp), 1) * CMP_STRIDE
    overlap_t = (jnp.maximum(jnp.minimum(oc + CMP_LEN, oj + SLC_BLOCK) - jnp.maximum(oc, oj), 0)
                 .astype(F32) / CMP_STRIDE)
    jrow = lax.broadcasted_iota(jnp.int32, (n_slc, qb), 0)
    tcol = lax.broadcasted_iota(jnp.int32, (n_slc, qb), 1)

    def softmax_rows(s):
        m = jnp.max(s, axis=-1, keepdims=True)
        p = jnp.exp(s - m)
        return p, jnp.sum(p, axis=-1, keepdims=True)

    def front(i):
        rows = pl.ds(pl.multiple_of(i * qb, qb), qb)
        q3 = jnp.concatenate([q_s[rr, rows, :] for rr in range(NSA_REP)], axis=0)
        sm = _sigmoid(s_ref[rows, :])

        def gate(rr, branch):
            ln = LANE_GATE + (grp * NSA_REP + rr) * 3 + branch
            return jnp.sum(jnp.where(lane == ln, sm, 0.0), axis=-1, keepdims=True)

        s_all = _dot_nt(q3, _bf(cmp_ref[0]))
        p_sum = jnp.zeros((qb, ncmp), F32)
        probs = []
        for rr in range(NSA_REP):
            cb = cbias_s[rr, rows, :]
            p, l = softmax_rows(head_rows(s_all, rr) + cb)
            p = jnp.where(cb > 0.5 * NEG, p, 0.0)
            p = p / jnp.maximum(jnp.sum(p, axis=-1, keepdims=True), 1e-30)
            p_sum = p_sum + p
            probs.append(_bf(p))
        o_cmp = _dot(jnp.concatenate(probs, axis=0), _bf(cmp_ref[1]))
        acc = [gate(rr, 0) * head_rows(o_cmp, rr) for rr in range(NSA_REP)]

        imp = _dot_nt(overlap_t, p_sum, precision=HIGHEST)
        cur = jnp.right_shift(i * qb + tcol, SLC_SHIFT)
        forced = (jrow == 0) | ((jrow <= cur) & (jrow > cur - 2))
        imp = jnp.where(forced, jnp.inf, jnp.where(jrow <= cur, imp, -jnp.inf))
        cnt = jnp.zeros((n_slc, qb), F32)
        for j2 in range(n_slc):
            other = imp[j2:j2 + 1, :]
            ge = jnp.where(other >= imp, 1.0, 0.0)
            gt = jnp.where(other > imp, 1.0, 0.0)
            cnt = cnt + jnp.where(jrow > j2, ge, gt)
        sel_t = jnp.where((cnt < SLC_TOPK) & (jrow <= cur), 1.0, 0.0)
        sel_t = jnp.concatenate([sel_t, jnp.zeros((LANES - n_slc, qb), F32)], axis=0)
        sel = _bf(sel_t.T)

        j0 = jnp.maximum(i - nwin, 0)
        krows = pl.ds(pl.multiple_of(j0 * qb, qb), win_keys)
        tile_start = pl.multiple_of((j0 - i + nwin) * qb, qb)
        s_all = _dot_nt(q3, kv_s[2, krows, :])
        probs, dens = [], []
        for rr in range(NSA_REP):
            p, l = softmax_rows(head_rows(s_all, rr) + wtile_s[rr, :, pl.ds(tile_start, win_keys)])
            probs.append(_bf(p))
            dens.append(l)
        pv = _dot(jnp.concatenate(probs, axis=0), kv_s[3, krows, :])
        acc = [acc[rr] + gate(rr, 2) * (head_rows(pv, rr) / dens[rr]) for rr in range(NSA_REP)]
        return q3, sel, acc, [gate(rr, 1) for rr in range(NSA_REP)]

    def selected(u, i, q3, sel):
        nk = (i + 1) * qb
        strip_start = (nqb - 1 - i) * qb
        s_all = _dot_nt(q3, kv_s[0, 0:nk, :])
        ej = lax.broadcasted_iota(jnp.int32, (LANES, nk), 0)
        ep = lax.broadcasted_iota(jnp.int32, (LANES, nk), 1)
        expand = jnp.where(ej == jnp.right_shift(ep, SLC_SHIFT), 1.0, 0.0).astype(BF16)
        keep = _dot(sel, expand) > 0.5
        probs, dens = [], []
        for rr in range(NSA_REP):
            s = head_rows(s_all, rr) + strip_s[rr, :, strip_start:strip_start + nk]
            p, l = softmax_rows(jnp.where(keep, s, NEG))
            probs.append(_bf(p))
            dens.append(l)
        pv = _dot(jnp.concatenate(probs, axis=0), kv_s[1, 0:nk, :])
        for rr in range(NSA_REP):
            oslc_s[u, rr] = head_rows(pv, rr) / dens[rr]

    fronts = [front(step * nsub + u) for u in range(nsub)]
    for s in range(nqb // nsub):
        @pl.when(step == s)
        def _(s=s):
            for u in range(nsub):
                selected(u, s * nsub + u, fronts[u][0], fronts[u][1])

    for u in range(nsub):
        _, _, acc, gate_slc = fronts[u]
        for rr in range(NSA_REP):
            out = acc[rr] + gate_slc[rr] * oslc_s[u, rr]
            o_ref[u * qb:(u + 1) * qb, rr * HEAD_DIM:(rr + 1) * HEAD_DIM] = out.astype(o_ref.dtype)


def _nsa(proj, cmp_kv, rel_bias):
    bsz, seq, _ = proj.shape
    ncmp = cmp_kv.shape[3]
    nqb = seq // Q_BLOCK
    strip_w = seq
    wtile_w = WIN + WIN + Q_BLOCK
    col = lambda fn: pl.BlockSpec((None, seq, LANES), lambda g, b, i: (b, 0, fn(g)))
    kv = lambda branch, which: col(lambda g: CB_NSA_KV + (branch * 2 + which) * NSA_GROUPS + g)
    return pl.pallas_call(
        _nsa_kernel,
        grid=(NSA_GROUPS, bsz, nqb // NSA_BLOCKS_PER_STEP),
        in_specs=[pl.BlockSpec(memory_space=pltpu.SMEM),
                  col(lambda g: CB_NSA_Q + g * NSA_REP), col(lambda g: CB_NSA_Q + g * NSA_REP + 1),
                  col(lambda g: CB_NSA_Q + g * NSA_REP + 2),
                  kv(1, 0), kv(1, 1), kv(2, 0), kv(2, 1),
                  pl.BlockSpec((None, None, 2, ncmp, HEAD_DIM), lambda g, b, i: (b, g, 0, 0, 0)),
                  pl.BlockSpec((None, seq, LANES), lambda g, b, i: (b, 0, CB_SMALL))],
        out_specs=pl.BlockSpec((None, NSA_BLOCKS_PER_STEP * Q_BLOCK, NSA_REP * HEAD_DIM),
                               lambda g, b, i: (b, i, g)),
        out_shape=jax.ShapeDtypeStruct((bsz, seq, NSA_GROUPS * NSA_REP * HEAD_DIM), BF16),
        scratch_shapes=[pltpu.VMEM((NSA_REP, Q_BLOCK, strip_w), F32),
                        pltpu.VMEM((NSA_REP, seq, ncmp), F32),
                        pltpu.VMEM((NSA_REP, Q_BLOCK, wtile_w), F32),
                        pltpu.VMEM((NSA_REP, seq, HEAD_DIM), BF16),
                        pltpu.VMEM((4, seq, HEAD_DIM), BF16),
                        pltpu.VMEM((NSA_BLOCKS_PER_STEP, NSA_REP, Q_BLOCK, HEAD_DIM), F32)],
        compiler_params=_params("arbitrary", "arbitrary", "arbitrary"),
        name="nsa_attention",
    )(rel_bias, proj, proj, proj, proj, proj, proj, proj, cmp_kv, proj)


IN_WIDE_A = 3072
IN_SMALL_A = 12
IN_WIDE_B = 3840
IN_SMALL_B = 18


def _w_in_prep_kernel(a_ref, b_ref, c_ref, o_ref):
    j = pl.program_id(0)
    depth = o_ref.shape[0]
    blk = o_ref.shape[1]
    n_a = IN_WIDE_A // blk
    n_b = IN_WIDE_B // blk
    s = IN_SMALL_A

    def put(lo, hi, src_ref, src_lo):
        for l in range(depth):
            o_ref[l, lo:hi, :] = src_ref[src_lo:src_lo + hi - lo, l, :].astype(o_ref.dtype)

    @pl.when(j < n_a)
    def _():
        put(0, blk, a_ref, 0)

    @pl.when(jnp.logical_and(j >= n_a, j < n_a + n_b))
    def _():
        put(0, blk - s, a_ref, s)
        put(blk - s, blk, b_ref, 0)

    @pl.when(j == n_a + n_b)
    def _():
        put(0, s, c_ref, 0)
        put(s, s + IN_SMALL_B, a_ref, s)
        o_ref[:, s + IN_SMALL_B:, :] = jnp.zeros((depth, blk - s - IN_SMALL_B, o_ref.shape[2]), o_ref.dtype)

    @pl.when(j > n_a + n_b)
    def _():
        o_ref[...] = jnp.zeros(o_ref.shape, o_ref.dtype)


def _prep_w_in(w):
    depth, d, cols = w.shape
    blk = LANES
    assert cols == IN_WIDE_A + IN_SMALL_A + IN_WIDE_B + IN_SMALL_B
    wt = jnp.transpose(w, (2, 0, 1))
    last = (cols - 1) // blk
    nxt = 2 * SUBLANES
    assert IN_SMALL_A <= nxt and blk % nxt == 0 and IN_WIDE_A % nxt == 0
    last_nxt = (cols - 1) // nxt
    return pl.pallas_call(
        _w_in_prep_kernel,
        grid=(PROJ_COLS // blk,),
        in_specs=[pl.BlockSpec((blk, depth, d), lambda j: (jnp.minimum(j, last), 0, 0)),
                  pl.BlockSpec((nxt, depth, d), lambda j: (jnp.minimum((j + 1) * (blk // nxt), last_nxt), 0, 0)),
                  pl.BlockSpec((nxt, depth, d), lambda j: (IN_WIDE_A // nxt, 0, 0))],
        out_specs=pl.BlockSpec((depth, blk, d), lambda j: (0, j, 0)),
        out_shape=jax.ShapeDtypeStruct((depth, PROJ_COLS, d), BF16),
        compiler_params=_params("parallel"),
        name="w_in_prep",
    )(wt, wt, wt)


_DENSE_TILES = {
    "in_proj": (1024, 1792),
    "out_proj": 1024,
    "ffn_up": (1024, 512),
    "ffn_down": (1024, 512),
    "final_norm": 512,
}


def kernel(x, norm1_g, w_in, dn_conv, dn_a_log, dn_dt_bias, dn_norm_g, cmp_pos, cmp_w1, cmp_w2, w_out,
           norm2_g, ffn_up, ffn_conv, ffn_down, rel_bias, final_g):
    bsz, seq, d = x.shape
    depth = w_in.shape[0]
    w_in_b = _prep_w_in(w_in)
    cmp_w1_b, cmp_w2_b = cmp_w1.astype(BF16), cmp_w2.astype(BF16)
    w_out_b, ffn_up_b, ffn_down_b = w_out.astype(BF16), ffn_up.astype(BF16), ffn_down.astype(BF16)
    xf = x.reshape(bsz * seq, d)
    t = _DENSE_TILES
    for l in range(depth):
        proj = _norm_matmul(xf, norm1_g[l], w_in_b, l, *t["in_proj"], F32).reshape(bsz, seq, PROJ_COLS)
        o_dn = _deltanet(proj, dn_conv[l], dn_a_log[l], dn_dt_bias[l], dn_norm_g[l])
        o_dil = _dilated(proj, rel_bias)
        cmp_kv = _compress(proj, cmp_pos[l], cmp_w1_b[l], cmp_w2_b[l])
        o_nsa = _nsa(proj, cmp_kv, rel_bias)
        xf = _out_proj(o_dn.reshape(bsz * seq, -1), o_dil.reshape(bsz * seq, -1), o_nsa.reshape(bsz * seq, -1),
                       w_out_b[l], xf, t["out_proj"])
        act = _ffn_up(xf, norm2_g[l], ffn_up_b, ffn_conv, l, seq, *t["ffn_up"])
        xf = _ffn_down(act, ffn_down_b, l, xf, *t["ffn_down"])
    return _rmsnorm(xf, final_g, t["final_norm"]).reshape(bsz, seq, d)
```

```python
import functools
import math

import numpy as np
import jax
import jax.numpy as jnp
from jax import lax
from jax.experimental import pallas as pl
from jax.experimental.pallas import tpu as pltpu

F32 = jnp.float32
BF16 = jnp.bfloat16
HIGHEST = lax.Precision.HIGHEST

LANES = 128
SUBLANES = 8
HEAD_DIM = 128
DN_HEADS = 6
DIL_HEADS = 4
NSA_GROUPS = 2
NSA_REP = 3
DN_CHUNK = 128
DN_SUB = 16
DN_HEADS_PER_STEP = 3
CMP_LEN = 32
CMP_STRIDE = 16
CMP_HIDDEN = 256
SLC_BLOCK = 64
SLC_SHIFT = 6
SLC_TOPK = 16
NSA_BLOCKS_PER_STEP = 4
WIN = 512
Q_BLOCK = 128
DIL_PATTERNS = ((128, 1), (512, 4), (2048, 16))
REL_BUCKETS = 32
REL_MAX_DIST = 2048
EPS = 1e-6
NEG = -1e30
VMEM_LIMIT = 56 * 1024 * 1024

CB_DN_Q, CB_DN_K, CB_DN_V, CB_DN_Z = 0, 6, 12, 18
CB_DIL_Q, CB_DIL_K, CB_DIL_V = 24, 28, 32
CB_NSA_Q = 36
CB_NSA_KV = 42
CB_SMALL = 54
PROJ_COLS = 7168
LANE_B, LANE_A, LANE_GATE = 0, 6, 12


def _bucket_thresholds():
    n = np.arange(0, REL_MAX_DIST + 1)
    max_exact = REL_BUCKETS // 2
    out = []
    for dt in (np.float32, np.float64):
        nf = np.maximum(n, 1).astype(dt)
        large = max_exact + (np.log(nf / dt(max_exact)) / dt(math.log(REL_MAX_DIST / max_exact))
                             * dt(REL_BUCKETS - max_exact)).astype(np.int32)
        large = np.minimum(large, REL_BUCKETS - 1)
        out.append(np.where(n < max_exact, n, large))
    assert np.array_equal(out[0], out[1])
    bucket = out[1]
    assert np.all(np.diff(bucket) >= 0)
    thr = [0]
    for b in range(1, REL_BUCKETS):
        idx = np.nonzero(bucket >= b)[0]
        thr.append(int(idx[0]) if idx.size else REL_MAX_DIST + 1)
    return tuple(thr)


_THR = _bucket_thresholds()


def _bias_of_dist(dist, tab_ref, head):
    val = jnp.full(dist.shape, tab_ref[head, 0], F32)
    for b in range(1, REL_BUCKETS):
        val = jnp.where(dist >= _THR[b], tab_ref[head, b], val)
    return val


def _sigmoid(x):
    return 1.0 / (1.0 + jnp.exp(-x))


def _bf(x):
    return x.astype(BF16)


def _dot(a, b, **kw):
    return jnp.dot(a, b, preferred_element_type=F32, **kw)


def _dot_nt(a, b, **kw):
    return lax.dot_general(a, b, (((1,), (1,)), ((), ())), preferred_element_type=F32, **kw)


def _bmm(a, b):
    return jnp.einsum('nij,njk->nik', _bf(a), _bf(b), preferred_element_type=F32)


def _bmm_nt(a, b):
    return jnp.einsum('nid,njd->nij', _bf(a), _bf(b), preferred_element_type=F32)


def _params(*sem):
    return pltpu.CompilerParams(dimension_semantics=sem, vmem_limit_bytes=VMEM_LIMIT)


def _norm_matmul_kernel(x_ref, g_ref, w_ref, o_ref, h_ref):
    @pl.when(pl.program_id(1) == 0)
    def _():
        x = x_ref[...]
        ms = jnp.mean(x * x, axis=-1, keepdims=True)
        h_ref[...] = (x * lax.rsqrt(ms + EPS) * g_ref[...]).astype(BF16)

    o_ref[...] = _dot_nt(h_ref[...], w_ref[...]).astype(o_ref.dtype)


def _norm_matmul(x, g, w_t, layer, tm, tn, out_dtype):
    m, d = x.shape
    n = w_t.shape[1]
    return pl.pallas_call(
        _norm_matmul_kernel,
        grid=(m // tm, n // tn),
        in_specs=[pl.BlockSpec((tm, d), lambda i, j: (i, 0)),
                  pl.BlockSpec((1, d), lambda i, j: (0, 0)),
                  pl.BlockSpec((None, tn, d), lambda i, j: (layer, j, 0))],
        out_specs=pl.BlockSpec((tm, tn), lambda i, j: (i, j)),
        out_shape=jax.ShapeDtypeStruct((m, n), out_dtype),
        scratch_shapes=[pltpu.VMEM((tm, d), BF16)],
        compiler_params=_params("parallel", "arbitrary"),
        name="in_proj",
    )(x, g.reshape(1, d), w_t)


def _out_proj_kernel(a_ref, b_ref, c_ref, wa_ref, wb_ref, wc_ref, x_ref, o_ref):
    acc = _dot(a_ref[...], wa_ref[...])
    acc = acc + _dot(b_ref[...], wb_ref[...])
    acc = acc + _dot(c_ref[...], wc_ref[...])
    o_ref[...] = x_ref[...] + acc


def _out_proj(o_dn, o_dil, o_nsa, w_out, x, tm):
    m, d = x.shape
    ka, kb, kc = o_dn.shape[1], o_dil.shape[1], o_nsa.shape[1]
    wa, wb, wc = w_out[:ka], w_out[ka:ka + kb], w_out[ka + kb:]
    row = lambda i: (i, 0)
    full = lambda i: (0, 0)
    return pl.pallas_call(
        _out_proj_kernel,
        grid=(m // tm,),
        in_specs=[pl.BlockSpec((tm, ka), row), pl.BlockSpec((tm, kb), row), pl.BlockSpec((tm, kc), row),
                  pl.BlockSpec((ka, d), full), pl.BlockSpec((kb, d), full), pl.BlockSpec((kc, d), full),
                  pl.BlockSpec((tm, d), row)],
        out_specs=pl.BlockSpec((tm, d), row),
        out_shape=jax.ShapeDtypeStruct((m, d), F32),
        compiler_params=_params("parallel"),
        name="out_proj",
    )(o_dn, o_dil, o_nsa, wa, wb, wc, x)


def _ffn_up_kernel(x_ref, xp_ref, g_ref, wg_ref, wv_ref, cg_ref, cv_ref, o_ref, h_ref, *, tiles_per_seq):
    tm = x_ref.shape[0]
    halo = xp_ref.shape[0]

    @pl.when(pl.program_id(1) == 0)
    def _():
        def norm(x):
            ms = jnp.mean(x * x, axis=-1, keepdims=True)
            return (x * lax.rsqrt(ms + EPS) * g_ref[...]).astype(BF16)
        first = (pl.program_id(0) % tiles_per_seq) == 0
        hp = norm(xp_ref[...])
        h_ref[0:halo, :] = jnp.where(first, jnp.zeros_like(hp), hp)
        h_ref[halo:halo + tm, :] = norm(x_ref[...])

    h = h_ref[...]

    def conv(w_ref, c_ref):
        u = _dot(h, w_ref[...])
        y = u * c_ref[2:3, :]
        y = y + pltpu.roll(u, 1, axis=0) * c_ref[1:2, :]
        y = y + pltpu.roll(u, 2, axis=0) * c_ref[0:1, :]
        return y[halo:, :]

    gate = conv(wg_ref, cg_ref)
    val = conv(wv_ref, cv_ref)
    o_ref[...] = (gate * _sigmoid(gate) * val).astype(o_ref.dtype)


def _ffn_up(x, g, w_up, conv_w, layer, seq, tm, tn):
    m, d = x.shape
    f = w_up.shape[2] // 2
    taps = conv_w.shape[1]
    halo = SUBLANES
    assert taps - 1 <= halo
    nf = f // tn
    kern = functools.partial(_ffn_up_kernel, tiles_per_seq=seq // tm)
    return pl.pallas_call(
        kern,
        grid=(m // tm, nf),
        in_specs=[pl.BlockSpec((tm, d), lambda i, j: (i, 0)),
                  pl.BlockSpec((halo, d), lambda i, j: (jnp.maximum(i * (tm // halo) - 1, 0), 0)),
                  pl.BlockSpec((1, d), lambda i, j: (0, 0)),
                  pl.BlockSpec((None, d, tn), lambda i, j: (layer, 0, j)),
                  pl.BlockSpec((None, d, tn), lambda i, j: (layer, 0, j + nf)),
                  pl.BlockSpec((None, taps, tn), lambda i, j: (layer, 0, j)),
                  pl.BlockSpec((None, taps, tn), lambda i, j: (layer, 0, j + nf))],
        out_specs=pl.BlockSpec((tm, tn), lambda i, j: (i, j)),
        out_shape=jax.ShapeDtypeStruct((m, f), BF16),
        scratch_shapes=[pltpu.VMEM((tm + halo, d), BF16)],
        compiler_params=_params("parallel", "arbitrary"),
        name="ffn_up",
    )(x, x, g.reshape(1, d), w_up, w_up, conv_w, conv_w)


def _ffn_down_kernel(a_ref, w_ref, x_ref, o_ref):
    o_ref[...] = x_ref[...] + _dot(a_ref[...], w_ref[...])


def _ffn_down(act, w_down, layer, x, tm, tn):
    m, f = act.shape
    d = w_down.shape[2]
    return pl.pallas_call(
        _ffn_down_kernel,
        grid=(d // tn, m // tm),
        in_specs=[pl.BlockSpec((tm, f), lambda j, i: (i, 0)),
                  pl.BlockSpec((None, f, tn), lambda j, i: (layer, 0, j)),
                  pl.BlockSpec((tm, tn), lambda j, i: (i, j))],
        out_specs=pl.BlockSpec((tm, tn), lambda j, i: (i, j)),
        out_shape=jax.ShapeDtypeStruct((m, d), F32),
        compiler_params=_params("parallel", "parallel"),
        name="ffn_down",
    )(act, w_down, x)


def _rmsnorm_kernel(x_ref, g_ref, o_ref):
    x = x_ref[...]
    ms = jnp.mean(x * x, axis=-1, keepdims=True)
    o_ref[...] = x * lax.rsqrt(ms + EPS) * g_ref[...]


def _rmsnorm(x, g, tm):
    m, d = x.shape
    return pl.pallas_call(
        _rmsnorm_kernel,
        grid=(m // tm,),
        in_specs=[pl.BlockSpec((tm, d), lambda i: (i, 0)), pl.BlockSpec((1, d), lambda i: (0, 0))],
        out_specs=pl.BlockSpec((tm, d), lambda i: (i, 0)),
        out_shape=jax.ShapeDtypeStruct((m, d), F32),
        compiler_params=_params("parallel"),
        name="final_norm",
    )(x, g.reshape(1, d))


def _dn_prepare(h, ls, alog_ref, dtb_ref, q_ref, k_ref, v_ref, s_ref, cq_ref, ck_ref, cv_ref,
                lhs_s, n_s, op_s, egl_s):
    seq = q_ref.shape[0]
    c = DN_CHUNK
    nc = seq // c

    def conv_silu(x_ref, c_ref):
        taps = c_ref.shape[0]

        def conv(x, mask_rows):
            acc = x * c_ref[taps - 1:taps, ls]
            for s in range(1, taps):
                xs = pltpu.roll(x, s, axis=0)
                if mask_rows is not None:
                    xs = jnp.where(mask_rows >= s, xs, 0.0)
                acc = acc + xs * c_ref[taps - 1 - s:taps - s, ls]
            return acc

        x = x_ref[:, ls]
        first = conv(x[0:SUBLANES, :], lax.broadcasted_iota(jnp.int32, (SUBLANES, 1), 0))
        acc = jnp.concatenate([first, conv(x, None)[SUBLANES:, :]], axis=0)
        return acc * _sigmoid(acc)

    def l2norm(x):
        return x * lax.rsqrt(jnp.sum(x * x, axis=-1, keepdims=True) + EPS)

    q = l2norm(conv_silu(q_ref, cq_ref)) * (HEAD_DIM ** -0.5)
    k = l2norm(conv_silu(k_ref, ck_ref))
    v = conv_silu(v_ref, cv_ref)

    lane = lax.broadcasted_iota(jnp.int32, (1, LANES), 1)
    sm = s_ref[...]
    b_col = jnp.sum(jnp.where(lane == LANE_B + h, sm, 0.0), axis=-1, keepdims=True)
    a_col = jnp.sum(jnp.where(lane == LANE_A + h, sm, 0.0), axis=-1, keepdims=True)
    beta = _sigmoid(b_col)
    ap = a_col + dtb_ref[h]
    softplus = jnp.maximum(ap, 0.0) + jnp.log1p(jnp.exp(-jnp.abs(ap)))
    neg_rate = -jnp.exp(jnp.full((1, 1), alog_ref[h], F32))
    g = jnp.broadcast_to(neg_rate * softplus, (seq, LANES))

    ii = lax.broadcasted_iota(jnp.int32, (c, c), 0)
    jj = lax.broadcasted_iota(jnp.int32, (c, c), 1)
    tri = jnp.broadcast_to(jnp.where(ii >= jj, 1.0, 0.0).astype(F32), (nc, c, c))
    gc3 = jnp.einsum('nij,njk->nik', tri, g.reshape(nc, c, LANES),
                     preferred_element_type=F32, precision=HIGHEST)
    gc = gc3.reshape(seq, LANES)
    egc = jnp.exp(gc)
    glast3 = jnp.broadcast_to(gc3[:, c - 1:c, :], (nc, c, LANES))
    kb = k * beta
    k3 = k.reshape(nc, c, HEAD_DIM)
    kdt = jnp.swapaxes(k3 * jnp.exp(glast3 - gc3), 1, 2)
    egl_s[...] = jnp.exp(glast3[:, 0:SUBLANES, :])

    gcol = gc3[:, :, 0:c]
    grow = jnp.swapaxes(gc3, 1, 2)[:, 0:c, :]
    decay = jnp.where(ii >= jj, jnp.exp(jnp.minimum(gcol - grow, 0.0)), 0.0)

    attn = _bmm_nt(q.reshape(nc, c, HEAD_DIM), k3) * decay
    a_mat = jnp.where(ii > jj, _bmm_nt(kb.reshape(nc, c, HEAD_DIM), k3) * decay, 0.0)

    same = (ii & -DN_SUB) == (jj & -DN_SUB)
    eye = jnp.where(ii == jj, 1.0, 0.0).astype(F32)
    x1 = jnp.where(same, -a_mat, 0.0)
    a_off = jnp.where(same, 0.0, a_mat)
    t_d = eye + x1
    xp = x1
    p = 2
    while p < DN_SUB:
        xp = _bmm(xp, xp)
        t_d = t_d + _bmm(t_d, xp)
        p *= 2
    y1 = -_bmm(t_d, a_off)
    t_o = eye + y1
    yp = y1
    p = 2
    while p < c // DN_SUB:
        yp = _bmm(yp, yp)
        t_o = t_o + _bmm(t_o, yp)
        p *= 2
    t_mat = _bmm(t_o, t_d)

    u = _bmm(t_mat, (v * beta).reshape(nc, c, HEAD_DIM))
    w = _bmm(t_mat, (kb * egc).reshape(nc, c, HEAD_DIM))

    lhs_s[:, 0:HEAD_DIM, :] = _bf(-_bmm(kdt, w))
    lhs_s[:, HEAD_DIM:HEAD_DIM + c, :] = _bf((q * egc).reshape(nc, c, HEAD_DIM) - _bmm(attn, w))
    n_s[...] = _bmm(kdt, u)
    op_s[...] = _bmm(attn, u)


def _dn_kernel(alog_ref, dtb_ref, q_ref, k_ref, v_ref, z_ref, s_ref, cq_ref, ck_ref, cv_ref, ng_ref,
               o_ref, lhs_s, n_s, op_s, egl_s, o_s):
    hs = DN_HEADS_PER_STEP
    seq = q_ref.shape[0]
    c = DN_CHUNK
    lanes = [slice(u * HEAD_DIM, (u + 1) * HEAD_DIM) for u in range(hs)]
    for u in range(hs):
        _dn_prepare(pl.program_id(1) * hs + u, lanes[u], alog_ref, dtb_ref, q_ref, k_ref, v_ref, s_ref,
                    cq_ref, ck_ref, cv_ref, lhs_s.at[u], n_s.at[u], op_s.at[u], egl_s.at[u])

    def chunk_step(n, states):
        rows = pl.ds(pl.multiple_of(n * c, c), c)
        out = []
        for u in range(hs):
            r = _dot(lhs_s[u, n], _bf(states[u]))
            o_s[u, rows, :] = r[HEAD_DIM:, :] + op_s[u, n]
            out.append(states[u] * egl_s[u, n][0:1, :] + r[0:HEAD_DIM, :] + n_s[u, n])
        return tuple(out)

    lax.fori_loop(0, seq // c, chunk_step, tuple(jnp.zeros((HEAD_DIM, HEAD_DIM), F32) for _ in range(hs)))

    for u in range(hs):
        o = o_s[u]
        y = o * lax.rsqrt(jnp.mean(o * o, axis=-1, keepdims=True) + EPS) * ng_ref[...]
        z = z_ref[:, lanes[u]]
        o_ref[:, lanes[u]] = (y * (z * _sigmoid(z))).astype(o_ref.dtype)


def _deltanet(proj, dn_conv, a_log, dt_bias, norm_g):
    bsz, seq, _ = proj.shape
    nc = seq // DN_CHUNK
    hs = DN_HEADS_PER_STEP
    width = hs * HEAD_DIM
    assert DN_HEADS % hs == 0 and all(cb % hs == 0 for cb in (CB_DN_Q, CB_DN_K, CB_DN_V, CB_DN_Z))
    col = lambda off: pl.BlockSpec((None, seq, width), lambda b, p: (b, 0, off // hs + p))
    cw = lambda off: pl.BlockSpec((dn_conv.shape[0], width), lambda b, p: (0, off // hs + p))
    smem = pl.BlockSpec(memory_space=pltpu.SMEM)
    return pl.pallas_call(
        _dn_kernel,
        grid=(bsz, DN_HEADS // hs),
        in_specs=[smem, smem, col(CB_DN_Q), col(CB_DN_K), col(CB_DN_V), col(CB_DN_Z),
                  pl.BlockSpec((None, seq, LANES), lambda b, p: (b, 0, CB_SMALL)),
                  cw(CB_DN_Q), cw(CB_DN_K), cw(CB_DN_V),
                  pl.BlockSpec((1, HEAD_DIM), lambda b, p: (0, 0))],
        out_specs=pl.BlockSpec((None, seq, width), lambda b, p: (b, 0, p)),
        out_shape=jax.ShapeDtypeStruct((bsz, seq, DN_HEADS * HEAD_DIM), BF16),
        scratch_shapes=[pltpu.VMEM((hs, nc, HEAD_DIM + DN_CHUNK, HEAD_DIM), BF16),
                        pltpu.VMEM((hs, nc, HEAD_DIM, HEAD_DIM), F32),
                        pltpu.VMEM((hs, nc, DN_CHUNK, HEAD_DIM), F32),
                        pltpu.VMEM((hs, nc, SUBLANES, LANES), F32),
                        pltpu.VMEM((hs, seq, HEAD_DIM), F32)],
        compiler_params=_params("parallel", "parallel"),
        name="deltanet",
    )(a_log, dt_bias, proj, proj, proj, proj, proj, dn_conv, dn_conv, dn_conv, norm_g.reshape(1, HEAD_DIM))


def _dil_kernel(tab_ref, q_ref, k_ref, v_ref, o_ref, tile_s, perm_s, back_s, nat_s):
    head = pl.program_id(0)
    seq = q_ref.shape[0]
    qb = Q_BLOCK
    scale = HEAD_DIM ** -0.5

    @pl.when(pl.program_id(1) == 0)
    def _():
        r = lax.broadcasted_iota(jnp.int32, (qb, 2 * qb), 0)
        cidx = lax.broadcasted_iota(jnp.int32, (qb, 2 * qb), 1)
        sub = qb + r - cidx
        for pat, (window, dil) in enumerate(DIL_PATTERNS):
            valid = (sub >= 0) & (sub <= window // dil)
            tile_s[pat] = jnp.where(valid, _bias_of_dist(jnp.maximum(sub, 0) * dil, tab_ref, head), NEG)

    st = DIL_PATTERNS[1][1]
    nb = seq // qb
    grp = seq // st
    per_class = grp // qb
    assert [d for _, d in DIL_PATTERNS] == [1, st, st * st] and all(w // d == qb for w, d in DIL_PATTERNS)
    assert grp // st == qb and per_class & (per_class - 1) == 0

    def level1_rows(rho):
        return pl.ds(rho, grp, stride=st), slice(rho * grp, (rho + 1) * grp)

    def level2_rows(c):
        rho, sigma = divmod(c, st)
        return pl.ds(rho * grp + sigma, qb, stride=st), slice(c * qb, (c + 1) * qb)

    for n, ref in enumerate((q_ref, k_ref, v_ref)):
        for rho in range(st):
            walk, dense = level1_rows(rho)
            perm_s[0, n, dense, :] = ref[walk, :]
        for c in range(st * st):
            walk, dense = level2_rows(c)
            perm_s[1, n, dense, :] = perm_s[0, n, walk, :]

    unit = lax.broadcasted_iota(jnp.int32, (nb, 1, 1), 0)

    def attend(pat, q, k, v, has_prev):
        q3 = _bf(q * scale).reshape(nb, qb, HEAD_DIM)
        k3 = _bf(k).reshape(nb, qb, HEAD_DIM)
        v3 = _bf(v).reshape(nb, qb, HEAD_DIM)
        tile = tile_s[pat]
        logits = [_bmm_nt(q3, k3) + tile[:, qb:]]
        values = [v3]
        if has_prev is not None:
            shift = lambda x: jnp.concatenate([jnp.zeros_like(x[:1]), x[:-1]], axis=0)
            logits.append(jnp.where(has_prev, _bmm_nt(q3, shift(k3)) + tile[:, :qb], NEG))
            values.append(shift(v3))
        m = functools.reduce(jnp.maximum, [jnp.max(s, axis=-1, keepdims=True) for s in logits])
        probs = [jnp.exp(s - m) for s in logits]
        l = sum(jnp.sum(p, axis=-1, keepdims=True) for p in probs)
        num = sum(_bmm(p, v_) for p, v_ in zip(probs, values))
        wide = lambda x: jnp.broadcast_to(x, (nb, qb, LANES)).reshape(seq, LANES)
        return num.reshape(seq, HEAD_DIM), wide(m), wide(l)

    res0 = attend(0, q_ref[...], k_ref[...], v_ref[...], unit >= 1)
    res1 = attend(1, perm_s[0, 0], perm_s[0, 1], perm_s[0, 2], (unit & (per_class - 1)) != 0)
    res2 = attend(2, perm_s[1, 0], perm_s[1, 1], perm_s[1, 2], None)

    for n in range(3):
        for c in range(st * st):
            walk, dense = level2_rows(c)
            back_s[n, walk, :] = res2[n][dense, :]
        for rho in range(st):
            walk, dense = level1_rows(rho)
            nat_s[1, n, walk, :] = back_s[n, dense, :]
            nat_s[0, n, walk, :] = res1[n][dense, :]

    parts = [res0, tuple(nat_s[0, n] for n in range(3)), tuple(nat_s[1, n] for n in range(3))]
    m_all = functools.reduce(jnp.maximum, [part[1] for part in parts])
    num = jnp.zeros((seq, HEAD_DIM), F32)
    den = jnp.zeros((seq, LANES), F32)
    for part_num, part_m, part_l in parts:
        wgt = jnp.exp(part_m - m_all)
        num = num + part_num * wgt
        den = den + part_l * wgt
    o_ref[...] = (num / den).astype(o_ref.dtype)


def _dilated(proj, dil_tab):
    bsz, seq, _ = proj.shape
    col = lambda off: pl.BlockSpec((None, seq, LANES), lambda h, b: (b, 0, off + h))
    return pl.pallas_call(
        _dil_kernel,
        grid=(DIL_HEADS, bsz),
        in_specs=[pl.BlockSpec(memory_space=pltpu.SMEM), col(CB_DIL_Q), col(CB_DIL_K), col(CB_DIL_V)],
        out_specs=pl.BlockSpec((None, seq, HEAD_DIM), lambda h, b: (b, 0, h)),
        out_shape=jax.ShapeDtypeStruct((bsz, seq, DIL_HEADS * HEAD_DIM), BF16),
        scratch_shapes=[pltpu.VMEM((len(DIL_PATTERNS), Q_BLOCK, 2 * Q_BLOCK), F32),
                        pltpu.VMEM((2, 3, seq, HEAD_DIM), F32),
                        pltpu.VMEM((3, seq, LANES), F32),
                        pltpu.VMEM((2, 3, seq, LANES), F32)],
        compiler_params=_params("arbitrary", "arbitrary"),
        name="dilated",
    )(dil_tab, proj, proj, proj)


def _cmp_kernel(x_ref, pos_ref, w1_ref, w2_ref, o_ref):
    nblk = o_ref.shape[0]
    half = CMP_LEN // 2
    ha = jnp.zeros((nblk, CMP_HIDDEN), F32)
    hb = jnp.zeros((nblk, CMP_HIDDEN), F32)
    for l in range(half):
        xl = x_ref[pl.ds(l, nblk, stride=CMP_STRIDE), :]
        wa = w1_ref[l * HEAD_DIM:(l + 1) * HEAD_DIM, :]
        wb = w1_ref[(half + l) * HEAD_DIM:(half + l + 1) * HEAD_DIM, :]
        ha = ha + _dot((xl + pos_ref[l:l + 1, :]).astype(BF16), wa)
        hb = hb + _dot((xl + pos_ref[half + l:half + l + 1, :]).astype(BF16), wb)
    hmid = ha + pltpu.roll(hb, nblk - 1, axis=0)
    hmid = 0.5 * hmid * (1.0 + jnp.tanh(math.sqrt(2.0 / math.pi) * (hmid + 0.044715 * hmid * hmid * hmid)))
    out = _dot(hmid.astype(BF16), w2_ref[...])
    rowi = lax.broadcasted_iota(jnp.int32, out.shape, 0)
    o_ref[...] = jnp.where(rowi < nblk - 1, out, 0.0)


def _compress(proj, cmp_pos, w1, w2):
    bsz, seq, _ = proj.shape
    nblk = seq // CMP_STRIDE
    return pl.pallas_call(
        _cmp_kernel,
        grid=(2, bsz, NSA_GROUPS),
        in_specs=[pl.BlockSpec((None, seq, LANES), lambda i, b, g: (b, 0, CB_NSA_KV + i * NSA_GROUPS + g)),
                  pl.BlockSpec((None, CMP_LEN, HEAD_DIM), lambda i, b, g: (i, 0, 0)),
                  pl.BlockSpec((None, CMP_LEN * HEAD_DIM, CMP_HIDDEN), lambda i, b, g: (i, 0, 0)),
                  pl.BlockSpec((None, CMP_HIDDEN, HEAD_DIM), lambda i, b, g: (i, 0, 0))],
        out_specs=pl.BlockSpec((None, None, None, nblk, HEAD_DIM), lambda i, b, g: (b, g, i, 0, 0)),
        out_shape=jax.ShapeDtypeStruct((bsz, NSA_GROUPS, 2, nblk, HEAD_DIM), F32),
        compiler_params=_params("parallel", "parallel", "parallel"),
        name="nsa_compress",
    )(proj, cmp_pos, w1, w2)


def _nsa_kernel(tab_ref, q0_ref, q1_ref, q2_ref, ks_ref, vs_ref, kw_ref, vw_ref, cmp_ref, s_ref,
                o_ref, strip_s, cbias_s, wtile_s, q_s, kv_s, oslc_s):
    grp = pl.program_id(0)
    step = pl.program_id(2)
    seq = q0_ref.shape[0]
    qb = Q_BLOCK
    nsub = o_ref.shape[0] // qb
    nqb = seq // qb
    n_slc = seq // SLC_BLOCK
    ncmp = cmp_ref.shape[1]
    nwin = WIN // qb
    win_keys = WIN + qb
    scale = HEAD_DIM ** -0.5
    q_refs = (q0_ref, q1_ref, q2_ref)
    head_rows = lambda x, rr: x[rr * qb:(rr + 1) * qb, :]

    @pl.when(step == 0)
    def _():
        for rr in range(NSA_REP):
            q_s[rr] = (q_refs[rr][...] * scale).astype(BF16)
        for n, ref in enumerate((ks_ref, vs_ref, kw_ref, vw_ref)):
            kv_s[n] = ref[...].astype(BF16)

    @pl.when(jnp.logical_and(pl.program_id(1) == 0, step == 0))
    def _():
        for rr in range(NSA_REP):
            head = DIL_HEADS + grp * NSA_REP + rr
            shape = strip_s.shape[1:]
            dist = (lax.broadcasted_iota(jnp.int32, shape, 0) - lax.broadcasted_iota(jnp.int32, shape, 1)
                    + (seq - qb))
            strip_s[rr] = jnp.where(dist >= 0, _bias_of_dist(jnp.maximum(dist, 0), tab_ref, head), NEG)
            shape = cbias_s.shape[1:]
            dist = (lax.broadcasted_iota(jnp.int32, shape, 0)
                    - (lax.broadcasted_iota(jnp.int32, shape, 1) * CMP_STRIDE + CMP_LEN - 1))
            cbias_s[rr] = jnp.where(dist >= 0, _bias_of_dist(jnp.maximum(dist, 0), tab_ref, head), NEG)
            shape = wtile_s.shape[1:]
            dist = lax.broadcasted_iota(jnp.int32, shape, 0) + WIN - lax.broadcasted_iota(jnp.int32, shape, 1)
            wtile_s[rr] = jnp.where((dist >= 0) & (dist < WIN),
                                    _bias_of_dist(jnp.clip(dist, 0, WIN), tab_ref, head), NEG)

    lane = lax.broadcasted_iota(jnp.int32, (1, LANES), 1)
    oj = lax.broadcasted_iota(jnp.int32, (n_slc, ncmp), 0) * SLC_BLOCK
    oc = lax.broadcasted_iota(jnp.int32, (n_slc, ncmp), 1) * CMP_STRIDE
    overlap_t = (jnp.maximum(jnp.minimum(oc + CMP_LEN, oj + SLC_BLOCK) - jnp.maximum(oc, oj), 0)
                 .astype(F32) / CMP_STRIDE)
    jrow = lax.broadcasted_iota(jnp.int32, (n_slc, qb), 0)
    tcol = lax.broadcasted_iota(jnp.int32, (n_slc, qb), 1)

    def softmax_rows(s):
        m = jnp.max(s, axis=-1, keepdims=True)
        p = jnp.exp(s - m)
        return p, jnp.sum(p, axis=-1, keepdims=True)

    def front(i):
        rows = pl.ds(pl.multiple_of(i * qb, qb), qb)
        q3 = jnp.concatenate([q_s[rr, rows, :] for rr in range(NSA_REP)], axis=0)
        sm = _sigmoid(s_ref[rows, :])

        def gate(rr, branch):
            ln = LANE_GATE + (grp * NSA_REP + rr) * 3 + branch
            return jnp.sum(jnp.where(lane == ln, sm, 0.0), axis=-1, keepdims=True)

        s_all = _dot_nt(q3, _bf(cmp_ref[0]))
        p_sum = jnp.zeros((qb, ncmp), F32)
        probs = []
        for rr in range(NSA_REP):
            cb = cbias_s[rr, rows, :]
            p, l = softmax_rows(head_rows(s_all, rr) + cb)
            p = jnp.where(cb > 0.5 * NEG, p, 0.0)
            p = p / jnp.maximum(jnp.sum(p, axis=-1, keepdims=True), 1e-30)
            p_sum = p_sum + p
            probs.append(_bf(p))
        o_cmp = _dot(jnp.concatenate(probs, axis=0), _bf(cmp_ref[1]))
        acc = [gate(rr, 0) * head_rows(o_cmp, rr) for rr in range(NSA_REP)]

        imp = _dot_nt(overlap_t, p_sum, precision=HIGHEST)
        cur = jnp.right_shift(i * qb + tcol, SLC_SHIFT)
        forced = (jrow == 0) | ((jrow <= cur) & (jrow > cur - 2))
        imp = jnp.where(forced, jnp.inf, jnp.where(jrow <= cur, imp, -jnp.inf))
        cnt = jnp.zeros((n_slc, qb), F32)
        for j2 in range(n_slc):
            other = imp[j2:j2 + 1, :]
            ge = jnp.where(other >= imp, 1.0, 0.0)
            gt = jnp.where(other > imp, 1.0, 0.0)
            cnt = cnt + jnp.where(jrow > j2, ge, gt)
        sel_t = jnp.where((cnt < SLC_TOPK) & (jrow <= cur), 1.0, 0.0)
        sel_t = jnp.concatenate([sel_t, jnp.zeros((LANES - n_slc, qb), F32)], axis=0)
        sel = _bf(sel_t.T)

        j0 = jnp.maximum(i - nwin, 0)
        krows = pl.ds(pl.multiple_of(j0 * qb, qb), win_keys)
        tile_start = pl.multiple_of((j0 - i + nwin) * qb, qb)
        s_all = _dot_nt(q3, kv_s[2, krows, :])
        probs, dens = [], []
        for rr in range(NSA_REP):
            p, l = softmax_rows(head_rows(s_all, rr) + wtile_s[rr, :, pl.ds(tile_start, win_keys)])
            probs.append(_bf(p))
            dens.append(l)
        pv = _dot(jnp.concatenate(probs, axis=0), kv_s[3, krows, :])
        acc = [acc[rr] + gate(rr, 2) * (head_rows(pv, rr) / dens[rr]) for rr in range(NSA_REP)]
        return q3, sel, acc, [gate(rr, 1) for rr in range(NSA_REP)]

    def selected(u, i, q3, sel):
        nk = (i + 1) * qb
        strip_start = (nqb - 1 - i) * qb
        s_all = _dot_nt(q3, kv_s[0, 0:nk, :])
        ej = lax.broadcasted_iota(jnp.int32, (LANES, nk), 0)
        ep = lax.broadcasted_iota(jnp.int32, (LANES, nk), 1)
        expand = jnp.where(ej == jnp.right_shift(ep, SLC_SHIFT), 1.0, 0.0).astype(BF16)
        keep = _dot(sel, expand) > 0.5
        probs, dens = [], []
        for rr in range(NSA_REP):
            s = head_rows(s_all, rr) + strip_s[rr, :, strip_start:strip_start + nk]
            p, l = softmax_rows(jnp.where(keep, s, NEG))
            probs.append(_bf(p))
            dens.append(l)
        pv = _dot(jnp.concatenate(probs, axis=0), kv_s[1, 0:nk, :])
        for rr in range(NSA_REP):
            oslc_s[u, rr] = head_rows(pv, rr) / dens[rr]

    fronts = [front(step * nsub + u) for u in range(nsub)]
    for s in range(nqb // nsub):
        @pl.when(step == s)
        def _(s=s):
            for u in range(nsub):
                selected(u, s * nsub + u, fronts[u][0], fronts[u][1])

    for u in range(nsub):
        _, _, acc, gate_slc = fronts[u]
        for rr in range(NSA_REP):
            out = acc[rr] + gate_slc[rr] * oslc_s[u, rr]
            o_ref[u * qb:(u + 1) * qb, rr * HEAD_DIM:(rr + 1) * HEAD_DIM] = out.astype(o_ref.dtype)


def _nsa(proj, cmp_kv, rel_bias):
    bsz, seq, _ = proj.shape
    ncmp = cmp_kv.shape[3]
    nqb = seq // Q_BLOCK
    strip_w = seq
    wtile_w = WIN + WIN + Q_BLOCK
    col = lambda fn: pl.BlockSpec((None, seq, LANES), lambda g, b, i: (b, 0, fn(g)))
    kv = lambda branch, which: col(lambda g: CB_NSA_KV + (branch * 2 + which) * NSA_GROUPS + g)
    return pl.pallas_call(
        _nsa_kernel,
        grid=(NSA_GROUPS, bsz, nqb // NSA_BLOCKS_PER_STEP),
        in_specs=[pl.BlockSpec(memory_space=pltpu.SMEM),
                  col(lambda g: CB_NSA_Q + g * NSA_REP), col(lambda g: CB_NSA_Q + g * NSA_REP + 1),
                  col(lambda g: CB_NSA_Q + g * NSA_REP + 2),
                  kv(1, 0), kv(1, 1), kv(2, 0), kv(2, 1),
                  pl.BlockSpec((None, None, 2, ncmp, HEAD_DIM), lambda g, b, i: (b, g, 0, 0, 0)),
                  pl.BlockSpec((None, seq, LANES), lambda g, b, i: (b, 0, CB_SMALL))],
        out_specs=pl.BlockSpec((None, NSA_BLOCKS_PER_STEP * Q_BLOCK, NSA_REP * HEAD_DIM),
                               lambda g, b, i: (b, i, g)),
        out_shape=jax.ShapeDtypeStruct((bsz, seq, NSA_GROUPS * NSA_REP * HEAD_DIM), BF16),
        scratch_shapes=[pltpu.VMEM((NSA_REP, Q_BLOCK, strip_w), F32),
                        pltpu.VMEM((NSA_REP, seq, ncmp), F32),
                        pltpu.VMEM((NSA_REP, Q_BLOCK, wtile_w), F32),
                        pltpu.VMEM((NSA_REP, seq, HEAD_DIM), BF16),
                        pltpu.VMEM((4, seq, HEAD_DIM), BF16),
                        pltpu.VMEM((NSA_BLOCKS_PER_STEP, NSA_REP, Q_BLOCK, HEAD_DIM), F32)],
        compiler_params=_params("arbitrary", "arbitrary", "arbitrary"),
        name="nsa_attention",
    )(rel_bias, proj, proj, proj, proj, proj, proj, proj, cmp_kv, proj)


IN_WIDE_A = 3072
IN_SMALL_A = 12
IN_WIDE_B = 3840
IN_SMALL_B = 18


def _w_in_prep_kernel(a_ref, b_ref, c_ref, o_ref):
    j = pl.program_id(0)
    depth = o_ref.shape[0]
    blk = o_ref.shape[1]
    n_a = IN_WIDE_A // blk
    n_b = IN_WIDE_B // blk
    s = IN_SMALL_A

    def put(lo, hi, src_ref, src_lo):
        for l in range(depth):
            o_ref[l, lo:hi, :] = src_ref[src_lo:src_lo + hi - lo, l, :].astype(o_ref.dtype)

    @pl.when(j < n_a)
    def _():
        put(0, blk, a_ref, 0)

    @pl.when(jnp.logical_and(j >= n_a, j < n_a + n_b))
    def _():
        put(0, blk - s, a_ref, s)
        put(blk - s, blk, b_ref, 0)

    @pl.when(j == n_a + n_b)
    def _():
        put(0, s, c_ref, 0)
        put(s, s + IN_SMALL_B, a_ref, s)
        o_ref[:, s + IN_SMALL_B:, :] = jnp.zeros((depth, blk - s - IN_SMALL_B, o_ref.shape[2]), o_ref.dtype)

    @pl.when(j > n_a + n_b)
    def _():
        o_ref[...] = jnp.zeros(o_ref.shape, o_ref.dtype)


def _prep_w_in(w):
    depth, d, cols = w.shape
    blk = LANES
    assert cols == IN_WIDE_A + IN_SMALL_A + IN_WIDE_B + IN_SMALL_B
    wt = jnp.transpose(w, (2, 0, 1))
    last = (cols - 1) // blk
    nxt = 2 * SUBLANES
    assert IN_SMALL_A <= nxt and blk % nxt == 0 and IN_WIDE_A % nxt == 0
    last_nxt = (cols - 1) // nxt
    return pl.pallas_call(
        _w_in_prep_kernel,
        grid=(PROJ_COLS // blk,),
        in_specs=[pl.BlockSpec((blk, depth, d), lambda j: (jnp.minimum(j, last), 0, 0)),
                  pl.BlockSpec((nxt, depth, d), lambda j: (jnp.minimum((j + 1) * (blk // nxt), last_nxt), 0, 0)),
                  pl.BlockSpec((nxt, depth, d), lambda j: (IN_WIDE_A // nxt, 0, 0))],
        out_specs=pl.BlockSpec((depth, blk, d), lambda j: (0, j, 0)),
        out_shape=jax.ShapeDtypeStruct((depth, PROJ_COLS, d), BF16),
        compiler_params=_params("parallel"),
        name="w_in_prep",
    )(wt, wt, wt)


_DENSE_TILES = {
    "in_proj": (1024, 1792),
    "out_proj": 1024,
    "ffn_up": (1024, 512),
    "ffn_down": (512, 1024),
    "final_norm": 512,
}


def kernel(x, norm1_g, w_in, dn_conv, dn_a_log, dn_dt_bias, dn_norm_g, cmp_pos, cmp_w1, cmp_w2, w_out,
           norm2_g, ffn_up, ffn_conv, ffn_down, rel_bias, final_g):
    bsz, seq, d = x.shape
    depth = w_in.shape[0]
    w_in_b = _prep_w_in(w_in)
    cmp_w1_b, cmp_w2_b = cmp_w1.astype(BF16), cmp_w2.astype(BF16)
    w_out_b, ffn_up_b, ffn_down_b = w_out.astype(BF16), ffn_up.astype(BF16), ffn_down.astype(BF16)
    xf = x.reshape(bsz * seq, d)
    t = _DENSE_TILES
    for l in range(depth):
        proj = _norm_matmul(xf, norm1_g[l], w_in_b, l, *t["in_proj"], F32).reshape(bsz, seq, PROJ_COLS)
        o_dn = _deltanet(proj, dn_conv[l], dn_a_log[l], dn_dt_bias[l], dn_norm_g[l])
        o_dil = _dilated(proj, rel_bias)
        cmp_kv = _compress(proj, cmp_pos[l], cmp_w1_b[l], cmp_w2_b[l])
        o_nsa = _nsa(proj, cmp_kv, rel_bias)
        xf = _out_proj(o_dn.reshape(bsz * seq, -1), o_dil.reshape(bsz * seq, -1), o_nsa.reshape(bsz * seq, -1),
                       w_out_b[l], xf, t["out_proj"])
        act = _ffn_up(xf, norm2_g[l], ffn_up_b, ffn_conv, l, seq, *t["ffn_up"])
        xf = _ffn_down(act, ffn_down_b, l, xf, *t["ffn_down"])
    return _rmsnorm(xf, final_g, t["final_norm"]).reshape(bsz, seq, d)
```
